```python
import math
import jax, jax.numpy as jnp
from jax import lax
import numpy as np

D_MODEL = 2048
BATCH = 2
SEQ = 4096
DEPTH = 2
DEC_BATCH = 16
DEC_SEQ = 16
PAST_LEN = 1024

CHUNK = 64
N_HEADS = 4
HEAD_DIM = 128
BR_WIDTH = N_HEADS * HEAD_DIM
N_BRANCH = 5
A_PREV_CHUNKS = 8
A_WINDOW = A_PREV_CHUNKS * CHUNK
A_BAND = (A_PREV_CHUNKS + 1) * CHUNK
A_REL_CLIP = 128
CONV_W = 4
B_QK = 2 * BR_WIDTH
DIFF_DH = HEAD_DIM // 2
Q_BLOCK = 128
RET_GAMMA_EXP0 = 5.0
ROPE_BASE = 10000.0
N_MEM = 256
EPS = 1e-6
NEG = -1e30
COLS = (('a_q', BR_WIDTH), ('a_k', BR_WIDTH), ('a_v', BR_WIDTH), ('a_z', BR_WIDTH),
        ('b_qk', B_QK), ('b_v', BR_WIDTH), ('b_o', BR_WIDTH), ('b_i', N_HEADS), ('b_f', N_HEADS), ('b_z', BR_WIDTH),
        ('c_q', BR_WIDTH), ('c_k', BR_WIDTH), ('c_v', BR_WIDTH), ('c_z', BR_WIDTH),
        ('d_q', BR_WIDTH), ('d_k', BR_WIDTH), ('d_v', BR_WIDTH), ('d_z', BR_WIDTH),
        ('m_q', BR_WIDTH), ('m_z', BR_WIDTH),
        ('gate', N_BRANCH * D_MODEL))
D_IN = sum(w for _, w in COLS)

kernel_name = 'hybrid_streaming_encoder_step'

f32 = jnp.float32


def rmsnorm(x, g):
    xf = x.astype(f32)
    y = xf * lax.rsqrt(jnp.mean(xf * xf, axis=-1, keepdims=True) + EPS)
    return (y * g.astype(f32)).astype(x.dtype)


def head_rmsnorm(o, g, dt):
    of = o.astype(f32)
    y = of * lax.rsqrt(jnp.mean(of * of, axis=-1, keepdims=True) + EPS)
    y = y * g.reshape(N_HEADS, -1).astype(f32)
    return y.reshape(y.shape[:2] + (-1,)).astype(dt)


def heads(t):
    return t.reshape(t.shape[:-1] + (N_HEADS, HEAD_DIM))


def flat(t):
    return t.reshape(t.shape[:2] + (-1,))


def split_proj(u):
    idx = np.cumsum([w for _, w in COLS])[:-1]
    parts = jnp.split(u, idx, axis=-1)
    return dict(zip([n for n, _ in COLS], parts))


def rope(t, pos):
    dh = t.shape[-1]
    inv = ROPE_BASE ** (-jnp.arange(0, dh, 2, dtype=f32) / dh)
    ang = pos.astype(f32)[:, None] * inv[None, :]
    cos = jnp.cos(ang)[None, :, None, :]
    sin = jnp.sin(ang)[None, :, None, :]
    tf = t.astype(f32)
    t1, t2 = tf[..., :dh // 2], tf[..., dh // 2:]
    return jnp.concatenate([t1 * cos - t2 * sin, t1 * sin + t2 * cos], axis=-1)


def ret_log_gamma():
    return jnp.log(1.0 - 2.0 ** (-RET_GAMMA_EXP0 - jnp.arange(N_HEADS, dtype=f32)))


def mixer_inputs(x, lp, pos, conv_buf):
    B, T, _ = x.shape
    dt = x.dtype
    u = split_proj(rmsnorm(x, lp['norm_g']) @ lp['w_in'])
    ub = jnp.concatenate([conv_buf.astype(dt), u['b_qk']], axis=1)
    conv = lp['conv_b'] + ub[:, 0:T] * lp['conv_w'][0]
    for j in range(1, CONV_W):
        conv = conv + ub[:, j:j + T] * lp['conv_w'][j]
    qb, kb = jnp.split(jax.nn.silu(conv), 2, axis=-1)
    a = (heads(u['a_q']), heads(u['a_k']), heads(u['a_v']))
    b = (heads(qb).astype(f32),
         heads(kb).astype(f32) * HEAD_DIM ** -0.5,
         heads(u['b_v']).astype(f32),
         (u['b_i'] + lp['b_ig']).astype(f32),
         jax.nn.log_sigmoid((u['b_f'] + lp['b_fg']).astype(f32)),
         jax.nn.sigmoid(heads(u['b_o']).astype(f32)))
    c = (u['c_q'].reshape(B, T, N_HEADS, 2, DIFF_DH),
         u['c_k'].reshape(B, T, N_HEADS, 2, DIFF_DH),
         heads(u['c_v']))
    d = (rope(heads(u['d_q']), pos),
         rope(heads(u['d_k']), pos) * HEAD_DIM ** -0.5,
         heads(u['d_v']).astype(f32))
    conv_tail = ub[:, -(CONV_W - 1):]
    return u, a, b, c, d, heads(u['m_q']), conv_tail


def band_attn_prompt(q, k, v, rel_bias):
    B, S, H, dh = q.shape
    nc = S // CHUNK
    qc = q.reshape(B, nc, CHUNK, H, dh)
    pad = jnp.zeros((B, A_WINDOW, H, dh), k.dtype)
    kp = jnp.concatenate([pad, k], axis=1).reshape(B, nc + A_PREV_CHUNKS, CHUNK, H, dh)
    vp = jnp.concatenate([pad.astype(v.dtype), v], axis=1).reshape(B, nc + A_PREV_CHUNKS, CHUNK, H, dh)
    kb = jnp.concatenate([kp[:, j:j + nc] for j in range(A_PREV_CHUNKS + 1)], axis=2)
    vb = jnp.concatenate([vp[:, j:j + nc] for j in range(A_PREV_CHUNKS + 1)], axis=2)
    s = jnp.einsum('bcqhd,bckhd->bchqk', qc, kb).astype(f32) * dh ** -0.5
    qi = jnp.arange(CHUNK)
    kj = jnp.arange(A_BAND)
    rel = (A_WINDOW + qi[:, None]) - kj[None, :]
    bias = rel_bias[:, jnp.clip(rel, -A_REL_CLIP, A_REL_CLIP) + A_REL_CLIP].astype(f32)
    valid = (jnp.arange(nc)[:, None] - A_PREV_CHUNKS + kj[None, :] // CHUNK) >= 0
    s = jnp.where(valid[None, :, None, None, :], s + bias[None, None], NEG)
    p = jax.nn.softmax(s, axis=-1)
    o = jnp.einsum('bchqk,bckhd->bcqhd', p.astype(vb.dtype), vb)
    return o.reshape(B, S, H, dh)


def band_attn_sample(q, k, v, k_cache, v_cache, rel_bias, past):
    T = q.shape[1]
    nrow = k_cache.shape[1]
    kk = jnp.concatenate([k_cache, k], axis=1)
    vv = jnp.concatenate([v_cache, v], axis=1)
    qpos = past + jnp.arange(T)
    kpos = jnp.concatenate([past - nrow + jnp.arange(nrow), qpos])
    rel = qpos[:, None] - kpos[None, :]
    bias = rel_bias[:, jnp.clip(rel, -A_REL_CLIP, A_REL_CLIP) + A_REL_CLIP].astype(f32)
    s = jnp.einsum('bqhd,bkhd->bhqk', q, kk).astype(f32) * HEAD_DIM ** -0.5 + bias[None]
    p = jax.nn.softmax(s, axis=-1)
    return jnp.einsum('bhqk,bkhd->bqhd', p.astype(vv.dtype), vv)


def mlstm_chunk(q, k, v, ig, lf, state):
    C, n, m = state
    T = q.shape[1]
    b = jnp.cumsum(lf, axis=1).swapaxes(1, 2)
    it = ig.swapaxes(1, 2)
    causal = jnp.tril(jnp.ones((T, T), bool))
    dlog = jnp.where(causal, b[..., :, None] - b[..., None, :] + it[..., None, :], -jnp.inf)
    inter = b + m[..., None]
    mt = jnp.maximum(inter, jnp.max(dlog, axis=-1))
    w = jnp.exp(dlog - mt[..., None])
    wi = jnp.exp(inter - mt)
    qk = jnp.einsum('bthd,bshd->bhts', q, k) * w
    num = jnp.einsum('bhts,bshe->bhte', qk, v) + wi[..., None] * jnp.einsum('bthd,bhde->bhte', q, C)
    den = jnp.sum(qk, axis=-1) + wi * jnp.einsum('bthd,bhd->bht', q, n)
    h = num / jnp.maximum(jnp.abs(den), jnp.exp(-mt))[..., None]
    m_new = mt[..., -1]
    ws = jnp.exp(b[..., -1:] - b + it - m_new[..., None])
    dec = jnp.exp(b[..., -1] + m - m_new)
    C_new = dec[..., None, None] * C + jnp.einsum('bhs,bshd,bshe->bhde', ws, k, v)
    n_new = dec[..., None] * n + jnp.einsum('bhs,bshd->bhd', ws, k)
    return h.swapaxes(1, 2), (C_new, n_new, m_new)


def mlstm_prompt(q, k, v, ig, lf):
    B, seq, H, d = q.shape
    nc = seq // CHUNK

    def chunks(t):
        return t.reshape((B, nc, CHUNK) + t.shape[2:]).swapaxes(0, 1)

    state0 = (jnp.zeros((B, H, d, d), f32), jnp.zeros((B, H, d), f32), jnp.zeros((B, H), f32))

    def step(st, xs):
        h, st = mlstm_chunk(xs[0], xs[1], xs[2], xs[3], xs[4], st)
        return st, h

    st, h = lax.scan(step, state0, (chunks(q), chunks(k), chunks(v), chunks(ig), chunks(lf)))
    return h.swapaxes(0, 1).reshape(B, seq, H, d), st


def diff_attend(q, k, v, lam, mask):
    s = jnp.einsum('bqhtd,bkhtd->bthqk', q, k).astype(f32) * DIFF_DH ** -0.5
    if mask is not None:
        s = jnp.where(mask, s, NEG)
    p = jax.nn.softmax(s, axis=-1)
    a = p[:, 0] - lam * p[:, 1]
    return jnp.einsum('bhqk,bkhd->bqhd', a.astype(v.dtype), v)


def diff_attn_prompt(q, k, v, lam):
    B, seq = q.shape[:2]
    nb = seq // Q_BLOCK
    qb = q.reshape((B, nb, Q_BLOCK) + q.shape[2:]).swapaxes(0, 1)
    kchunk = jnp.arange(seq) // CHUNK

    def block(args):
        qblk, bi = args
        qchunk = (bi * Q_BLOCK + jnp.arange(Q_BLOCK)) // CHUNK
        mask = kchunk[None, :] <= qchunk[:, None]
        return diff_attend(qblk, k, v, lam, mask)

    o = lax.map(block, (qb, jnp.arange(nb)))
    return o.swapaxes(0, 1).reshape((B, seq) + o.shape[3:])


def retention_chunk(q, k, v, S):
    T = q.shape[1]
    lg = ret_log_gamma()
    idx = jnp.arange(T, dtype=f32)
    rel = idx[:, None] - idx[None, :]
    decay = jnp.where(rel >= 0, jnp.exp(jnp.maximum(rel, 0.0)[None] * lg[:, None, None]), 0.0)
    att = jnp.einsum('bthd,bshd->bhts', q, k) * decay
    q_dec = q * jnp.exp((idx + 1.0)[:, None] * lg)[None, :, :, None]
    o = jnp.einsum('bhts,bshe->bthe', att, v) + jnp.einsum('bthd,bhde->bthe', q_dec, S)
    k_dec = k * jnp.exp((T - 1.0 - idx)[:, None] * lg)[None, :, :, None]
    S_new = jnp.exp(T * lg)[:, None, None] * S + jnp.einsum('bshd,bshe->bhde', k_dec, v)
    return o, S_new


def retention_prompt(q, k, v):
    B, seq, H, dk = q.shape
    nc = seq // CHUNK

    def chunks(t):
        return t.reshape((B, nc, CHUNK) + t.shape[2:]).swapaxes(0, 1)

    S0 = jnp.zeros((B, H, dk, v.shape[-1]), f32)

    def step(st, xs):
        o, st = retention_chunk(xs[0], xs[1], xs[2], st)
        return st, o

    S_fin, o = lax.scan(step, S0, (chunks(q), chunks(k), chunks(v)))
    return o.swapaxes(0, 1).reshape(B, seq, H, -1), S_fin


def mem_kv(mem, lp):
    mn = rmsnorm(mem, lp['mem_norm_g'])
    return heads(mn @ lp['w_mk']), heads(mn @ lp['w_mv'])


def mem_attend(q, mk, mv):
    s = jnp.einsum('bqhd,bkhd->bhqk', q, mk).astype(f32) * HEAD_DIM ** -0.5
    p = jax.nn.softmax(s, axis=-1)
    return jnp.einsum('bhqk,bkhd->bqhd', p.astype(mv.dtype), mv)


def gated_merge(x, u, outs, lp):
    B, T, _ = x.shape
    gates = jax.nn.sigmoid(u['gate'].reshape(B, T, N_BRANCH, D_MODEL))
    zs = (u['a_z'], u['b_z'], u['c_z'], u['d_z'], u['m_z'])
    merged = None
    for i in range(N_BRANCH):
        term = gates[:, :, i] * ((outs[i] * jax.nn.silu(zs[i])) @ lp['w_branch'][i])
        merged = term if merged is None else merged + term
    return x + merged @ lp['w_out']


def prompt_layer(x, mem, lp):
    B, seq, _ = x.shape
    dt = x.dtype
    pos = jnp.arange(seq)
    u, a, b, c, d, qm, conv_tail = mixer_inputs(x, lp, pos, jnp.zeros((B, CONV_W - 1, B_QK), dt))
    qa, ka, va = a
    qb, kb, vb, ig, lf, og = b
    qc, kc, vc = c
    qd, kd, vd = d
    oa = band_attn_prompt(qa, ka, va, lp['rel_bias'])
    hb, (C, n, m) = mlstm_prompt(qb, kb, vb, ig, lf)
    ob = head_rmsnorm(og * hb, lp['gn_b'], dt)
    oc = head_rmsnorm(diff_attn_prompt(qc, kc, vc, lp['lam']), lp['subln_c'], dt) * (1.0 - lp['lam_init'])
    hd, Sd = retention_prompt(qd, kd, vd)
    od = head_rmsnorm(hd, lp['gn_d'], dt)
    mk, mv = mem_kv(mem, lp)
    om = mem_attend(qm, mk, mv)
    y = gated_merge(x, u, (flat(oa), ob, oc, od, flat(om)), lp)
    rows = min(A_WINDOW, seq)
    state = (ka[:, seq - rows:], va[:, seq - rows:], kc.reshape(B, seq, N_HEADS, HEAD_DIM), vc, mk, mv,
             C.astype(dt), n.astype(dt), m.astype(dt), conv_tail, Sd.astype(dt))
    return y, state


def sample_layer(x, lp, ca_k, ca_v, cc_k, cc_v, cm_k, cm_v, sC, sn, sm, sconv, sS):
    B, T, _ = x.shape
    dt = x.dtype
    past = cc_k.shape[1]
    pos = past + jnp.arange(T)
    u, a, b, c, d, qm, conv_tail = mixer_inputs(x, lp, pos, sconv)
    qa, ka, va = a
    qb, kb, vb, ig, lf, og = b
    qc, kc, vc = c
    qd, kd, vd = d
    oa = band_attn_sample(qa, ka, va, ca_k, ca_v, lp['rel_bias'], past)
    hb, (C, n, m) = mlstm_chunk(qb, kb, vb, ig, lf, (sC.astype(f32), sn.astype(f32), sm.astype(f32)))
    ob = head_rmsnorm(og * hb, lp['gn_b'], dt)
    k_all = jnp.concatenate([cc_k.reshape(B, past, N_HEADS, 2, DIFF_DH), kc], axis=1)
    v_all = jnp.concatenate([cc_v, vc], axis=1)
    oc = head_rmsnorm(diff_attend(qc, k_all, v_all, lp['lam'], None), lp['subln_c'], dt) * (1.0 - lp['lam_init'])
    hd, Sd = retention_chunk(qd, kd, vd, sS.astype(f32))
    od = head_rmsnorm(hd, lp['gn_d'], dt)
    om = mem_attend(qm, cm_k, cm_v)
    y = gated_merge(x, u, (flat(oa), ob, oc, od, flat(om)), lp)
    state = (ka, va, kc.reshape(B, T, N_HEADS, HEAD_DIM), vc,
             C.astype(sC.dtype), n.astype(sn.dtype), m.astype(sm.dtype),
             conv_tail.astype(sconv.dtype), Sd.astype(sS.dtype))
    return y, state


def stack_states(group, i):
    return jnp.stack([st[i] for st in group])


def setup_inputs(seed: int = 0) -> dict:
    key = jax.random.key(seed)
    ks = iter(jax.random.split(key, 48))

    def nrm(shape, scale):
        return jax.random.normal(next(ks), shape, f32) * scale

    a_rows = min(A_WINDOW, PAST_LEN)
    H, dh = N_HEADS, HEAD_DIM
    return {
        'x_prompt': nrm((BATCH, SEQ, D_MODEL), 1.0),
        'x_sample': nrm((DEC_BATCH, DEC_SEQ, D_MODEL), 1.0),
        'mem_prompt': nrm((BATCH, N_MEM, D_MODEL), 1.0),
        'cache_a_k': nrm((DEPTH, DEC_BATCH, a_rows, H, dh), 1.0),
        'cache_a_v': nrm((DEPTH, DEC_BATCH, a_rows, H, dh), 1.0),
        'cache_c_k': nrm((DEPTH, DEC_BATCH, PAST_LEN, H, dh), 1.0),
        'cache_c_v': nrm((DEPTH, DEC_BATCH, PAST_LEN, H, dh), 1.0),
        'cache_mem_k': nrm((DEPTH, DEC_BATCH, N_MEM, H, dh), 1.0),
        'cache_mem_v': nrm((DEPTH, DEC_BATCH, N_MEM, H, dh), 1.0),
        'state_b_C': nrm((DEPTH, DEC_BATCH, H, dh, dh), 0.05),
        'state_b_n': nrm((DEPTH, DEC_BATCH, H, dh), 0.1),
        'state_b_m': nrm((DEPTH, DEC_BATCH, H), 0.5),
        'state_b_conv': nrm((DEPTH, DEC_BATCH, CONV_W - 1, B_QK), 1.0),
        'state_d_S': nrm((DEPTH, DEC_BATCH, H, dh, dh), 0.05),
        'norm_g': 1.0 + nrm((DEPTH, D_MODEL), 0.02),
        'w_in': nrm((DEPTH, D_MODEL, D_IN), D_MODEL ** -0.5),
        'conv_w': nrm((DEPTH, CONV_W, B_QK), CONV_W ** -0.5),
        'conv_b': nrm((DEPTH, B_QK), 0.02),
        'b_ig': nrm((DEPTH, H), 0.1),
        'b_fg': jnp.linspace(3.0, 6.0, H, dtype=f32)[None, :] + nrm((DEPTH, H), 0.1),
        'rel_bias': nrm((DEPTH, H, 2 * A_REL_CLIP + 1), 0.1),
        'lam_q1': nrm((DEPTH, DIFF_DH), 0.1),
        'lam_k1': nrm((DEPTH, DIFF_DH), 0.1),
        'lam_q2': nrm((DEPTH, DIFF_DH), 0.1),
        'lam_k2': nrm((DEPTH, DIFF_DH), 0.1),
        'gn_b': 1.0 + nrm((DEPTH, BR_WIDTH), 0.02),
        'subln_c': 1.0 + nrm((DEPTH, BR_WIDTH), 0.02),
        'gn_d': 1.0 + nrm((DEPTH, BR_WIDTH), 0.02),
        'mem_norm_g': 1.0 + nrm((DEPTH, D_MODEL), 0.02),
        'w_mk': nrm((DEPTH, D_MODEL, BR_WIDTH), D_MODEL ** -0.5),
        'w_mv': nrm((DEPTH, D_MODEL, BR_WIDTH), D_MODEL ** -0.5),
        'w_branch': nrm((DEPTH, N_BRANCH, BR_WIDTH, D_MODEL), BR_WIDTH ** -0.5),
        'w_out': nrm((DEPTH, D_MODEL, D_MODEL), D_MODEL ** -0.5),
        'final_g': 1.0 + nrm((D_MODEL,), 0.02),
    }


def reference(x_prompt, x_sample, mem_prompt, cache_a_k, cache_a_v, cache_c_k, cache_c_v,
              cache_mem_k, cache_mem_v, state_b_C, state_b_n, state_b_m, state_b_conv, state_d_S,
              norm_g, w_in, conv_w, conv_b, b_ig, b_fg, rel_bias, lam_q1, lam_k1, lam_q2, lam_k2,
              gn_b, subln_c, gn_d, mem_norm_g, w_mk, w_mv, w_branch, w_out, final_g):
    xp, xs = x_prompt, x_sample
    sp, ss = [], []
    for l in range(DEPTH):
        lam_init = 0.8 - 0.6 * math.exp(-0.3 * l)
        lam = (jnp.exp(jnp.sum(lam_q1[l].astype(f32) * lam_k1[l].astype(f32)))
               - jnp.exp(jnp.sum(lam_q2[l].astype(f32) * lam_k2[l].astype(f32))) + lam_init)
        lp = dict(norm_g=norm_g[l], w_in=w_in[l], conv_w=conv_w[l], conv_b=conv_b[l],
                  b_ig=b_ig[l], b_fg=b_fg[l], rel_bias=rel_bias[l], lam=lam, lam_init=lam_init,
                  gn_b=gn_b[l], subln_c=subln_c[l], gn_d=gn_d[l], mem_norm_g=mem_norm_g[l],
                  w_mk=w_mk[l], w_mv=w_mv[l], w_branch=w_branch[l], w_out=w_out[l])
        xp, st_p = prompt_layer(xp, mem_prompt, lp)
        xs, st_s = sample_layer(xs, lp, cache_a_k[l], cache_a_v[l], cache_c_k[l], cache_c_v[l],
                                cache_mem_k[l], cache_mem_v[l], state_b_C[l], state_b_n[l],
                                state_b_m[l], state_b_conv[l], state_d_S[l])
        sp.append(st_p)
        ss.append(st_s)
    y_prompt = rmsnorm(xp, final_g)
    y_sample = rmsnorm(xs, final_g)
    p_a_k = stack_states(sp, 0)
    p_a_v = stack_states(sp, 1)
    p_c_k = stack_states(sp, 2)
    p_c_v = stack_states(sp, 3)
    p_mem_k = stack_states(sp, 4)
    p_mem_v = stack_states(sp, 5)
    p_b_C = stack_states(sp, 6)
    p_b_n = stack_states(sp, 7)
    p_b_m = stack_states(sp, 8)
    p_b_conv = stack_states(sp, 9)
    p_d_S = stack_states(sp, 10)
    s_a_k = stack_states(ss, 0)
    s_a_v = stack_states(ss, 1)
    s_c_k = stack_states(ss, 2)
    s_c_v = stack_states(ss, 3)
    s_b_C = stack_states(ss, 4)
    s_b_n = stack_states(ss, 5)
    s_b_m = stack_states(ss, 6)
    s_b_conv = stack_states(ss, 7)
    s_d_S = stack_states(ss, 8)
    return (y_prompt, y_sample, p_a_k, p_a_v, p_c_k, p_c_v, p_mem_k, p_mem_v, p_b_C, p_b_n, p_b_m,
            p_b_conv, p_d_S, s_a_k, s_a_v, s_c_k, s_c_v, s_b_C, s_b_n, s_b_m, s_b_conv, s_d_S)
```

```python
import functools
import math

import jax
import jax.numpy as jnp
import numpy as np
from jax import lax
from jax.experimental import pallas as pl
from jax.experimental.pallas import tpu as pltpu

f32 = jnp.float32
bf16 = jnp.bfloat16

D_MODEL = 2048
N_HEADS = 4
HEAD_DIM = 128
BR_WIDTH = N_HEADS * HEAD_DIM
N_BRANCH = 5
CHUNK = 64
A_PREV_CHUNKS = 8
A_WINDOW = A_PREV_CHUNKS * CHUNK
A_BAND = (A_PREV_CHUNKS + 1) * CHUNK
A_REL_CLIP = 128
CONV_W = 4
B_QK = 2 * BR_WIDTH
DIFF_DH = HEAD_DIM // 2
RET_GAMMA_EXP0 = 5.0
ROPE_BASE = 10000.0
EPS = 1e-6
NEG = -1e30
HEAD_SCALE = HEAD_DIM ** -0.5
DIFF_SCALE = DIFF_DH ** -0.5

U_AQ, U_AK, U_AV, U_AZ = 0, 512, 1024, 1536
U_BQK, U_BV, U_BO, U_BZ = 2048, 3072, 3584, 4096
U_CQ, U_CK, U_CV, U_CZ = 4608, 5120, 5632, 6144
U_DQ, U_DK, U_DV, U_DZ = 6656, 7168, 7680, 8192
U_MQ, U_MZ = 8704, 9216
U_GATES = 9728
GATE_PAD = 128
U_WIDTH = U_GATES + GATE_PAD
W_BI = 4096
W_BZ = 4104
W_GATE = 9736

VMEM_LIMIT_BYTES = 56 * 1024 * 1024
A_QBLOCK = 4 * CHUNK
C_BLOCK = 512
NT_DIMS = (((1,), (1,)), ((), ()))
TN_DIMS = (((0,), (0,)), ((), ()))


def _divisor_tile(n, target, multiple):
    best = None
    for t in range(multiple, min(n, target) + 1, multiple):
        if n % t == 0:
            best = t
    assert best is not None, (n, target, multiple)
    return best


def _params(*sem):
    return pltpu.CompilerParams(dimension_semantics=sem, vmem_limit_bytes=VMEM_LIMIT_BYTES)


def _sigmoid(x):
    return 1.0 / (1.0 + jnp.exp(-x))


def _silu(x):
    return x * _sigmoid(x)


def _log_sigmoid(x):
    return jnp.minimum(x, 0.0) - jnp.log1p(jnp.exp(-jnp.abs(x)))


def _head_norm(y, gain):
    return y * lax.rsqrt(jnp.mean(y * y, axis=-1, keepdims=True) + EPS) * gain


def _rmsnorm_rows(x_ref, g_ref, h_ref, rows):
    tm = x_ref.shape[0]
    for r in range(0, tm, rows):
        x = x_ref[r:r + rows, :]
        ms = jnp.mean(x * x, axis=-1, keepdims=True)
        h_ref[r:r + rows, :] = ((x * lax.rsqrt(ms + EPS)) * g_ref[...]).astype(bf16)


def _norm_matmul_body(x_ref, g_ref, w_ref, o_ref, h_ref, *, rows):
    @pl.when(pl.program_id(1) == 0)
    def _():
        _rmsnorm_rows(x_ref, g_ref, h_ref, rows)

    o_ref[...] = jnp.dot(h_ref[...], w_ref[...], preferred_element_type=f32).astype(o_ref.dtype)


def _norm_matmul(x, g, w, *, tm_target, tn, name):
    m, d = x.shape
    n = w.shape[1]
    tm = _divisor_tile(m, tm_target, 16)
    rows = _divisor_tile(tm, 256, 8)
    assert n % tn == 0
    return pl.pallas_call(
        functools.partial(_norm_matmul_body, rows=rows),
        out_shape=jax.ShapeDtypeStruct((m, n), f32),
        grid=(m // tm, n // tn),
        in_specs=[pl.BlockSpec((tm, d), lambda i, j: (i, 0)),
                  pl.BlockSpec((1, d), lambda i, j: (0, 0)),
                  pl.BlockSpec((d, tn), lambda i, j: (0, j))],
        out_specs=pl.BlockSpec((tm, tn), lambda i, j: (i, j)),
        scratch_shapes=[pltpu.VMEM((tm, d), bf16)],
        compiler_params=_params("parallel", "arbitrary"),
        name=name,
    )(x, g.reshape(1, d), w)


def _band_prompt_body(q_ref, k0_ref, k1_ref, k2_ref, v0_ref, v1_ref, v2_ref, z_ref, bias_ref, o_ref):
    tq = q_ref.shape[0]
    i = pl.program_id(1)
    w_idx = lax.broadcasted_iota(jnp.int32, (tq, 3 * tq), 1)
    valid = (w_idx + (i - 2) * tq) >= 0
    for h in range(N_HEADS):
        sl = slice(h * HEAD_DIM, (h + 1) * HEAD_DIM)
        q = q_ref[:, sl].astype(bf16)
        kw = jnp.concatenate([k0_ref[:, sl], k1_ref[:, sl], k2_ref[:, sl]], axis=0).astype(bf16)
        vw = jnp.concatenate([v0_ref[:, sl], v1_ref[:, sl], v2_ref[:, sl]], axis=0).astype(bf16)
        s = lax.dot_general(q, kw, NT_DIMS, preferred_element_type=f32) * HEAD_SCALE + bias_ref[h]
        s = jnp.where(valid, s, NEG)
        p = jnp.exp(s - jnp.max(s, axis=-1, keepdims=True))
        l = jnp.sum(p, axis=-1, keepdims=True)
        o = jnp.dot(p.astype(bf16), vw, preferred_element_type=f32) / l
        o_ref[:, sl] = (o * _silu(z_ref[:, sl])).astype(bf16)


def _band_bias_prompt(rel_bias):
    r = np.arange(A_QBLOCK)[:, None]
    w = np.arange(3 * A_QBLOCK)[None, :]
    kj = w - CHUNK * (r // CHUNK)
    inside = (kj >= 0) & (kj < A_BAND)
    rel = A_WINDOW + (r % CHUNK) - kj
    idx = np.clip(rel, -A_REL_CLIP, A_REL_CLIP) + A_REL_CLIP
    return jnp.where(jnp.asarray(inside)[None], rel_bias.astype(f32)[:, idx], NEG)


def _band_prompt(u, rel_bias, batch, seq):
    tq = A_QBLOCK
    nqb = seq // tq
    bias = _band_bias_prompt(rel_bias)

    def blk(col, back):
        return pl.BlockSpec((tq, BR_WIDTH),
                            lambda b, i: (b * nqb + jnp.maximum(i - back, 0), col // BR_WIDTH))

    return pl.pallas_call(
        _band_prompt_body,
        out_shape=jax.ShapeDtypeStruct((batch * seq, BR_WIDTH), bf16),
        grid=(batch, nqb),
        in_specs=[blk(U_AQ, 0),
                  blk(U_AK, 2), blk(U_AK, 1), blk(U_AK, 0),
                  blk(U_AV, 2), blk(U_AV, 1), blk(U_AV, 0),
                  blk(U_AZ, 0),
                  pl.BlockSpec((N_HEADS, tq, 3 * tq), lambda b, i: (0, 0, 0))],
        out_specs=pl.BlockSpec((tq, BR_WIDTH), lambda b, i: (b * nqb + i, 0)),
        compiler_params=_params("parallel", "arbitrary"),
        name="band_attn_prompt",
    )(u, u, u, u, u, u, u, u, bias)


def _band_sample_body(q_ref, k_ref, v_ref, z_ref, ck_ref, cv_ref, bc_ref, bn_ref, o_ref):
    for h in range(N_HEADS):
        sl = slice(h * HEAD_DIM, (h + 1) * HEAD_DIM)
        q = q_ref[:, sl].astype(bf16)
        s_c = lax.dot_general(q, ck_ref[:, sl].astype(bf16), NT_DIMS,
                              preferred_element_type=f32) * HEAD_SCALE + bc_ref[h]
        s_n = lax.dot_general(q, k_ref[:, sl].astype(bf16), NT_DIMS,
                              preferred_element_type=f32) * HEAD_SCALE + bn_ref[h]
        m = jnp.maximum(jnp.max(s_c, axis=-1, keepdims=True), jnp.max(s_n, axis=-1, keepdims=True))
        p_c = jnp.exp(s_c - m)
        p_n = jnp.exp(s_n - m)
        l = jnp.sum(p_c, axis=-1, keepdims=True) + jnp.sum(p_n, axis=-1, keepdims=True)
        o = (jnp.dot(p_c.astype(bf16), cv_ref[:, sl].astype(bf16), preferred_element_type=f32)
             + jnp.dot(p_n.astype(bf16), v_ref[:, sl].astype(bf16), preferred_element_type=f32)) / l
        o_ref[:, sl] = (o * _silu(z_ref[:, sl])).astype(bf16)


def _band_sample(u, cache_k, cache_v, layer, rel_bias, row0, nb, t):
    nrow = cache_k.shape[2]
    qpos = np.arange(t)[:, None]
    rel_c = qpos + nrow - np.arange(nrow)[None, :]
    rel_n = qpos - np.arange(t)[None, :]
    rb = rel_bias.astype(f32)
    bias_c = rb[:, np.clip(rel_c, -A_REL_CLIP, A_REL_CLIP) + A_REL_CLIP]
    bias_n = rb[:, np.clip(rel_n, -A_REL_CLIP, A_REL_CLIP) + A_REL_CLIP]
    rb0 = row0 // t

    def blk(col):
        return pl.BlockSpec((t, BR_WIDTH), lambda b: (rb0 + b, col // BR_WIDTH))

    cache_spec = pl.BlockSpec((None, None, nrow, BR_WIDTH), lambda b: (layer, b, 0, 0))
    return pl.pallas_call(
        _band_sample_body,
        out_shape=jax.ShapeDtypeStruct((nb * t, BR_WIDTH), bf16),
        grid=(nb,),
        in_specs=[blk(U_AQ), blk(U_AK), blk(U_AV), blk(U_AZ), cache_spec, cache_spec,
                  pl.BlockSpec((N_HEADS, t, nrow), lambda b: (0, 0, 0)),
                  pl.BlockSpec((N_HEADS, t, t), lambda b: (0, 0, 0))],
        out_specs=pl.BlockSpec((t, BR_WIDTH), lambda b: (b, 0)),
        compiler_params=_params("arbitrary"),
        name="band_attn_sample",
    )(u, u, u, u, cache_k, cache_v, bias_c, bias_n)


def _mem_body(q_ref, z_ref, k_ref, v_ref, o_ref):
    for h in range(N_HEADS):
        sl = slice(h * HEAD_DIM, (h + 1) * HEAD_DIM)
        q = q_ref[:, sl].astype(bf16)
        s = lax.dot_general(q, k_ref[:, sl].astype(bf16), NT_DIMS, preferred_element_type=f32) * HEAD_SCALE
        p = jnp.exp(s - jnp.max(s, axis=-1, keepdims=True))
        l = jnp.sum(p, axis=-1, keepdims=True)
        o = jnp.dot(p.astype(bf16), v_ref[:, sl].astype(bf16), preferred_element_type=f32) / l
        o_ref[:, sl] = (o * _silu(z_ref[:, sl])).astype(bf16)


def _mem_attn(u, row0, nb, t, tq, k_arr, v_arr, k_spec, v_spec, name):
    nq = t // tq
    rb0 = row0 // tq

    def blk(col):
        return pl.BlockSpec((tq, BR_WIDTH), lambda b, i: (rb0 + b * nq + i, col // BR_WIDTH))

    return pl.pallas_call(
        _mem_body,
        out_shape=jax.ShapeDtypeStruct((nb * t, BR_WIDTH), bf16),
        grid=(nb, nq),
        in_specs=[blk(U_MQ), blk(U_MZ), k_spec, v_spec],
        out_specs=pl.BlockSpec((tq, BR_WIDTH), lambda b, i: (b * nq + i, 0)),
        compiler_params=_params("parallel", "arbitrary"),
        name=name,
    )(u, u, k_arr, v_arr)


def _diff_lambda(lamp_ref, lam_init):
    lp = lamp_ref[...]
    a = jnp.sum(lp[0:1] * lp[1:2], axis=-1, keepdims=True)
    b = jnp.sum(lp[2:3] * lp[3:4], axis=-1, keepdims=True)
    return jnp.exp(a) - jnp.exp(b) + lam_init


def _diff_epilogue(o0, o1, lam, gain, z, lam_init):
    y = _head_norm(o0 - lam * o1, gain) * (1.0 - lam_init)
    return (y * _silu(z)).astype(bf16)


def _half_masks():
    lane = lax.broadcasted_iota(jnp.int32, (1, HEAD_DIM), 1)
    return lane < DIFF_DH, lane >= DIFF_DH


def _diff_prompt_body(q_ref, k_ref, v_ref, z_ref, lamp_ref, gain_ref, o_ref, m_scr, l_scr, acc_scr,
                      *, lam_init):
    tq, tk = q_ref.shape[0], k_ref.shape[0]
    i = pl.program_id(1)
    j = pl.program_id(2)

    @pl.when(j == 0)
    def _():
        m_scr[...] = jnp.full(m_scr.shape, NEG, f32)
        l_scr[...] = jnp.zeros(l_scr.shape, f32)
        acc_scr[...] = jnp.zeros(acc_scr.shape, f32)

    halves = _half_masks()

    def step(diagonal):
        if diagonal:
            qc = lax.broadcasted_iota(jnp.int32, (tq, tk), 0) // CHUNK
            kc = lax.broadcasted_iota(jnp.int32, (tq, tk), 1) // CHUNK
            mask = kc <= qc
        for h in range(N_HEADS):
            sl = slice(h * HEAD_DIM, (h + 1) * HEAD_DIM)
            q = q_ref[:, sl]
            k = k_ref[:, sl].astype(bf16)
            v = v_ref[:, sl].astype(bf16)
            for t in range(2):
                idx = 2 * h + t
                qt = jnp.where(halves[t], q, 0.0).astype(bf16)
                s = lax.dot_general(qt, k, NT_DIMS, preferred_element_type=f32) * DIFF_SCALE
                if diagonal:
                    s = jnp.where(mask, s, NEG)
                m_old = m_scr[idx]
                m_new = jnp.maximum(m_old, jnp.max(s, axis=-1, keepdims=True))
                p = jnp.exp(s - m_new)
                alpha = jnp.exp(m_old - m_new)
                l_scr[idx] = alpha * l_scr[idx] + jnp.sum(p, axis=-1, keepdims=True)
                acc_scr[idx] = alpha * acc_scr[idx] + jnp.dot(p.astype(bf16), v, preferred_element_type=f32)
                m_scr[idx] = m_new

    @pl.when(j < i)
    def _():
        step(False)

    @pl.when(j == i)
    def _():
        step(True)
        lam = _diff_lambda(lamp_ref, lam_init)
        for h in range(N_HEADS):
            sl = slice(h * HEAD_DIM, (h + 1) * HEAD_DIM)
            o0 = acc_scr[2 * h] / l_scr[2 * h]
            o1 = acc_scr[2 * h + 1] / l_scr[2 * h + 1]
            o_ref[:, sl] = _diff_epilogue(o0, o1, lam, gain_ref[:, sl], z_ref[:, sl], lam_init)


def _diff_prompt(u, lamp, gain, lam_init, batch, seq):
    t = _divisor_tile(seq, C_BLOCK, 2 * CHUNK)
    nt = seq // t

    def qblk(col):
        return pl.BlockSpec((t, BR_WIDTH), lambda b, i, j: (b * nt + i, col // BR_WIDTH))

    def kblk(col):
        return pl.BlockSpec((t, BR_WIDTH), lambda b, i, j: (b * nt + jnp.minimum(j, i), col // BR_WIDTH))

    return pl.pallas_call(
        functools.partial(_diff_prompt_body, lam_init=lam_init),
        out_shape=jax.ShapeDtypeStruct((batch * seq, BR_WIDTH), bf16),
        grid=(batch, nt, nt),
        in_specs=[qblk(U_CQ), kblk(U_CK), kblk(U_CV), qblk(U_CZ),
                  pl.BlockSpec((4, DIFF_DH), lambda b, i, j: (0, 0)),
                  pl.BlockSpec((1, BR_WIDTH), lambda b, i, j: (0, 0))],
        out_specs=pl.BlockSpec((t, BR_WIDTH), lambda b, i, j: (b * nt + i, 0)),
        scratch_shapes=[pltpu.VMEM((2 * N_HEADS, t, 1), f32),
                        pltpu.VMEM((2 * N_HEADS, t, 1), f32),
                        pltpu.VMEM((2 * N_HEADS, t, HEAD_DIM), f32)],
        compiler_params=_params("parallel", "parallel", "arbitrary"),
        name="diff_attn_prompt",
    )(u, u, u, u, lamp, gain.reshape(1, BR_WIDTH))


def _diff_sample_body(q_ref, k_ref, v_ref, z_ref, ck_ref, cv_ref, lamp_ref, gain_ref, o_ref, *, lam_init):
    halves = _half_masks()
    lam = _diff_lambda(lamp_ref, lam_init)
    for h in range(N_HEADS):
        sl = slice(h * HEAD_DIM, (h + 1) * HEAD_DIM)
        q = q_ref[:, sl]
        k = k_ref[:, sl].astype(bf16)
        v = v_ref[:, sl].astype(bf16)
        ck = ck_ref[:, sl].astype(bf16)
        cv = cv_ref[:, sl].astype(bf16)
        outs = []
        for t in range(2):
            qt = jnp.where(halves[t], q, 0.0).astype(bf16)
            s_c = lax.dot_general(qt, ck, NT_DIMS, preferred_element_type=f32) * DIFF_SCALE
            s_n = lax.dot_general(qt, k, NT_DIMS, preferred_element_type=f32) * DIFF_SCALE
            m = jnp.maximum(jnp.max(s_c, axis=-1, keepdims=True), jnp.max(s_n, axis=-1, keepdims=True))
            p_c = jnp.exp(s_c - m)
            p_n = jnp.exp(s_n - m)
            l = jnp.sum(p_c, axis=-1, keepdims=True) + jnp.sum(p_n, axis=-1, keepdims=True)
            outs.append((jnp.dot(p_c.astype(bf16), cv, preferred_element_type=f32)
                         + jnp.dot(p_n.astype(bf16), v, preferred_element_type=f32)) / l)
        o_ref[:, sl] = _diff_epilogue(outs[0], outs[1], lam, gain_ref[:, sl], z_ref[:, sl], lam_init)


def _diff_sample(u, cache_k, cache_v, layer, lamp, gain, lam_init, row0, nb, t):
    past = cache_k.shape[2]
    rb0 = row0 // t

    def blk(col):
        return pl.BlockSpec((t, BR_WIDTH), lambda b: (rb0 + b, col // BR_WIDTH))

    cache_spec = pl.BlockSpec((None, None, past, BR_WIDTH), lambda b: (layer, b, 0, 0))
    return pl.pallas_call(
        functools.partial(_diff_sample_body, lam_init=lam_init),
        out_shape=jax.ShapeDtypeStruct((nb * t, BR_WIDTH), bf16),
        grid=(nb,),
        in_specs=[blk(U_CQ), blk(U_CK), blk(U_CV), blk(U_CZ), cache_spec, cache_spec,
                  pl.BlockSpec((4, DIFF_DH), lambda b: (0, 0)),
                  pl.BlockSpec((1, BR_WIDTH), lambda b: (0, 0))],
        out_specs=pl.BlockSpec((t, BR_WIDTH), lambda b: (b, 0)),
        compiler_params=_params("arbitrary"),
        name="diff_attn_sample",
    )(u, u, u, u, cache_k, cache_v, lamp, gain.reshape(1, BR_WIDTH))


CONV_PAD = 8


def _mlstm_body(qk_ref, v_ref, og_ref, z_ref, gc_ref, gr_ref, cw_ref, cb_ref, gbr_ref, gbc_ref, gn_ref,
                c0_ref, n0_ref, m0_ref, conv0_ref,
                out_ref, c_out_ref, n_out_ref, m_out_ref,
                cbuf, c_scr, n_scr, m_scr):
    ln = qk_ref.shape[0]
    c = pl.program_id(1)
    lo = CONV_PAD - (CONV_W - 1)

    @pl.when(c == 0)
    def _():
        cbuf[lo:CONV_PAD, :] = conv0_ref[...]
        c_scr[...] = c0_ref[...]
        n_scr[...] = n0_ref[...]
        m_scr[...] = m0_ref[...]

    cbuf[CONV_PAD:CONV_PAD + ln, :] = qk_ref[...]
    conv = cb_ref[...] + cbuf[lo:lo + ln, :] * cw_ref[0:1, :]
    for jj in range(1, CONV_W):
        conv = conv + cbuf[lo + jj:lo + jj + ln, :] * cw_ref[jj:jj + 1, :]
    tail = cbuf[lo + ln:CONV_PAD + ln, :]
    cbuf[lo:CONV_PAD, :] = tail
    act = _silu(conv)

    gcb = gc_ref[...] + gbr_ref[...]
    grb = gr_ref[...] + gbc_ref[...]
    row = lax.broadcasted_iota(jnp.int32, (ln, ln), 0)
    col = lax.broadcasted_iota(jnp.int32, (ln, ln), 1)
    causal = col <= row

    for h in range(N_HEADS):
        sl = slice(h * HEAD_DIM, (h + 1) * HEAD_DIM)
        q = act[:, h * HEAD_DIM:(h + 1) * HEAD_DIM]
        k = act[:, BR_WIDTH + h * HEAD_DIM:BR_WIDTH + (h + 1) * HEAD_DIM] * HEAD_SCALE
        v = v_ref[:, sl]
        i_col = gcb[:, h:h + 1]
        lf_col = _log_sigmoid(gcb[:, N_HEADS + h:N_HEADS + h + 1])
        i_row = grb[h:h + 1, :]
        lf_row = _log_sigmoid(grb[N_HEADS + h:N_HEADS + h + 1, :])
        b_col = jnp.sum(jnp.where(causal, lf_row, 0.0), axis=1, keepdims=True)
        b_row = jnp.sum(jnp.where(row <= col, lf_col, 0.0), axis=0, keepdims=True)
        b_last = b_col[ln - 1:ln, :]
        m_old = m_scr[h][:, 0:1]
        c_old = c_scr[h]
        n_old = n_scr[h]

        dlog = jnp.where(causal, b_col - b_row + i_row, NEG)
        inter = b_col + m_old
        mt = jnp.maximum(inter, jnp.max(dlog, axis=1, keepdims=True))
        w = jnp.exp(dlog - mt)
        wi = jnp.exp(inter - mt)
        qb = q.astype(bf16)
        vb = v.astype(bf16)
        qk = lax.dot_general(qb, k.astype(bf16), NT_DIMS, preferred_element_type=f32) * w
        num = (jnp.dot(qk.astype(bf16), vb, preferred_element_type=f32)
               + wi * jnp.dot(qb, c_old.astype(bf16), preferred_element_type=f32))
        den = jnp.sum(qk, axis=1, keepdims=True) + wi * jnp.sum(q * n_old, axis=1, keepdims=True)
        hh = num / jnp.maximum(jnp.abs(den), jnp.exp(-mt))

        m_new = mt[ln - 1:ln, :]
        ws = jnp.exp(b_last - b_col + i_col - m_new)
        dec = jnp.exp(b_last + m_old - m_new)
        kw = k * ws
        c_scr[h] = dec * c_old + lax.dot_general(kw.astype(bf16), vb, TN_DIMS, preferred_element_type=f32)
        n_scr[h] = dec * n_old + jnp.sum(kw, axis=0, keepdims=True)
        m_scr[h] = jnp.broadcast_to(m_new, (1, HEAD_DIM))

        y = _head_norm(_sigmoid(og_ref[:, sl]) * hh, gn_ref[:, sl])
        out_ref[:, sl] = (y * _silu(z_ref[:, sl])).astype(bf16)

    @pl.when(c == pl.num_programs(1) - 1)
    def _():
        c_out_ref[...] = c_scr[...]
        n_out_ref[...] = n_scr[...]
        m_out_ref[...] = m_scr[...]


def _mlstm(u, gates_r, conv_w, conv_b, b_ig, b_fg, gn, c0, n0, m0, conv0, row0, ng, nc, ln):
    rb0 = row0 // ln
    gbias = jnp.concatenate([b_ig, b_fg]).astype(f32)
    gbr = jnp.zeros((1, GATE_PAD), f32).at[0, :2 * N_HEADS].set(gbias)
    gbc = gbias.reshape(2 * N_HEADS, 1)

    def blk(col, width):
        return pl.BlockSpec((ln, width), lambda g, c: (rb0 + g * nc + c, col // width))

    def const(shape):
        return pl.BlockSpec(shape, lambda g, c: (0,) * len(shape))

    def per_seq(shape):
        return pl.BlockSpec((None,) + shape, lambda g, c: (g,) + (0,) * len(shape))

    state_shapes = [jax.ShapeDtypeStruct((ng, N_HEADS, HEAD_DIM, HEAD_DIM), f32),
                    jax.ShapeDtypeStruct((ng, N_HEADS, 1, HEAD_DIM), f32),
                    jax.ShapeDtypeStruct((ng, N_HEADS, 1, HEAD_DIM), f32)]
    state_specs = [per_seq((N_HEADS, HEAD_DIM, HEAD_DIM)),
                   per_seq((N_HEADS, 1, HEAD_DIM)),
                   per_seq((N_HEADS, 1, HEAD_DIM))]
    out, c_new, n_new, m_new = pl.pallas_call(
        _mlstm_body,
        out_shape=[jax.ShapeDtypeStruct((ng * nc * ln, BR_WIDTH), bf16)] + state_shapes,
        grid=(ng, nc),
        in_specs=[blk(U_BQK, B_QK), blk(U_BV, BR_WIDTH), blk(U_BO, BR_WIDTH), blk(U_BZ, BR_WIDTH),
                  blk(U_GATES, GATE_PAD),
                  pl.BlockSpec((None, 2 * N_HEADS, ln), lambda g, c: (g * nc + c, 0, 0)),
                  const((CONV_W, B_QK)), const((1, B_QK)), const((1, GATE_PAD)), const((2 * N_HEADS, 1)),
                  const((1, BR_WIDTH))] + state_specs + [per_seq((CONV_W - 1, B_QK))],
        out_specs=[pl.BlockSpec((ln, BR_WIDTH), lambda g, c: (g * nc + c, 0))] + state_specs,
        scratch_shapes=[pltpu.VMEM((CONV_PAD + ln, B_QK), f32),
                        pltpu.VMEM((N_HEADS, HEAD_DIM, HEAD_DIM), f32),
                        pltpu.VMEM((N_HEADS, 1, HEAD_DIM), f32),
                        pltpu.VMEM((N_HEADS, 1, HEAD_DIM), f32)],
        compiler_params=_params("parallel", "arbitrary"),
        name="mlstm",
    )(u, u, u, u, u, gates_r, conv_w, conv_b.reshape(1, B_QK), gbr, gbc, gn.reshape(1, BR_WIDTH),
      c0, n0.reshape(ng, N_HEADS, 1, HEAD_DIM),
      jnp.broadcast_to(m0[:, :, None, None], (ng, N_HEADS, 1, HEAD_DIM)), conv0)
    return out, c_new, n_new[:, :, 0, :], m_new[:, :, 0, 0]


def _ret_body(q_ref, k_ref, v_ref, z_ref, cc_ref, ss_ref, gn_ref, s0_ref, out_ref, s_out_ref, s_scr):
    ln = q_ref.shape[0]
    c = pl.program_id(1)

    @pl.when(c == 0)
    def _():
        s_scr[...] = s0_ref[...]

    row = lax.broadcasted_iota(jnp.int32, (ln, ln), 0)
    col = lax.broadcasted_iota(jnp.int32, (ln, ln), 1)
    rel = (row - col).astype(f32)
    tpos = lax.broadcasted_iota(jnp.int32, (ln, 1), 0).astype(f32)
    cc = cc_ref[...]
    ss = ss_ref[...]
    for h in range(N_HEADS):
        sl = slice(h * HEAD_DIM, (h + 1) * HEAD_DIM)
        lg = math.log(1.0 - 2.0 ** (-RET_GAMMA_EXP0 - h))
        q = q_ref[:, sl]
        k = k_ref[:, sl]
        qr = q * cc + pltpu.roll(q, DIFF_DH, 1) * ss
        kr = (k * cc + pltpu.roll(k, DIFF_DH, 1) * ss) * HEAD_SCALE
        vb = v_ref[:, sl].astype(bf16)
        decay = jnp.where(rel >= 0.0, jnp.exp(jnp.maximum(rel, 0.0) * lg), 0.0)
        att = lax.dot_general(qr.astype(bf16), kr.astype(bf16), NT_DIMS, preferred_element_type=f32) * decay
        q_dec = qr * jnp.exp((tpos + 1.0) * lg)
        s_old = s_scr[h]
        o = (jnp.dot(att.astype(bf16), vb, preferred_element_type=f32)
             + jnp.dot(q_dec.astype(bf16), s_old.astype(bf16), preferred_element_type=f32))
        k_dec = kr * jnp.exp((ln - 1.0 - tpos) * lg)
        s_scr[h] = math.exp(ln * lg) * s_old + lax.dot_general(k_dec.astype(bf16), vb, TN_DIMS,
                                                               preferred_element_type=f32)
        out_ref[:, sl] = (_head_norm(o, gn_ref[:, sl]) * _silu(z_ref[:, sl])).astype(bf16)

    @pl.when(c == pl.num_programs(1) - 1)
    def _():
        s_out_ref[...] = s_scr[...]


def _rope_tables(pos):
    inv = ROPE_BASE ** (-jnp.arange(0, HEAD_DIM, 2, dtype=f32) / HEAD_DIM)
    ang = pos.astype(f32)[:, None] * inv[None, :]
    cos, sin = jnp.cos(ang), jnp.sin(ang)
    return jnp.concatenate([cos, cos], axis=-1), jnp.concatenate([-sin, sin], axis=-1)


def _retention(u, pos, gn, s0, row0, ng, nc, ln):
    rb0 = row0 // ln
    cc, ss = _rope_tables(pos)

    def blk(col):
        return pl.BlockSpec((ln, BR_WIDTH), lambda g, c: (rb0 + g * nc + c, col // BR_WIDTH))

    rope_spec = pl.BlockSpec((ln, HEAD_DIM), lambda g, c: (c, 0))
    state_spec = pl.BlockSpec((None, N_HEADS, HEAD_DIM, HEAD_DIM), lambda g, c: (g, 0, 0, 0))
    return pl.pallas_call(
        _ret_body,
        out_shape=[jax.ShapeDtypeStruct((ng * nc * ln, BR_WIDTH), bf16),
                   jax.ShapeDtypeStruct((ng, N_HEADS, HEAD_DIM, HEAD_DIM), f32)],
        grid=(ng, nc),
        in_specs=[blk(U_DQ), blk(U_DK), blk(U_DV), blk(U_DZ), rope_spec, rope_spec,
                  pl.BlockSpec((1, BR_WIDTH), lambda g, c: (0, 0)), state_spec],
        out_specs=[pl.BlockSpec((ln, BR_WIDTH), lambda g, c: (g * nc + c, 0)), state_spec],
        scratch_shapes=[pltpu.VMEM((N_HEADS, HEAD_DIM, HEAD_DIM), f32)],
        compiler_params=_params("parallel", "arbitrary"),
        name="retention",
    )(u, u, u, u, cc, ss, gn.reshape(1, BR_WIDTH), s0)


def _merge_body(x_ref, g_ref, oa_ref, ob_ref, oc_ref, od_ref, om_ref,
                wg0_ref, wg1_ref, wg2_ref, wg3_ref, wg4_ref, wb_ref, wo_ref, fg_ref,
                y_ref, h_scr, acc_scr, *, rows, final):
    n = pl.program_id(1)

    @pl.when(n == 0)
    def _():
        _rmsnorm_rows(x_ref, g_ref, h_scr, rows)
        acc_scr[...] = jnp.zeros(acc_scr.shape, f32)

    h = h_scr[...]
    merged = None
    for i, (o_ref, wg_ref) in enumerate(zip((oa_ref, ob_ref, oc_ref, od_ref, om_ref),
                                            (wg0_ref, wg1_ref, wg2_ref, wg3_ref, wg4_ref))):
        gate = _sigmoid(jnp.dot(h, wg_ref[...], preferred_element_type=f32))
        term = gate * jnp.dot(o_ref[...], wb_ref[i], preferred_element_type=f32)
        merged = term if merged is None else merged + term
    acc_scr[...] += jnp.dot(merged.astype(bf16), wo_ref[...], preferred_element_type=f32)

    @pl.when(n == pl.num_programs(1) - 1)
    def _():
        tm = x_ref.shape[0]
        for r in range(0, tm, rows):
            y = x_ref[r:r + rows, :] + acc_scr[r:r + rows, :]
            if final:
                y = (y * lax.rsqrt(jnp.mean(y * y, axis=-1, keepdims=True) + EPS)) * fg_ref[...]
            y_ref[r:r + rows, :] = y


def _merge(x, g, outs, wg, wb, wo, fg, *, final, tm_target=640, tn=256):
    m, d = x.shape
    tm = _divisor_tile(m, tm_target, 16)
    rows = _divisor_tile(tm, 256, 8)
    nn = d // tn

    def wg_spec(i):
        return pl.BlockSpec((d, tn), lambda r, n: (0, i * nn + n))

    o_spec = pl.BlockSpec((tm, BR_WIDTH), lambda r, n: (r, 0))
    return pl.pallas_call(
        functools.partial(_merge_body, rows=rows, final=final),
        out_shape=jax.ShapeDtypeStruct((m, d), f32),
        grid=(m // tm, nn),
        in_specs=[pl.BlockSpec((tm, d), lambda r, n: (r, 0)),
                  pl.BlockSpec((1, d), lambda r, n: (0, 0))]
                 + [o_spec] * N_BRANCH
                 + [wg_spec(i) for i in range(N_BRANCH)]
                 + [pl.BlockSpec((N_BRANCH, BR_WIDTH, tn), lambda r, n: (0, 0, n)),
                    pl.BlockSpec((tn, d), lambda r, n: (n, 0)),
                    pl.BlockSpec((1, d), lambda r, n: (0, 0))],
        out_specs=pl.BlockSpec((tm, d), lambda r, n: (r, 0)),
        scratch_shapes=[pltpu.VMEM((tm, d), bf16), pltpu.VMEM((tm, d), f32)],
        compiler_params=_params("parallel", "arbitrary"),
        name="gated_merge",
    )(x, g.reshape(1, d), *outs, wg, wg, wg, wg, wg, wb, wo, fg.reshape(1, d))


def _repack_w_in(w):
    pad = jnp.zeros((w.shape[0], GATE_PAD - 2 * N_HEADS), w.dtype)
    wu = jnp.concatenate([w[:, :W_BI], w[:, W_BZ:W_GATE], w[:, W_BI:W_BZ], pad], axis=1)
    return wu.astype(bf16), w[:, W_GATE:].astype(bf16)


def _gates_rowform(u, row0, nrows, ln):
    g = u[row0:row0 + nrows, U_GATES:U_GATES + 2 * N_HEADS]
    return g.reshape(nrows // ln, ln, 2 * N_HEADS).transpose(0, 2, 1)


def kernel(x_prompt, x_sample, mem_prompt, cache_a_k, cache_a_v, cache_c_k, cache_c_v, cache_mem_k, cache_mem_v, state_b_C, state_b_n, state_b_m, state_b_conv, state_d_S, norm_g, w_in, conv_w, conv_b, b_ig, b_fg, rel_bias, lam_q1, lam_k1, lam_q2, lam_k2, gn_b, subln_c, gn_d, mem_norm_g, w_mk, w_mv, w_branch, w_out, final_g):
    batch, seq, d = x_prompt.shape
    nb, t, _ = x_sample.shape
    depth = w_in.shape[0]
    n_mem = mem_prompt.shape[1]
    past = cache_c_k.shape[2]
    mp = batch * seq
    assert d == D_MODEL and seq % C_BLOCK == 0 and seq % A_QBLOCK == 0 and mp % t == 0

    x = jnp.concatenate([x_prompt.reshape(mp, d), x_sample.reshape(nb * t, d)], axis=0)
    mem = mem_prompt.reshape(batch * n_mem, d)
    ca_k = cache_a_k.reshape(cache_a_k.shape[:3] + (BR_WIDTH,))
    ca_v = cache_a_v.reshape(cache_a_v.shape[:3] + (BR_WIDTH,))
    cc_k = cache_c_k.reshape(cache_c_k.shape[:3] + (BR_WIDTH,))
    cc_v = cache_c_v.reshape(cache_c_v.shape[:3] + (BR_WIDTH,))
    cm_k = cache_mem_k.reshape(cache_mem_k.shape[:3] + (BR_WIDTH,))
    cm_v = cache_mem_v.reshape(cache_mem_v.shape[:3] + (BR_WIDTH,))
    pos_p = jnp.arange(seq)
    pos_s = past + jnp.arange(t)
    a_rows = min(A_WINDOW, seq)
    mem_tq = _divisor_tile(seq, 512, 16)

    sp, ss = [], []
    for l in range(depth):
        lam_init = 0.8 - 0.6 * math.exp(-0.3 * l)
        lamp = jnp.stack([lam_q1[l], lam_k1[l], lam_q2[l], lam_k2[l]]).astype(f32)
        wu, wg = _repack_w_in(w_in[l])
        u = _norm_matmul(x, norm_g[l], wu, tm_target=768, tn=896, name="in_proj")
        mkv = _norm_matmul(mem, mem_norm_g[l], jnp.concatenate([w_mk[l], w_mv[l]], axis=1).astype(bf16),
                           tm_target=512, tn=512, name="mem_proj")

        oa_p = _band_prompt(u, rel_bias[l], batch, seq)
        zeros_c = jnp.zeros((batch, N_HEADS, HEAD_DIM, HEAD_DIM), f32)
        ob_p, c_p, n_p, m_p = _mlstm(
            u, _gates_rowform(u, 0, mp, CHUNK), conv_w[l], conv_b[l], b_ig[l], b_fg[l], gn_b[l],
            zeros_c, jnp.zeros((batch, N_HEADS, HEAD_DIM), f32), jnp.zeros((batch, N_HEADS), f32),
            jnp.zeros((batch, CONV_W - 1, B_QK), f32), 0, batch, seq // CHUNK, CHUNK)
        oc_p = _diff_prompt(u, lamp, subln_c[l], lam_init, batch, seq)
        od_p, s_p = _retention(u, pos_p, gn_d[l], zeros_c, 0, batch, seq // CHUNK, CHUNK)
        mem_spec_k = pl.BlockSpec((n_mem, BR_WIDTH), lambda b, i: (b, 0))
        mem_spec_v = pl.BlockSpec((n_mem, BR_WIDTH), lambda b, i: (b, 1))
        om_p = _mem_attn(u, 0, batch, seq, mem_tq, mkv, mkv, mem_spec_k, mem_spec_v, "mem_attn_prompt")

        oa_s = _band_sample(u, ca_k, ca_v, l, rel_bias[l], mp, nb, t)
        ob_s, c_s, n_s, m_s = _mlstm(
            u, _gates_rowform(u, mp, nb * t, t), conv_w[l], conv_b[l], b_ig[l], b_fg[l], gn_b[l],
            state_b_C[l].astype(f32), state_b_n[l].astype(f32), state_b_m[l].astype(f32),
            state_b_conv[l].astype(f32), mp, nb, 1, t)
        oc_s = _diff_sample(u, cc_k, cc_v, l, lamp, subln_c[l], lam_init, mp, nb, t)
        od_s, s_s = _retention(u, pos_s, gn_d[l], state_d_S[l].astype(f32), mp, nb, 1, t)
        cache_spec = pl.BlockSpec((None, None, n_mem, BR_WIDTH), lambda b, i: (l, b, 0, 0))
        om_s = _mem_attn(u, mp, nb, t, t, cm_k, cm_v, cache_spec, cache_spec, "mem_attn_sample")

        outs = [jnp.concatenate([p, s], axis=0) for p, s in
                ((oa_p, oa_s), (ob_p, ob_s), (oc_p, oc_s), (od_p, od_s), (om_p, om_s))]
        x = _merge(x, norm_g[l], outs, wg, w_branch[l].astype(bf16), w_out[l].astype(bf16), final_g,
                   final=(l == depth - 1))

        up = u[:mp].reshape(batch, seq, U_WIDTH)
        us = u[mp:].reshape(nb, t, U_WIDTH)

        def hd(a):
            return a.reshape(a.shape[:2] + (N_HEADS, HEAD_DIM))

        sp.append((hd(up[:, seq - a_rows:, U_AK:U_AK + BR_WIDTH]), hd(up[:, seq - a_rows:, U_AV:U_AV + BR_WIDTH]),
                   hd(up[:, :, U_CK:U_CK + BR_WIDTH]), hd(up[:, :, U_CV:U_CV + BR_WIDTH]),
                   hd(mkv[:, :BR_WIDTH].reshape(batch, n_mem, BR_WIDTH)),
                   hd(mkv[:, BR_WIDTH:].reshape(batch, n_mem, BR_WIDTH)),
                   c_p, n_p, m_p, up[:, seq - (CONV_W - 1):, U_BQK:U_BQK + B_QK], s_p))
        ss.append((hd(us[:, :, U_AK:U_AK + BR_WIDTH]), hd(us[:, :, U_AV:U_AV + BR_WIDTH]),
                   hd(us[:, :, U_CK:U_CK + BR_WIDTH]), hd(us[:, :, U_CV:U_CV + BR_WIDTH]),
                   c_s, n_s, m_s, us[:, t - (CONV_W - 1):, U_BQK:U_BQK + B_QK], s_s))

    y_prompt = x[:mp].reshape(batch, seq, d)
    y_sample = x[mp:].reshape(nb, t, d)
    p_states = tuple(jnp.stack([st[i] for st in sp]) for i in range(11))
    s_states = tuple(jnp.stack([st[i] for st in ss]) for i in range(9))
    return (y_prompt, y_sample) + p_states + s_states
```

```python
import functools
import math

import jax
import jax.numpy as jnp
import numpy as np
from jax import lax
from jax.experimental import pallas as pl
from jax.experimental.pallas import tpu as pltpu

f32 = jnp.float32
bf16 = jnp.bfloat16

D_MODEL = 2048
N_HEADS = 4
HEAD_DIM = 128
BR_WIDTH = N_HEADS * HEAD_DIM
N_BRANCH = 5
CHUNK = 64
A_PREV_CHUNKS = 8
A_WINDOW = A_PREV_CHUNKS * CHUNK
A_BAND = (A_PREV_CHUNKS + 1) * CHUNK
A_REL_CLIP = 128
CONV_W = 4
B_QK = 2 * BR_WIDTH
DIFF_DH = HEAD_DIM // 2
RET_GAMMA_EXP0 = 5.0
ROPE_BASE = 10000.0
EPS = 1e-6
NEG = -1e30
HEAD_SCALE = HEAD_DIM ** -0.5
DIFF_SCALE = DIFF_DH ** -0.5

U_AQ, U_AK, U_AV, U_AZ = 0, 512, 1024, 1536
U_BQK, U_BV, U_BO, U_BZ = 2048, 3072, 3584, 4096
U_CQ, U_CK, U_CV, U_CZ = 4608, 5120, 5632, 6144
U_DQ, U_DK, U_DV, U_DZ = 6656, 7168, 7680, 8192
U_MQ, U_MZ = 8704, 9216
U_GATES = 9728
GATE_PAD = 128
U_WIDTH = U_GATES + GATE_PAD
W_BI = 4096
W_BZ = 4104
W_GATE = 9736

VMEM_LIMIT_BYTES = 56 * 1024 * 1024
A_QBLOCK = 4 * CHUNK
C_BLOCK = 512
SCAN_CHUNK = 256
NT_DIMS = (((1,), (1,)), ((), ()))
TN_DIMS = (((0,), (0,)), ((), ()))


def _divisor_tile(n, target, multiple):
    best = None
    for t in range(multiple, min(n, target) + 1, multiple):
        if n % t == 0:
            best = t
    assert best is not None, (n, target, multiple)
    return best


def _params(*sem):
    return pltpu.CompilerParams(dimension_semantics=sem, vmem_limit_bytes=VMEM_LIMIT_BYTES)


def _sigmoid(x):
    return 1.0 / (1.0 + jnp.exp(-x))


def _silu(x):
    return x * _sigmoid(x)


def _log_sigmoid(x):
    return jnp.minimum(x, 0.0) - jnp.log1p(jnp.exp(-jnp.abs(x)))


def _head_norm(y, gain):
    return y * lax.rsqrt(jnp.mean(y * y, axis=-1, keepdims=True) + EPS) * gain


def _rmsnorm_rows(x_ref, g_ref, h_ref, rows):
    tm = x_ref.shape[0]
    for r in range(0, tm, rows):
        x = x_ref[r:r + rows, :]
        ms = jnp.mean(x * x, axis=-1, keepdims=True)
        h_ref[r:r + rows, :] = ((x * lax.rsqrt(ms + EPS)) * g_ref[...]).astype(bf16)


def _norm_matmul_body(x_ref, g_ref, w_ref, o_ref, h_ref, *, rows):
    @pl.when(pl.program_id(1) == 0)
    def _():
        _rmsnorm_rows(x_ref, g_ref, h_ref, rows)

    o_ref[...] = jnp.dot(h_ref[...], w_ref[...], preferred_element_type=f32).astype(o_ref.dtype)


def _norm_matmul(x, g, w, *, tm_target, tn, name):
    m, d = x.shape
    n = w.shape[1]
    tm = _divisor_tile(m, tm_target, 16)
    rows = _divisor_tile(tm, 256, 8)
    assert n % tn == 0
    return pl.pallas_call(
        functools.partial(_norm_matmul_body, rows=rows),
        out_shape=jax.ShapeDtypeStruct((m, n), f32),
        grid=(m // tm, n // tn),
        in_specs=[pl.BlockSpec((tm, d), lambda i, j: (i, 0)),
                  pl.BlockSpec((1, d), lambda i, j: (0, 0)),
                  pl.BlockSpec((d, tn), lambda i, j: (0, j))],
        out_specs=pl.BlockSpec((tm, tn), lambda i, j: (i, j)),
        scratch_shapes=[pltpu.VMEM((tm, d), bf16)],
        compiler_params=_params("parallel", "arbitrary"),
        name=name,
    )(x, g.reshape(1, d), w)


def _band_prompt_body(q_ref, k0_ref, k1_ref, k2_ref, v0_ref, v1_ref, v2_ref, z_ref, bias_ref, o_ref):
    tq = q_ref.shape[0]
    i = pl.program_id(1)
    w_idx = lax.broadcasted_iota(jnp.int32, (tq, 3 * tq), 1)
    valid = (w_idx + (i - 2) * tq) >= 0
    for h in range(N_HEADS):
        sl = slice(h * HEAD_DIM, (h + 1) * HEAD_DIM)
        q = q_ref[:, sl].astype(bf16)
        kw = jnp.concatenate([k0_ref[:, sl], k1_ref[:, sl], k2_ref[:, sl]], axis=0).astype(bf16)
        vw = jnp.concatenate([v0_ref[:, sl], v1_ref[:, sl], v2_ref[:, sl]], axis=0).astype(bf16)
        s = lax.dot_general(q, kw, NT_DIMS, preferred_element_type=f32) * HEAD_SCALE + bias_ref[h]
        s = jnp.where(valid, s, NEG)
        p = jnp.exp(s - jnp.max(s, axis=-1, keepdims=True))
        l = jnp.sum(p, axis=-1, keepdims=True)
        o = jnp.dot(p.astype(bf16), vw, preferred_element_type=f32) / l
        o_ref[:, sl] = (o * _silu(z_ref[:, sl])).astype(bf16)


def _rel_bias_toeplitz(rel_bias, rows, cols, rel0):
    p = rows + cols
    j = np.arange(p)
    d = np.where(j < cols, j, j - p)
    idx = np.clip(rel0 - d, -A_REL_CLIP, A_REL_CLIP) + A_REL_CLIP
    v = rel_bias.astype(f32)[:, idx]
    flat = jnp.tile(v, (1, rows))[:, :rows * (p - 1)]
    return flat.reshape(rel_bias.shape[0], rows, p - 1)[:, :, :cols]


def _band_bias_prompt(rel_bias):
    r = np.arange(A_QBLOCK)[:, None]
    w = np.arange(3 * A_QBLOCK)[None, :]
    kj = w - CHUNK * (r // CHUNK)
    inside = (kj >= 0) & (kj < A_BAND)
    table = _rel_bias_toeplitz(rel_bias, A_QBLOCK, 3 * A_QBLOCK, 2 * A_QBLOCK)
    return jnp.where(jnp.asarray(inside)[None], table, NEG)


def _band_prompt(u, rel_bias, batch, seq):
    tq = A_QBLOCK
    nqb = seq // tq
    bias = _band_bias_prompt(rel_bias)

    def blk(col, back):
        return pl.BlockSpec((tq, BR_WIDTH),
                            lambda b, i: (b * nqb + jnp.maximum(i - back, 0), col // BR_WIDTH))

    return pl.pallas_call(
        _band_prompt_body,
        out_shape=jax.ShapeDtypeStruct((batch * seq, BR_WIDTH), bf16),
        grid=(batch, nqb),
        in_specs=[blk(U_AQ, 0),
                  blk(U_AK, 2), blk(U_AK, 1), blk(U_AK, 0),
                  blk(U_AV, 2), blk(U_AV, 1), blk(U_AV, 0),
                  blk(U_AZ, 0),
                  pl.BlockSpec((N_HEADS, tq, 3 * tq), lambda b, i: (0, 0, 0))],
        out_specs=pl.BlockSpec((tq, BR_WIDTH), lambda b, i: (b * nqb + i, 0)),
        compiler_params=_params("parallel", "arbitrary"),
        name="band_attn_prompt",
    )(u, u, u, u, u, u, u, u, bias)


def _band_sample_body(q_ref, k_ref, v_ref, z_ref, ck_ref, cv_ref, bc_ref, bn_ref, o_ref):
    for h in range(N_HEADS):
        sl = slice(h * HEAD_DIM, (h + 1) * HEAD_DIM)
        q = q_ref[:, sl].astype(bf16)
        s_c = lax.dot_general(q, _head_rows(ck_ref, h, True).astype(bf16), NT_DIMS,
                              preferred_element_type=f32) * HEAD_SCALE + bc_ref[h]
        s_n = lax.dot_general(q, k_ref[:, sl].astype(bf16), NT_DIMS,
                              preferred_element_type=f32) * HEAD_SCALE + bn_ref[h]
        m = jnp.maximum(jnp.max(s_c, axis=-1, keepdims=True), jnp.max(s_n, axis=-1, keepdims=True))
        p_c = jnp.exp(s_c - m)
        p_n = jnp.exp(s_n - m)
        l = jnp.sum(p_c, axis=-1, keepdims=True) + jnp.sum(p_n, axis=-1, keepdims=True)
        o = (jnp.dot(p_c.astype(bf16), _head_rows(cv_ref, h, True).astype(bf16), preferred_element_type=f32)
             + jnp.dot(p_n.astype(bf16), v_ref[:, sl].astype(bf16), preferred_element_type=f32)) / l
        o_ref[:, sl] = (o * _silu(z_ref[:, sl])).astype(bf16)


def _band_sample(u, cache_k, cache_v, layer, rel_bias, row0, nb, t):
    nrow = cache_k.shape[2] // N_HEADS
    bias_c = _rel_bias_toeplitz(rel_bias, t, nrow, nrow)
    bias_n = _rel_bias_toeplitz(rel_bias, t, t, 0)
    rb0 = row0 // t

    def blk(col):
        return pl.BlockSpec((t, BR_WIDTH), lambda b: (rb0 + b, col // BR_WIDTH))

    cache_spec = pl.BlockSpec((None, None, nrow * N_HEADS, HEAD_DIM), lambda b: (layer, b, 0, 0))
    return pl.pallas_call(
        _band_sample_body,
        out_shape=jax.ShapeDtypeStruct((nb * t, BR_WIDTH), bf16),
        grid=(nb,),
        in_specs=[blk(U_AQ), blk(U_AK), blk(U_AV), blk(U_AZ), cache_spec, cache_spec,
                  pl.BlockSpec((N_HEADS, t, nrow), lambda b: (0, 0, 0)),
                  pl.BlockSpec((N_HEADS, t, t), lambda b: (0, 0, 0))],
        out_specs=pl.BlockSpec((t, BR_WIDTH), lambda b: (b, 0)),
        compiler_params=_params("arbitrary"),
        name="band_attn_sample",
    )(u, u, u, u, cache_k, cache_v, bias_c, bias_n)


def _head_rows(ref, h, interleaved):
    if interleaved:
        return ref[pl.ds(h, ref.shape[0] // N_HEADS, stride=N_HEADS), :]
    return ref[:, h * HEAD_DIM:(h + 1) * HEAD_DIM]


def _cache_rows(cache):
    d0, d1, rows = cache.shape[:3]
    return cache.reshape(d0, d1, rows * N_HEADS, HEAD_DIM)


def _mem_body(q_ref, z_ref, k_ref, v_ref, o_ref, *, interleaved):
    for h in range(N_HEADS):
        sl = slice(h * HEAD_DIM, (h + 1) * HEAD_DIM)
        q = q_ref[:, sl].astype(bf16)
        k = _head_rows(k_ref, h, interleaved).astype(bf16)
        v = _head_rows(v_ref, h, interleaved).astype(bf16)
        s = lax.dot_general(q, k, NT_DIMS, preferred_element_type=f32) * HEAD_SCALE
        p = jnp.exp(s - jnp.max(s, axis=-1, keepdims=True))
        l = jnp.sum(p, axis=-1, keepdims=True)
        o = jnp.dot(p.astype(bf16), v, preferred_element_type=f32) / l
        o_ref[:, sl] = (o * _silu(z_ref[:, sl])).astype(bf16)


def _mem_attn(u, row0, nb, t, tq, k_arr, v_arr, k_spec, v_spec, interleaved, name):
    nq = t // tq
    rb0 = row0 // tq

    def blk(col):
        return pl.BlockSpec((tq, BR_WIDTH), lambda b, i: (rb0 + b * nq + i, col // BR_WIDTH))

    return pl.pallas_call(
        functools.partial(_mem_body, interleaved=interleaved),
        out_shape=jax.ShapeDtypeStruct((nb * t, BR_WIDTH), bf16),
        grid=(nb, nq),
        in_specs=[blk(U_MQ), blk(U_MZ), k_spec, v_spec],
        out_specs=pl.BlockSpec((tq, BR_WIDTH), lambda b, i: (b * nq + i, 0)),
        compiler_params=_params("parallel", "arbitrary"),
        name=name,
    )(u, u, k_arr, v_arr)


def _diff_lambda(lamp_ref, lam_init):
    lp = lamp_ref[...]
    a = jnp.sum(lp[0:1] * lp[1:2], axis=-1, keepdims=True)
    b = jnp.sum(lp[2:3] * lp[3:4], axis=-1, keepdims=True)
    return jnp.exp(a) - jnp.exp(b) + lam_init


def _diff_epilogue(o0, o1, lam, gain, z, lam_init):
    y = _head_norm(o0 - lam * o1, gain) * (1.0 - lam_init)
    return (y * _silu(z)).astype(bf16)


def _half_masks():
    lane = lax.broadcasted_iota(jnp.int32, (1, HEAD_DIM), 1)
    return lane < DIFF_DH, lane >= DIFF_DH


def _diff_prompt_body(q_ref, k_ref, v_ref, z_ref, lamp_ref, gain_ref, o_ref, m_scr, l_scr, acc_scr,
                      *, lam_init):
    tq, tk = q_ref.shape[0], k_ref.shape[0]
    i = pl.program_id(1)
    j = pl.program_id(2)

    @pl.when(j == 0)
    def _():
        m_scr[...] = jnp.full(m_scr.shape, NEG, f32)
        l_scr[...] = jnp.zeros(l_scr.shape, f32)
        acc_scr[...] = jnp.zeros(acc_scr.shape, f32)

    lo, hi = _half_masks()

    def step(diagonal):
        if diagonal:
            kc = lax.broadcasted_iota(jnp.int32, (tk, 2 * tq), 0) // CHUNK
            qpos = lax.broadcasted_iota(jnp.int32, (tk, 2 * tq), 1)
            qc = jnp.where(qpos >= tq, qpos - tq, qpos) // CHUNK
            mask = kc <= qc
        for h in range(N_HEADS):
            sl = slice(h * HEAD_DIM, (h + 1) * HEAD_DIM)
            q = q_ref[:, sl] * DIFF_SCALE
            qq = jnp.concatenate([jnp.where(lo, q, 0.0), jnp.where(hi, q, 0.0)], axis=0).astype(bf16)
            k = k_ref[:, sl].astype(bf16)
            vt = v_ref[:, sl].T.astype(bf16)
            s = lax.dot_general(k, qq, NT_DIMS, preferred_element_type=f32)
            if diagonal:
                s = jnp.where(mask, s, NEG)
            m_old = m_scr[h]
            m_new = jnp.maximum(m_old, jnp.max(s, axis=0, keepdims=True))
            p = jnp.exp(s - m_new)
            alpha = jnp.exp(m_old - m_new)
            l_scr[h] = alpha * l_scr[h] + jnp.sum(p, axis=0, keepdims=True)
            acc_scr[h] = alpha * acc_scr[h] + jnp.dot(vt, p.astype(bf16), preferred_element_type=f32)
            m_scr[h] = m_new

    @pl.when(j < i)
    def _():
        step(False)

    @pl.when(j == i)
    def _():
        step(True)
        lam = _diff_lambda(lamp_ref, lam_init)
        for h in range(N_HEADS):
            sl = slice(h * HEAD_DIM, (h + 1) * HEAD_DIM)
            on = acc_scr[h] / l_scr[h]
            o0 = on[:, :tq].T
            o1 = on[:, tq:].T
            o_ref[:, sl] = _diff_epilogue(o0, o1, lam, gain_ref[:, sl], z_ref[:, sl], lam_init)


def _diff_prompt(u, lamp, gain, lam_init, batch, seq):
    t = _divisor_tile(seq, C_BLOCK, 2 * CHUNK)
    nt = seq // t

    def qblk(col):
        return pl.BlockSpec((t, BR_WIDTH), lambda b, i, j: (b * nt + i, col // BR_WIDTH))

    def kblk(col):
        return pl.BlockSpec((t, BR_WIDTH), lambda b, i, j: (b * nt + jnp.minimum(j, i), col // BR_WIDTH))

    return pl.pallas_call(
        functools.partial(_diff_prompt_body, lam_init=lam_init),
        out_shape=jax.ShapeDtypeStruct((batch * seq, BR_WIDTH), bf16),
        grid=(batch, nt, nt),
        in_specs=[qblk(U_CQ), kblk(U_CK), kblk(U_CV), qblk(U_CZ),
                  pl.BlockSpec((4, DIFF_DH), lambda b, i, j: (0, 0)),
                  pl.BlockSpec((1, BR_WIDTH), lambda b, i, j: (0, 0))],
        out_specs=pl.BlockSpec((t, BR_WIDTH), lambda b, i, j: (b * nt + i, 0)),
        scratch_shapes=[pltpu.VMEM((N_HEADS, 1, 2 * t), f32),
                        pltpu.VMEM((N_HEADS, 1, 2 * t), f32),
                        pltpu.VMEM((N_HEADS, HEAD_DIM, 2 * t), f32)],
        compiler_params=_params("parallel", "parallel", "arbitrary"),
        name="diff_attn_prompt",
    )(u, u, u, u, lamp, gain.reshape(1, BR_WIDTH))


def _diff_sample_body(q_ref, k_ref, v_ref, z_ref, ck_ref, cv_ref, lamp_ref, gain_ref, o_ref, *, lam_init):
    t = q_ref.shape[0]
    lo, hi = _half_masks()
    lam = _diff_lambda(lamp_ref, lam_init)
    for h in range(N_HEADS):
        sl = slice(h * HEAD_DIM, (h + 1) * HEAD_DIM)
        q = q_ref[:, sl] * DIFF_SCALE
        qq = jnp.concatenate([jnp.where(lo, q, 0.0), jnp.where(hi, q, 0.0)], axis=0).astype(bf16)
        k = k_ref[:, sl].astype(bf16)
        v = v_ref[:, sl].astype(bf16)
        ck = _head_rows(ck_ref, h, True).astype(bf16)
        cv = _head_rows(cv_ref, h, True).astype(bf16)
        s_c = lax.dot_general(qq, ck, NT_DIMS, preferred_element_type=f32)
        s_n = lax.dot_general(qq, k, NT_DIMS, preferred_element_type=f32)
        m = jnp.maximum(jnp.max(s_c, axis=-1, keepdims=True), jnp.max(s_n, axis=-1, keepdims=True))
        p_c = jnp.exp(s_c - m)
        p_n = jnp.exp(s_n - m)
        l = jnp.sum(p_c, axis=-1, keepdims=True) + jnp.sum(p_n, axis=-1, keepdims=True)
        o = (jnp.dot(p_c.astype(bf16), cv, preferred_element_type=f32)
             + jnp.dot(p_n.astype(bf16), v, preferred_element_type=f32)) / l
        o_ref[:, sl] = _diff_epilogue(o[:t], o[t:], lam, gain_ref[:, sl], z_ref[:, sl], lam_init)


def _diff_sample(u, cache_k, cache_v, layer, lamp, gain, lam_init, row0, nb, t):
    rb0 = row0 // t

    def blk(col):
        return pl.BlockSpec((t, BR_WIDTH), lambda b: (rb0 + b, col // BR_WIDTH))

    cache_spec = pl.BlockSpec((None, None) + cache_k.shape[2:], lambda b: (layer, b, 0, 0))
    return pl.pallas_call(
        functools.partial(_diff_sample_body, lam_init=lam_init),
        out_shape=jax.ShapeDtypeStruct((nb * t, BR_WIDTH), bf16),
        grid=(nb,),
        in_specs=[blk(U_CQ), blk(U_CK), blk(U_CV), blk(U_CZ), cache_spec, cache_spec,
                  pl.BlockSpec((4, DIFF_DH), lambda b: (0, 0)),
                  pl.BlockSpec((1, BR_WIDTH), lambda b: (0, 0))],
        out_specs=pl.BlockSpec((t, BR_WIDTH), lambda b: (b, 0)),
        compiler_params=_params("arbitrary"),
        name="diff_attn_sample",
    )(u, u, u, u, cache_k, cache_v, lamp, gain.reshape(1, BR_WIDTH))


CONV_PAD = 8


def _mlstm_body(qk_ref, v_ref, og_ref, z_ref, gc_ref, gr_ref, cw_ref, cb_ref, gbr_ref, gbc_ref, gn_ref,
                c0_ref, n0_ref, m0_ref, conv0_ref,
                out_ref, c_out_ref, n_out_ref, m_out_ref,
                cbuf, c_scr, n_scr, m_scr):
    ln = qk_ref.shape[0]
    c = pl.program_id(1)
    lo = CONV_PAD - (CONV_W - 1)

    @pl.when(c == 0)
    def _():
        cbuf[lo:CONV_PAD, :] = conv0_ref[...]
        c_scr[...] = c0_ref[...]
        n_scr[...] = n0_ref[...]
        m_scr[...] = m0_ref[...]

    cbuf[CONV_PAD:CONV_PAD + ln, :] = qk_ref[...]
    conv = cb_ref[...] + cbuf[lo:lo + ln, :] * cw_ref[0:1, :]
    for jj in range(1, CONV_W):
        conv = conv + cbuf[lo + jj:lo + jj + ln, :] * cw_ref[jj:jj + 1, :]
    tail = cbuf[lo + ln:CONV_PAD + ln, :]
    cbuf[lo:CONV_PAD, :] = tail
    act = _silu(conv)

    gcb = gc_ref[...] + gbr_ref[...]
    grb = gr_ref[...] + gbc_ref[...]
    row = lax.broadcasted_iota(jnp.int32, (ln, ln), 0)
    col = lax.broadcasted_iota(jnp.int32, (ln, ln), 1)
    causal = col <= row

    for h in range(N_HEADS):
        sl = slice(h * HEAD_DIM, (h + 1) * HEAD_DIM)
        q = act[:, h * HEAD_DIM:(h + 1) * HEAD_DIM]
        k = act[:, BR_WIDTH + h * HEAD_DIM:BR_WIDTH + (h + 1) * HEAD_DIM] * HEAD_SCALE
        v = v_ref[:, sl]
        i_col = gcb[:, h:h + 1]
        lf_col = _log_sigmoid(gcb[:, N_HEADS + h:N_HEADS + h + 1])
        i_row = grb[h:h + 1, :]
        lf_row = _log_sigmoid(grb[N_HEADS + h:N_HEADS + h + 1, :])
        b_col = jnp.sum(jnp.where(causal, lf_row, 0.0), axis=1, keepdims=True)
        b_row = jnp.sum(jnp.where(row <= col, lf_col, 0.0), axis=0, keepdims=True)
        b_last = b_col[ln - 1:ln, :]
        m_old = m_scr[h][:, 0:1]
        c_old = c_scr[h]
        n_old = n_scr[h]

        dlog = jnp.where(causal, b_col - b_row + i_row, NEG)
        inter = b_col + m_old
        mt = jnp.maximum(inter, jnp.max(dlog, axis=1, keepdims=True))
        w = jnp.exp(dlog - mt)
        wi = jnp.exp(inter - mt)
        qb = q.astype(bf16)
        vb = v.astype(bf16)
        qk = lax.dot_general(qb, k.astype(bf16), NT_DIMS, preferred_element_type=f32) * w
        num = (jnp.dot(qk.astype(bf16), vb, preferred_element_type=f32)
               + wi * jnp.dot(qb, c_old.astype(bf16), preferred_element_type=f32))
        den = jnp.sum(qk, axis=1, keepdims=True) + wi * jnp.sum(q * n_old, axis=1, keepdims=True)
        hh = num / jnp.maximum(jnp.abs(den), jnp.exp(-mt))

        m_new = mt[ln - 1:ln, :]
        ws = jnp.exp(b_last - b_col + i_col - m_new)
        dec = jnp.exp(b_last + m_old - m_new)
        kw = k * ws
        c_scr[h] = dec * c_old + lax.dot_general(kw.astype(bf16), vb, TN_DIMS, preferred_element_type=f32)
        n_scr[h] = dec * n_old + jnp.sum(kw, axis=0, keepdims=True)
        m_scr[h] = jnp.broadcast_to(m_new, (1, HEAD_DIM))

        y = _head_norm(_sigmoid(og_ref[:, sl]) * hh, gn_ref[:, sl])
        out_ref[:, sl] = (y * _silu(z_ref[:, sl])).astype(bf16)

    @pl.when(c == pl.num_programs(1) - 1)
    def _():
        c_out_ref[...] = c_scr[...]
        n_out_ref[...] = n_scr[...]
        m_out_ref[...] = m_scr[...]


def _mlstm(u, gates_r, conv_w, conv_b, b_ig, b_fg, gn, c0, n0, m0, conv0, row0, ng, nc, ln):
    rb0 = row0 // ln
    gbias = jnp.concatenate([b_ig, b_fg]).astype(f32)
    gbr = jnp.zeros((1, GATE_PAD), f32).at[0, :2 * N_HEADS].set(gbias)
    gbc = gbias.reshape(2 * N_HEADS, 1)

    def blk(col, width):
        return pl.BlockSpec((ln, width), lambda g, c: (rb0 + g * nc + c, col // width))

    def const(shape):
        return pl.BlockSpec(shape, lambda g, c: (0,) * len(shape))

    def per_seq(shape):
        return pl.BlockSpec((None,) + shape, lambda g, c: (g,) + (0,) * len(shape))

    state_shapes = [jax.ShapeDtypeStruct((ng, N_HEADS, HEAD_DIM, HEAD_DIM), f32),
                    jax.ShapeDtypeStruct((ng, N_HEADS, 1, HEAD_DIM), f32),
                    jax.ShapeDtypeStruct((ng, N_HEADS, 1, HEAD_DIM), f32)]
    state_specs = [per_seq((N_HEADS, HEAD_DIM, HEAD_DIM)),
                   per_seq((N_HEADS, 1, HEAD_DIM)),
                   per_seq((N_HEADS, 1, HEAD_DIM))]
    out, c_new, n_new, m_new = pl.pallas_call(
        _mlstm_body,
        out_shape=[jax.ShapeDtypeStruct((ng * nc * ln, BR_WIDTH), bf16)] + state_shapes,
        grid=(ng, nc),
        in_specs=[blk(U_BQK, B_QK), blk(U_BV, BR_WIDTH), blk(U_BO, BR_WIDTH), blk(U_BZ, BR_WIDTH),
                  blk(U_GATES, GATE_PAD),
                  pl.BlockSpec((None, 2 * N_HEADS, ln), lambda g, c: (g * nc + c, 0, 0)),
                  const((CONV_W, B_QK)), const((1, B_QK)), const((1, GATE_PAD)), const((2 * N_HEADS, 1)),
                  const((1, BR_WIDTH))] + state_specs + [per_seq((CONV_W - 1, B_QK))],
        out_specs=[pl.BlockSpec((ln, BR_WIDTH), lambda g, c: (g * nc + c, 0))] + state_specs,
        scratch_shapes=[pltpu.VMEM((CONV_PAD + ln, B_QK), f32),
                        pltpu.VMEM((N_HEADS, HEAD_DIM, HEAD_DIM), f32),
                        pltpu.VMEM((N_HEADS, 1, HEAD_DIM), f32),
                        pltpu.VMEM((N_HEADS, 1, HEAD_DIM), f32)],
        compiler_params=_params("parallel", "arbitrary"),
        name="mlstm",
    )(u, u, u, u, u, gates_r, conv_w, conv_b.reshape(1, B_QK), gbr, gbc, gn.reshape(1, BR_WIDTH),
      c0, n0.reshape(ng, N_HEADS, 1, HEAD_DIM),
      jnp.broadcast_to(m0[:, :, None, None], (ng, N_HEADS, 1, HEAD_DIM)), conv0)
    return out, c_new, n_new[:, :, 0, :], m_new[:, :, 0, 0]


def _ret_body(q_ref, k_ref, v_ref, z_ref, cc_ref, ss_ref, gn_ref, s0_ref, out_ref, s_out_ref, s_scr):
    ln = q_ref.shape[0]
    c = pl.program_id(1)

    @pl.when(c == 0)
    def _():
        s_scr[...] = s0_ref[...]

    row = lax.broadcasted_iota(jnp.int32, (ln, ln), 0)
    col = lax.broadcasted_iota(jnp.int32, (ln, ln), 1)
    rel = (row - col).astype(f32)
    tpos = lax.broadcasted_iota(jnp.int32, (ln, 1), 0).astype(f32)
    cc = cc_ref[...]
    ss = ss_ref[...]
    for h in range(N_HEADS):
        sl = slice(h * HEAD_DIM, (h + 1) * HEAD_DIM)
        lg = math.log(1.0 - 2.0 ** (-RET_GAMMA_EXP0 - h))
        q = q_ref[:, sl]
        k = k_ref[:, sl]
        qr = q * cc + pltpu.roll(q, DIFF_DH, 1) * ss
        kr = (k * cc + pltpu.roll(k, DIFF_DH, 1) * ss) * HEAD_SCALE
        vb = v_ref[:, sl].astype(bf16)
        decay = jnp.where(rel >= 0.0, jnp.exp(jnp.maximum(rel, 0.0) * lg), 0.0)
        att = lax.dot_general(qr.astype(bf16), kr.astype(bf16), NT_DIMS, preferred_element_type=f32) * decay
        q_dec = qr * jnp.exp((tpos + 1.0) * lg)
        s_old = s_scr[h]
        o = (jnp.dot(att.astype(bf16), vb, preferred_element_type=f32)
             + jnp.dot(q_dec.astype(bf16), s_old.astype(bf16), preferred_element_type=f32))
        k_dec = kr * jnp.exp((ln - 1.0 - tpos) * lg)
        s_scr[h] = math.exp(ln * lg) * s_old + lax.dot_general(k_dec.astype(bf16), vb, TN_DIMS,
                                                               preferred_element_type=f32)
        out_ref[:, sl] = (_head_norm(o, gn_ref[:, sl]) * _silu(z_ref[:, sl])).astype(bf16)

    @pl.when(c == pl.num_programs(1) - 1)
    def _():
        s_out_ref[...] = s_scr[...]


def _rope_tables(pos):
    inv = ROPE_BASE ** (-jnp.arange(0, HEAD_DIM, 2, dtype=f32) / HEAD_DIM)
    ang = pos.astype(f32)[:, None] * inv[None, :]
    cos, sin = jnp.cos(ang), jnp.sin(ang)
    return jnp.concatenate([cos, cos], axis=-1), jnp.concatenate([-sin, sin], axis=-1)


def _retention(u, pos, gn, s0, row0, ng, nc, ln):
    rb0 = row0 // ln
    cc, ss = _rope_tables(pos)

    def blk(col):
        return pl.BlockSpec((ln, BR_WIDTH), lambda g, c: (rb0 + g * nc + c, col // BR_WIDTH))

    rope_spec = pl.BlockSpec((ln, HEAD_DIM), lambda g, c: (c, 0))
    state_spec = pl.BlockSpec((None, N_HEADS, HEAD_DIM, HEAD_DIM), lambda g, c: (g, 0, 0, 0))
    return pl.pallas_call(
        _ret_body,
        out_shape=[jax.ShapeDtypeStruct((ng * nc * ln, BR_WIDTH), bf16),
                   jax.ShapeDtypeStruct((ng, N_HEADS, HEAD_DIM, HEAD_DIM), f32)],
        grid=(ng, nc),
        in_specs=[blk(U_DQ), blk(U_DK), blk(U_DV), blk(U_DZ), rope_spec, rope_spec,
                  pl.BlockSpec((1, BR_WIDTH), lambda g, c: (0, 0)), state_spec],
        out_specs=[pl.BlockSpec((ln, BR_WIDTH), lambda g, c: (g * nc + c, 0)), state_spec],
        scratch_shapes=[pltpu.VMEM((N_HEADS, HEAD_DIM, HEAD_DIM), f32)],
        compiler_params=_params("parallel", "arbitrary"),
        name="retention",
    )(u, u, u, u, cc, ss, gn.reshape(1, BR_WIDTH), s0)


def _merge_body(x_ref, g_ref, oa_ref, ob_ref, oc_ref, od_ref, om_ref,
                wg0_ref, wg1_ref, wg2_ref, wg3_ref, wg4_ref, wb_ref, wo_ref, fg_ref,
                y_ref, h_scr, acc_scr, *, rows, final):
    n = pl.program_id(1)

    @pl.when(n == 0)
    def _():
        _rmsnorm_rows(x_ref, g_ref, h_scr, rows)
        acc_scr[...] = jnp.zeros(acc_scr.shape, f32)

    h = h_scr[...]
    merged = None
    for i, (o_ref, wg_ref) in enumerate(zip((oa_ref, ob_ref, oc_ref, od_ref, om_ref),
                                            (wg0_ref, wg1_ref, wg2_ref, wg3_ref, wg4_ref))):
        gate = _sigmoid(jnp.dot(h, wg_ref[...], preferred_element_type=f32))
        term = gate * jnp.dot(o_ref[...], wb_ref[i], preferred_element_type=f32)
        merged = term if merged is None else merged + term
    acc_scr[...] += jnp.dot(merged.astype(bf16), wo_ref[...], preferred_element_type=f32)

    @pl.when(n == pl.num_programs(1) - 1)
    def _():
        tm = x_ref.shape[0]
        for r in range(0, tm, rows):
            y = x_ref[r:r + rows, :] + acc_scr[r:r + rows, :]
            if final:
                y = (y * lax.rsqrt(jnp.mean(y * y, axis=-1, keepdims=True) + EPS)) * fg_ref[...]
            y_ref[r:r + rows, :] = y


def _merge(x, g, outs, wg, wb, wo, fg, *, final, tm_target=640, tn=256):
    m, d = x.shape
    tm = _divisor_tile(m, tm_target, 16)
    rows = _divisor_tile(tm, 256, 8)
    nn = d // tn

    def wg_spec(i):
        return pl.BlockSpec((d, tn), lambda r, n: (0, i * nn + n))

    o_spec = pl.BlockSpec((tm, BR_WIDTH), lambda r, n: (r, 0))
    return pl.pallas_call(
        functools.partial(_merge_body, rows=rows, final=final),
        out_shape=jax.ShapeDtypeStruct((m, d), f32),
        grid=(m // tm, nn),
        in_specs=[pl.BlockSpec((tm, d), lambda r, n: (r, 0)),
                  pl.BlockSpec((1, d), lambda r, n: (0, 0))]
                 + [o_spec] * N_BRANCH
                 + [wg_spec(i) for i in range(N_BRANCH)]
                 + [pl.BlockSpec((N_BRANCH, BR_WIDTH, tn), lambda r, n: (0, 0, n)),
                    pl.BlockSpec((tn, d), lambda r, n: (n, 0)),
                    pl.BlockSpec((1, d), lambda r, n: (0, 0))],
        out_specs=pl.BlockSpec((tm, d), lambda r, n: (r, 0)),
        scratch_shapes=[pltpu.VMEM((tm, d), bf16), pltpu.VMEM((tm, d), f32)],
        compiler_params=_params("parallel", "arbitrary"),
        name="gated_merge",
    )(x, g.reshape(1, d), *outs, wg, wg, wg, wg, wg, wb, wo, fg.reshape(1, d))


def _repack_w_in(w):
    pad = jnp.zeros((w.shape[0], GATE_PAD - 2 * N_HEADS), w.dtype)
    wu = jnp.concatenate([w[:, :W_BI], w[:, W_BZ:W_GATE], w[:, W_BI:W_BZ], pad], axis=1)
    return wu.astype(bf16), w[:, W_GATE:].astype(bf16)


def _gates_rowform(u, row0, nrows, ln):
    g = u[row0:row0 + nrows, U_GATES:U_GATES + 2 * N_HEADS]
    return g.reshape(nrows // ln, ln, 2 * N_HEADS).transpose(0, 2, 1)


def kernel(x_prompt, x_sample, mem_prompt, cache_a_k, cache_a_v, cache_c_k, cache_c_v, cache_mem_k, cache_mem_v, state_b_C, state_b_n, state_b_m, state_b_conv, state_d_S, norm_g, w_in, conv_w, conv_b, b_ig, b_fg, rel_bias, lam_q1, lam_k1, lam_q2, lam_k2, gn_b, subln_c, gn_d, mem_norm_g, w_mk, w_mv, w_branch, w_out, final_g):
    batch, seq, d = x_prompt.shape
    nb, t, _ = x_sample.shape
    depth = w_in.shape[0]
    n_mem = mem_prompt.shape[1]
    past = cache_c_k.shape[2]
    mp = batch * seq
    assert d == D_MODEL and seq % C_BLOCK == 0 and seq % A_QBLOCK == 0 and seq % SCAN_CHUNK == 0
    assert mp % t == 0

    x = jnp.concatenate([x_prompt.reshape(mp, d), x_sample.reshape(nb * t, d)], axis=0)
    mem = mem_prompt.reshape(batch * n_mem, d)
    ca_k, ca_v = _cache_rows(cache_a_k), _cache_rows(cache_a_v)
    cc_k, cc_v = _cache_rows(cache_c_k), _cache_rows(cache_c_v)
    cm_k, cm_v = _cache_rows(cache_mem_k), _cache_rows(cache_mem_v)
    pos_p = jnp.arange(seq)
    pos_s = past + jnp.arange(t)
    a_rows = min(A_WINDOW, seq)
    mem_tq = _divisor_tile(seq, 512, 16)

    sp, ss = [], []
    for l in range(depth):
        lam_init = 0.8 - 0.6 * math.exp(-0.3 * l)
        lamp = jnp.stack([lam_q1[l], lam_k1[l], lam_q2[l], lam_k2[l]]).astype(f32)
        wu, wg = _repack_w_in(w_in[l])
        u = _norm_matmul(x, norm_g[l], wu, tm_target=768, tn=896, name="in_proj")
        mkv = _norm_matmul(mem, mem_norm_g[l], jnp.concatenate([w_mk[l], w_mv[l]], axis=1).astype(bf16),
                           tm_target=512, tn=512, name="mem_proj")

        oa_p = _band_prompt(u, rel_bias[l], batch, seq)
        zeros_c = jnp.zeros((batch, N_HEADS, HEAD_DIM, HEAD_DIM), f32)
        ob_p, c_p, n_p, m_p = _mlstm(
            u, _gates_rowform(u, 0, mp, SCAN_CHUNK), conv_w[l], conv_b[l], b_ig[l], b_fg[l], gn_b[l],
            zeros_c, jnp.zeros((batch, N_HEADS, HEAD_DIM), f32), jnp.zeros((batch, N_HEADS), f32),
            jnp.zeros((batch, CONV_W - 1, B_QK), f32), 0, batch, seq // SCAN_CHUNK, SCAN_CHUNK)
        oc_p = _diff_prompt(u, lamp, subln_c[l], lam_init, batch, seq)
        od_p, s_p = _retention(u, pos_p, gn_d[l], zeros_c, 0, batch, seq // SCAN_CHUNK, SCAN_CHUNK)
        mem_spec_k = pl.BlockSpec((n_mem, BR_WIDTH), lambda b, i: (b, 0))
        mem_spec_v = pl.BlockSpec((n_mem, BR_WIDTH), lambda b, i: (b, 1))
        om_p = _mem_attn(u, 0, batch, seq, mem_tq, mkv, mkv, mem_spec_k, mem_spec_v, False, "mem_attn_prompt")

        oa_s = _band_sample(u, ca_k, ca_v, l, rel_bias[l], mp, nb, t)
        ob_s, c_s, n_s, m_s = _mlstm(
            u, _gates_rowform(u, mp, nb * t, t), conv_w[l], conv_b[l], b_ig[l], b_fg[l], gn_b[l],
            state_b_C[l].astype(f32), state_b_n[l].astype(f32), state_b_m[l].astype(f32),
            state_b_conv[l].astype(f32), mp, nb, 1, t)
        oc_s = _diff_sample(u, cc_k, cc_v, l, lamp, subln_c[l], lam_init, mp, nb, t)
        od_s, s_s = _retention(u, pos_s, gn_d[l], state_d_S[l].astype(f32), mp, nb, 1, t)
        cache_spec = pl.BlockSpec((None, None, n_mem * N_HEADS, HEAD_DIM), lambda b, i: (l, b, 0, 0))
        om_s = _mem_attn(u, mp, nb, t, t, cm_k, cm_v, cache_spec, cache_spec, True, "mem_attn_sample")

        outs = [jnp.concatenate([p, s], axis=0) for p, s in
                ((oa_p, oa_s), (ob_p, ob_s), (oc_p, oc_s), (od_p, od_s), (om_p, om_s))]
        x = _merge(x, norm_g[l], outs, wg, w_branch[l].astype(bf16), w_out[l].astype(bf16), final_g,
                   final=(l == depth - 1))

        def seg_p(col, width, first_row):
            return u[:mp, col:col + width].reshape(batch, seq, width)[:, first_row:]

        def seg_s(col, width, first_row):
            return u[mp:, col:col + width].reshape(nb, t, width)[:, first_row:]

        def hd(a):
            return a.reshape(a.shape[:2] + (N_HEADS, HEAD_DIM))

        sp.append((hd(seg_p(U_AK, BR_WIDTH, seq - a_rows)), hd(seg_p(U_AV, BR_WIDTH, seq - a_rows)),
                   hd(seg_p(U_CK, BR_WIDTH, 0)), hd(seg_p(U_CV, BR_WIDTH, 0)),
                   hd(mkv[:, :BR_WIDTH].reshape(batch, n_mem, BR_WIDTH)),
                   hd(mkv[:, BR_WIDTH:].reshape(batch, n_mem, BR_WIDTH)),
                   c_p, n_p, m_p, seg_p(U_BQK, B_QK, seq - (CONV_W - 1)), s_p))
        ss.append((hd(seg_s(U_AK, BR_WIDTH, 0)), hd(seg_s(U_AV, BR_WIDTH, 0)),
                   hd(seg_s(U_CK, BR_WIDTH, 0)), hd(seg_s(U_CV, BR_WIDTH, 0)),
                   c_s, n_s, m_s, seg_s(U_BQK, B_QK, t - (CONV_W - 1)), s_s))

    y_prompt = x[:mp].reshape(batch, seq, d)
    y_sample = x[mp:].reshape(nb, t, d)
    p_states = tuple(jnp.stack([st[i] for st in sp]) for i in range(11))
    s_states = tuple(jnp.stack([st[i] for st in ss]) for i in range(9))
    return (y_prompt, y_sample) + p_states + s_states
```

```python
import functools
import math

import jax
import jax.numpy as jnp
import numpy as np
from jax import lax
from jax.experimental import pallas as pl
from jax.experimental.pallas import tpu as pltpu

f32 = jnp.float32
bf16 = jnp.bfloat16

D_MODEL = 2048
N_HEADS = 4
HEAD_DIM = 128
BR_WIDTH = N_HEADS * HEAD_DIM
N_BRANCH = 5
CHUNK = 64
A_PREV_CHUNKS = 8
A_WINDOW = A_PREV_CHUNKS * CHUNK
A_BAND = (A_PREV_CHUNKS + 1) * CHUNK
A_REL_CLIP = 128
CONV_W = 4
B_QK = 2 * BR_WIDTH
DIFF_DH = HEAD_DIM // 2
RET_GAMMA_EXP0 = 5.0
ROPE_BASE = 10000.0
EPS = 1e-6
NEG = -1e30
HEAD_SCALE = HEAD_DIM ** -0.5
DIFF_SCALE = DIFF_DH ** -0.5

U_AQ, U_AK, U_AV, U_AZ = 0, 512, 1024, 1536
U_BQK, U_BV, U_BO, U_BZ = 2048, 3072, 3584, 4096
U_CQ, U_CK, U_CV, U_CZ = 4608, 5120, 5632, 6144
U_DQ, U_DK, U_DV, U_DZ = 6656, 7168, 7680, 8192
U_MQ, U_MZ = 8704, 9216
U_GATES = 9728
GATE_PAD = 128
U_WIDTH = U_GATES + GATE_PAD
W_BI = 4096
W_BZ = 4104
W_GATE = 9736

VMEM_LIMIT_BYTES = 56 * 1024 * 1024
A_QBLOCK = 4 * CHUNK
C_BLOCK = 512
SCAN_CHUNK = 256
NT_DIMS = (((1,), (1,)), ((), ()))
TN_DIMS = (((0,), (0,)), ((), ()))


def _divisor_tile(n, target, multiple):
    best = None
    for t in range(multiple, min(n, target) + 1, multiple):
        if n % t == 0:
            best = t
    assert best is not None, (n, target, multiple)
    return best


def _params(*sem):
    return pltpu.CompilerParams(dimension_semantics=sem, vmem_limit_bytes=VMEM_LIMIT_BYTES)


def _sigmoid(x):
    return 1.0 / (1.0 + jnp.exp(-x))


def _silu(x):
    return x * _sigmoid(x)


def _log_sigmoid(x):
    return jnp.minimum(x, 0.0) - jnp.log1p(jnp.exp(-jnp.abs(x)))


def _head_norm(y, gain):
    return y * lax.rsqrt(jnp.mean(y * y, axis=-1, keepdims=True) + EPS) * gain


def _rmsnorm_rows(x_ref, g_ref, h_ref, rows):
    tm = x_ref.shape[0]
    for r in range(0, tm, rows):
        x = x_ref[r:r + rows, :]
        ms = jnp.mean(x * x, axis=-1, keepdims=True)
        h_ref[r:r + rows, :] = ((x * lax.rsqrt(ms + EPS)) * g_ref[...]).astype(bf16)


def _norm_matmul_body(x_ref, g_ref, w_ref, o_ref, h_ref, *, rows):
    @pl.when(pl.program_id(1) == 0)
    def _():
        _rmsnorm_rows(x_ref, g_ref, h_ref, rows)

    o_ref[...] = jnp.dot(h_ref[...], w_ref[...], preferred_element_type=f32).astype(o_ref.dtype)


def _norm_matmul(x, g, w, *, tm_target, tn, name):
    m, d = x.shape
    n = w.shape[1]
    tm = _divisor_tile(m, tm_target, 16)
    rows = _divisor_tile(tm, 256, 8)
    assert n % tn == 0
    return pl.pallas_call(
        functools.partial(_norm_matmul_body, rows=rows),
        out_shape=jax.ShapeDtypeStruct((m, n), f32),
        grid=(m // tm, n // tn),
        in_specs=[pl.BlockSpec((tm, d), lambda i, j: (i, 0)),
                  pl.BlockSpec((1, d), lambda i, j: (0, 0)),
                  pl.BlockSpec((d, tn), lambda i, j: (0, j))],
        out_specs=pl.BlockSpec((tm, tn), lambda i, j: (i, j)),
        scratch_shapes=[pltpu.VMEM((tm, d), bf16)],
        compiler_params=_params("parallel", "arbitrary"),
        name=name,
    )(x, g.reshape(1, d), w)


def _store_head_rows(dst_ref, src_ref):
    rows = src_ref.shape[0]
    for h in range(N_HEADS):
        dst_ref[pl.ds(h, rows, stride=N_HEADS), :] = src_ref[:, h * HEAD_DIM:(h + 1) * HEAD_DIM]


def _band_prompt_body(q_ref, k0_ref, k1_ref, k2_ref, v0_ref, v1_ref, v2_ref, z_ref, bias_ref,
                      o_ref, ks_ref, vs_ref, *, n_tail):
    tq = q_ref.shape[0]
    i = pl.program_id(1)

    @pl.when(i >= pl.num_programs(1) - n_tail)
    def _():
        _store_head_rows(ks_ref, k2_ref)
        _store_head_rows(vs_ref, v2_ref)

    w_idx = lax.broadcasted_iota(jnp.int32, (tq, 3 * tq), 1)
    valid = (w_idx + (i - 2) * tq) >= 0
    for h in range(N_HEADS):
        sl = slice(h * HEAD_DIM, (h + 1) * HEAD_DIM)
        q = q_ref[:, sl].astype(bf16)
        kw = jnp.concatenate([k0_ref[:, sl], k1_ref[:, sl], k2_ref[:, sl]], axis=0).astype(bf16)
        vw = jnp.concatenate([v0_ref[:, sl], v1_ref[:, sl], v2_ref[:, sl]], axis=0).astype(bf16)
        s = lax.dot_general(q, kw, NT_DIMS, preferred_element_type=f32) * HEAD_SCALE + bias_ref[h]
        s = jnp.where(valid, s, NEG)
        p = jnp.exp(s - jnp.max(s, axis=-1, keepdims=True))
        l = jnp.sum(p, axis=-1, keepdims=True)
        o = jnp.dot(p.astype(bf16), vw, preferred_element_type=f32) / l
        o_ref[:, sl] = (o * _silu(z_ref[:, sl])).astype(bf16)


def _rel_bias_toeplitz(rel_bias, rows, cols, rel0):
    p = rows + cols
    j = np.arange(p)
    d = np.where(j < cols, j, j - p)
    idx = np.clip(rel0 - d, -A_REL_CLIP, A_REL_CLIP) + A_REL_CLIP
    v = rel_bias.astype(f32)[:, idx]
    flat = jnp.tile(v, (1, rows))[:, :rows * (p - 1)]
    return flat.reshape(rel_bias.shape[0], rows, p - 1)[:, :, :cols]


def _band_bias_prompt(rel_bias):
    r = np.arange(A_QBLOCK)[:, None]
    w = np.arange(3 * A_QBLOCK)[None, :]
    kj = w - CHUNK * (r // CHUNK)
    inside = (kj >= 0) & (kj < A_BAND)
    table = _rel_bias_toeplitz(rel_bias, A_QBLOCK, 3 * A_QBLOCK, 2 * A_QBLOCK)
    return jnp.where(jnp.asarray(inside)[None], table, NEG)


def _band_prompt(u, rel_bias, batch, seq, a_rows):
    tq = A_QBLOCK
    nqb = seq // tq
    assert a_rows % tq == 0
    n_tail = a_rows // tq
    bias = _band_bias_prompt(rel_bias)

    def blk(col, back):
        return pl.BlockSpec((tq, BR_WIDTH),
                            lambda b, i: (b * nqb + jnp.maximum(i - back, 0), col // BR_WIDTH))

    state_shape = jax.ShapeDtypeStruct((batch, a_rows * N_HEADS, HEAD_DIM), f32)
    state_spec = pl.BlockSpec((None, tq * N_HEADS, HEAD_DIM),
                              lambda b, i: (b, jnp.maximum(i - (nqb - n_tail), 0), 0))
    return pl.pallas_call(
        functools.partial(_band_prompt_body, n_tail=n_tail),
        out_shape=[jax.ShapeDtypeStruct((batch * seq, BR_WIDTH), bf16), state_shape, state_shape],
        grid=(batch, nqb),
        in_specs=[blk(U_AQ, 0),
                  blk(U_AK, 2), blk(U_AK, 1), blk(U_AK, 0),
                  blk(U_AV, 2), blk(U_AV, 1), blk(U_AV, 0),
                  blk(U_AZ, 0),
                  pl.BlockSpec((N_HEADS, tq, 3 * tq), lambda b, i: (0, 0, 0))],
        out_specs=[pl.BlockSpec((tq, BR_WIDTH), lambda b, i: (b * nqb + i, 0)), state_spec, state_spec],
        compiler_params=_params("parallel", "arbitrary"),
        name="band_attn_prompt",
    )(u, u, u, u, u, u, u, u, bias)


def _band_sample_body(q_ref, k_ref, v_ref, z_ref, ck_ref, cv_ref, bc_ref, bn_ref, o_ref, ks_ref, vs_ref):
    _store_head_rows(ks_ref, k_ref)
    _store_head_rows(vs_ref, v_ref)
    for h in range(N_HEADS):
        sl = slice(h * HEAD_DIM, (h + 1) * HEAD_DIM)
        q = q_ref[:, sl].astype(bf16)
        s_c = lax.dot_general(q, _head_rows(ck_ref, h, True).astype(bf16), NT_DIMS,
                              preferred_element_type=f32) * HEAD_SCALE + bc_ref[h]
        s_n = lax.dot_general(q, k_ref[:, sl].astype(bf16), NT_DIMS,
                              preferred_element_type=f32) * HEAD_SCALE + bn_ref[h]
        m = jnp.maximum(jnp.max(s_c, axis=-1, keepdims=True), jnp.max(s_n, axis=-1, keepdims=True))
        p_c = jnp.exp(s_c - m)
        p_n = jnp.exp(s_n - m)
        l = jnp.sum(p_c, axis=-1, keepdims=True) + jnp.sum(p_n, axis=-1, keepdims=True)
        o = (jnp.dot(p_c.astype(bf16), _head_rows(cv_ref, h, True).astype(bf16), preferred_element_type=f32)
             + jnp.dot(p_n.astype(bf16), v_ref[:, sl].astype(bf16), preferred_element_type=f32)) / l
        o_ref[:, sl] = (o * _silu(z_ref[:, sl])).astype(bf16)


def _band_sample(u, cache_k, cache_v, layer, rel_bias, row0, nb, t):
    nrow = cache_k.shape[2] // N_HEADS
    bias_c = _rel_bias_toeplitz(rel_bias, t, nrow, nrow)
    bias_n = _rel_bias_toeplitz(rel_bias, t, t, 0)
    rb0 = row0 // t

    def blk(col):
        return pl.BlockSpec((t, BR_WIDTH), lambda b: (rb0 + b, col // BR_WIDTH))

    cache_spec = pl.BlockSpec((None, None, nrow * N_HEADS, HEAD_DIM), lambda b: (layer, b, 0, 0))
    state_shape = jax.ShapeDtypeStruct((nb, t * N_HEADS, HEAD_DIM), f32)
    state_spec = pl.BlockSpec((None, t * N_HEADS, HEAD_DIM), lambda b: (b, 0, 0))
    return pl.pallas_call(
        _band_sample_body,
        out_shape=[jax.ShapeDtypeStruct((nb * t, BR_WIDTH), bf16), state_shape, state_shape],
        grid=(nb,),
        in_specs=[blk(U_AQ), blk(U_AK), blk(U_AV), blk(U_AZ), cache_spec, cache_spec,
                  pl.BlockSpec((N_HEADS, t, nrow), lambda b: (0, 0, 0)),
                  pl.BlockSpec((N_HEADS, t, t), lambda b: (0, 0, 0))],
        out_specs=[pl.BlockSpec((t, BR_WIDTH), lambda b: (b, 0)), state_spec, state_spec],
        compiler_params=_params("arbitrary"),
        name="band_attn_sample",
    )(u, u, u, u, cache_k, cache_v, bias_c, bias_n)


def _head_rows(ref, h, interleaved):
    if interleaved:
        return ref[pl.ds(h, ref.shape[0] // N_HEADS, stride=N_HEADS), :]
    return ref[:, h * HEAD_DIM:(h + 1) * HEAD_DIM]


def _cache_rows(cache):
    d0, d1, rows = cache.shape[:3]
    return cache.reshape(d0, d1, rows * N_HEADS, HEAD_DIM)


def _mem_body(q_ref, z_ref, k_ref, v_ref, o_ref, *, interleaved):
    for h in range(N_HEADS):
        sl = slice(h * HEAD_DIM, (h + 1) * HEAD_DIM)
        q = q_ref[:, sl].astype(bf16)
        k = _head_rows(k_ref, h, interleaved).astype(bf16)
        v = _head_rows(v_ref, h, interleaved).astype(bf16)
        s = lax.dot_general(q, k, NT_DIMS, preferred_element_type=f32) * HEAD_SCALE
        p = jnp.exp(s - jnp.max(s, axis=-1, keepdims=True))
        l = jnp.sum(p, axis=-1, keepdims=True)
        o = jnp.dot(p.astype(bf16), v, preferred_element_type=f32) / l
        o_ref[:, sl] = (o * _silu(z_ref[:, sl])).astype(bf16)


def _mem_attn(u, row0, nb, t, tq, k_arr, v_arr, k_spec, v_spec, interleaved, name):
    nq = t // tq
    rb0 = row0 // tq

    def blk(col):
        return pl.BlockSpec((tq, BR_WIDTH), lambda b, i: (rb0 + b * nq + i, col // BR_WIDTH))

    return pl.pallas_call(
        functools.partial(_mem_body, interleaved=interleaved),
        out_shape=jax.ShapeDtypeStruct((nb * t, BR_WIDTH), bf16),
        grid=(nb, nq),
        in_specs=[blk(U_MQ), blk(U_MZ), k_spec, v_spec],
        out_specs=pl.BlockSpec((tq, BR_WIDTH), lambda b, i: (b * nq + i, 0)),
        compiler_params=_params("parallel", "arbitrary"),
        name=name,
    )(u, u, k_arr, v_arr)


def _diff_lambda(lamp_ref, lam_init):
    lp = lamp_ref[...]
    a = jnp.sum(lp[0:1] * lp[1:2], axis=-1, keepdims=True)
    b = jnp.sum(lp[2:3] * lp[3:4], axis=-1, keepdims=True)
    return jnp.exp(a) - jnp.exp(b) + lam_init


def _diff_epilogue(o0, o1, lam, gain, z, lam_init):
    y = _head_norm(o0 - lam * o1, gain) * (1.0 - lam_init)
    return (y * _silu(z)).astype(bf16)


def _half_masks():
    lane = lax.broadcasted_iota(jnp.int32, (1, HEAD_DIM), 1)
    return lane < DIFF_DH, lane >= DIFF_DH


def _diff_prompt_body(q_ref, k_ref, v_ref, z_ref, lamp_ref, gain_ref, o_ref, ks_ref, vs_ref,
                      m_scr, l_scr, acc_scr, *, lam_init):
    tq, tk = q_ref.shape[0], k_ref.shape[0]
    i = pl.program_id(1)
    j = pl.program_id(2)

    @pl.when(j == 0)
    def _():
        m_scr[...] = jnp.full(m_scr.shape, NEG, f32)
        l_scr[...] = jnp.zeros(l_scr.shape, f32)
        acc_scr[...] = jnp.zeros(acc_scr.shape, f32)

    lo, hi = _half_masks()

    def step(diagonal):
        if diagonal:
            kc = lax.broadcasted_iota(jnp.int32, (tk, 2 * tq), 0) // CHUNK
            qpos = lax.broadcasted_iota(jnp.int32, (tk, 2 * tq), 1)
            qc = jnp.where(qpos >= tq, qpos - tq, qpos) // CHUNK
            mask = kc <= qc
        for h in range(N_HEADS):
            sl = slice(h * HEAD_DIM, (h + 1) * HEAD_DIM)
            q = q_ref[:, sl] * DIFF_SCALE
            qq = jnp.concatenate([jnp.where(lo, q, 0.0), jnp.where(hi, q, 0.0)], axis=0).astype(bf16)
            k = k_ref[:, sl].astype(bf16)
            vt = v_ref[:, sl].T.astype(bf16)
            s = lax.dot_general(k, qq, NT_DIMS, preferred_element_type=f32)
            if diagonal:
                s = jnp.where(mask, s, NEG)
            m_old = m_scr[h]
            m_new = jnp.maximum(m_old, jnp.max(s, axis=0, keepdims=True))
            p = jnp.exp(s - m_new)
            alpha = jnp.exp(m_old - m_new)
            l_scr[h] = alpha * l_scr[h] + jnp.sum(p, axis=0, keepdims=True)
            acc_scr[h] = alpha * acc_scr[h] + jnp.dot(vt, p.astype(bf16), preferred_element_type=f32)
            m_scr[h] = m_new

    @pl.when(j < i)
    def _():
        step(False)

    @pl.when(j == i)
    def _():
        step(True)
        _store_head_rows(ks_ref, k_ref)
        _store_head_rows(vs_ref, v_ref)
        lam = _diff_lambda(lamp_ref, lam_init)
        for h in range(N_HEADS):
            sl = slice(h * HEAD_DIM, (h + 1) * HEAD_DIM)
            on = acc_scr[h] / l_scr[h]
            o0 = on[:, :tq].T
            o1 = on[:, tq:].T
            o_ref[:, sl] = _diff_epilogue(o0, o1, lam, gain_ref[:, sl], z_ref[:, sl], lam_init)


def _diff_prompt(u, lamp, gain, lam_init, batch, seq):
    t = _divisor_tile(seq, C_BLOCK, 2 * CHUNK)
    nt = seq // t

    def qblk(col):
        return pl.BlockSpec((t, BR_WIDTH), lambda b, i, j: (b * nt + i, col // BR_WIDTH))

    def kblk(col):
        return pl.BlockSpec((t, BR_WIDTH), lambda b, i, j: (b * nt + jnp.minimum(j, i), col // BR_WIDTH))

    state_shape = jax.ShapeDtypeStruct((batch, seq * N_HEADS, HEAD_DIM), f32)
    state_spec = pl.BlockSpec((None, t * N_HEADS, HEAD_DIM), lambda b, i, j: (b, i, 0))
    return pl.pallas_call(
        functools.partial(_diff_prompt_body, lam_init=lam_init),
        out_shape=[jax.ShapeDtypeStruct((batch * seq, BR_WIDTH), bf16), state_shape, state_shape],
        grid=(batch, nt, nt),
        in_specs=[qblk(U_CQ), kblk(U_CK), kblk(U_CV), qblk(U_CZ),
                  pl.BlockSpec((4, DIFF_DH), lambda b, i, j: (0, 0)),
                  pl.BlockSpec((1, BR_WIDTH), lambda b, i, j: (0, 0))],
        out_specs=[pl.BlockSpec((t, BR_WIDTH), lambda b, i, j: (b * nt + i, 0)), state_spec, state_spec],
        scratch_shapes=[pltpu.VMEM((N_HEADS, 1, 2 * t), f32),
                        pltpu.VMEM((N_HEADS, 1, 2 * t), f32),
                        pltpu.VMEM((N_HEADS, HEAD_DIM, 2 * t), f32)],
        compiler_params=_params("parallel", "parallel", "arbitrary"),
        name="diff_attn_prompt",
    )(u, u, u, u, lamp, gain.reshape(1, BR_WIDTH))


def _diff_sample_body(q_ref, k_ref, v_ref, z_ref, ck_ref, cv_ref, lamp_ref, gain_ref, o_ref, ks_ref, vs_ref,
                      *, lam_init):
    t = q_ref.shape[0]
    _store_head_rows(ks_ref, k_ref)
    _store_head_rows(vs_ref, v_ref)
    lo, hi = _half_masks()
    lam = _diff_lambda(lamp_ref, lam_init)
    for h in range(N_HEADS):
        sl = slice(h * HEAD_DIM, (h + 1) * HEAD_DIM)
        q = q_ref[:, sl] * DIFF_SCALE
        qq = jnp.concatenate([jnp.where(lo, q, 0.0), jnp.where(hi, q, 0.0)], axis=0).astype(bf16)
        k = k_ref[:, sl].astype(bf16)
        v = v_ref[:, sl].astype(bf16)
        ck = _head_rows(ck_ref, h, True).astype(bf16)
        cv = _head_rows(cv_ref, h, True).astype(bf16)
        s_c = lax.dot_general(qq, ck, NT_DIMS, preferred_element_type=f32)
        s_n = lax.dot_general(qq, k, NT_DIMS, preferred_element_type=f32)
        m = jnp.maximum(jnp.max(s_c, axis=-1, keepdims=True), jnp.max(s_n, axis=-1, keepdims=True))
        p_c = jnp.exp(s_c - m)
        p_n = jnp.exp(s_n - m)
        l = jnp.sum(p_c, axis=-1, keepdims=True) + jnp.sum(p_n, axis=-1, keepdims=True)
        o = (jnp.dot(p_c.astype(bf16), cv, preferred_element_type=f32)
             + jnp.dot(p_n.astype(bf16), v, preferred_element_type=f32)) / l
        o_ref[:, sl] = _diff_epilogue(o[:t], o[t:], lam, gain_ref[:, sl], z_ref[:, sl], lam_init)


def _diff_sample(u, cache_k, cache_v, layer, lamp, gain, lam_init, row0, nb, t):
    rb0 = row0 // t

    def blk(col):
        return pl.BlockSpec((t, BR_WIDTH), lambda b: (rb0 + b, col // BR_WIDTH))

    cache_spec = pl.BlockSpec((None, None) + cache_k.shape[2:], lambda b: (layer, b, 0, 0))
    state_shape = jax.ShapeDtypeStruct((nb, t * N_HEADS, HEAD_DIM), f32)
    state_spec = pl.BlockSpec((None, t * N_HEADS, HEAD_DIM), lambda b: (b, 0, 0))
    return pl.pallas_call(
        functools.partial(_diff_sample_body, lam_init=lam_init),
        out_shape=[jax.ShapeDtypeStruct((nb * t, BR_WIDTH), bf16), state_shape, state_shape],
        grid=(nb,),
        in_specs=[blk(U_CQ), blk(U_CK), blk(U_CV), blk(U_CZ), cache_spec, cache_spec,
                  pl.BlockSpec((4, DIFF_DH), lambda b: (0, 0)),
                  pl.BlockSpec((1, BR_WIDTH), lambda b: (0, 0))],
        out_specs=[pl.BlockSpec((t, BR_WIDTH), lambda b: (b, 0)), state_spec, state_spec],
        compiler_params=_params("arbitrary"),
        name="diff_attn_sample",
    )(u, u, u, u, cache_k, cache_v, lamp, gain.reshape(1, BR_WIDTH))


CONV_PAD = 8


def _mlstm_body(qk_ref, v_ref, og_ref, z_ref, gc_ref, gr_ref, cw_ref, cb_ref, gbr_ref, gbc_ref, gn_ref,
                c0_ref, n0_ref, m0_ref, conv0_ref,
                out_ref, c_out_ref, n_out_ref, m_out_ref,
                cbuf, c_scr, n_scr, m_scr):
    ln = qk_ref.shape[0]
    c = pl.program_id(1)
    lo = CONV_PAD - (CONV_W - 1)

    @pl.when(c == 0)
    def _():
        cbuf[lo:CONV_PAD, :] = conv0_ref[...]
        c_scr[...] = c0_ref[...]
        n_scr[...] = n0_ref[...]
        m_scr[...] = m0_ref[...]

    cbuf[CONV_PAD:CONV_PAD + ln, :] = qk_ref[...]
    conv = cb_ref[...] + cbuf[lo:lo + ln, :] * cw_ref[0:1, :]
    for jj in range(1, CONV_W):
        conv = conv + cbuf[lo + jj:lo + jj + ln, :] * cw_ref[jj:jj + 1, :]
    tail = cbuf[lo + ln:CONV_PAD + ln, :]
    cbuf[lo:CONV_PAD, :] = tail
    act = _silu(conv)

    gcb = gc_ref[...] + gbr_ref[...]
    grb = gr_ref[...] + gbc_ref[...]
    row = lax.broadcasted_iota(jnp.int32, (ln, ln), 0)
    col = lax.broadcasted_iota(jnp.int32, (ln, ln), 1)
    causal = col <= row

    for h in range(N_HEADS):
        sl = slice(h * HEAD_DIM, (h + 1) * HEAD_DIM)
        q = act[:, h * HEAD_DIM:(h + 1) * HEAD_DIM]
        k = act[:, BR_WIDTH + h * HEAD_DIM:BR_WIDTH + (h + 1) * HEAD_DIM] * HEAD_SCALE
        v = v_ref[:, sl]
        i_col = gcb[:, h:h + 1]
        lf_col = _log_sigmoid(gcb[:, N_HEADS + h:N_HEADS + h + 1])
        i_row = grb[h:h + 1, :]
        lf_row = _log_sigmoid(grb[N_HEADS + h:N_HEADS + h + 1, :])
        b_col = jnp.sum(jnp.where(causal, lf_row, 0.0), axis=1, keepdims=True)
        b_row = jnp.sum(jnp.where(row <= col, lf_col, 0.0), axis=0, keepdims=True)
        b_last = b_col[ln - 1:ln, :]
        m_old = m_scr[h][:, 0:1]
        c_old = c_scr[h]
        n_old = n_scr[h]

        dlog = jnp.where(causal, b_col - b_row + i_row, NEG)
        inter = b_col + m_old
        mt = jnp.maximum(inter, jnp.max(dlog, axis=1, keepdims=True))
        w = jnp.exp(dlog - mt)
        wi = jnp.exp(inter - mt)
        qb = q.astype(bf16)
        vb = v.astype(bf16)
        qk = lax.dot_general(qb, k.astype(bf16), NT_DIMS, preferred_element_type=f32) * w
        num = (jnp.dot(qk.astype(bf16), vb, preferred_element_type=f32)
               + wi * jnp.dot(qb, c_old.astype(bf16), preferred_element_type=f32))
        den = jnp.sum(qk, axis=1, keepdims=True) + wi * jnp.sum(q * n_old, axis=1, keepdims=True)
        hh = num / jnp.maximum(jnp.abs(den), jnp.exp(-mt))

        m_new = mt[ln - 1:ln, :]
        ws = jnp.exp(b_last - b_col + i_col - m_new)
        dec = jnp.exp(b_last + m_old - m_new)
        kw = k * ws
        c_scr[h] = dec * c_old + lax.dot_general(kw.astype(bf16), vb, TN_DIMS, preferred_element_type=f32)
        n_scr[h] = dec * n_old + jnp.sum(kw, axis=0, keepdims=True)
        m_scr[h] = jnp.broadcast_to(m_new, (1, HEAD_DIM))

        y = _head_norm(_sigmoid(og_ref[:, sl]) * hh, gn_ref[:, sl])
        out_ref[:, sl] = (y * _silu(z_ref[:, sl])).astype(bf16)

    @pl.when(c == pl.num_programs(1) - 1)
    def _():
        c_out_ref[...] = c_scr[...]
        n_out_ref[...] = n_scr[...]
        m_out_ref[...] = m_scr[...]


def _mlstm(u, gates_r, conv_w, conv_b, b_ig, b_fg, gn, c0, n0, m0, conv0, row0, ng, nc, ln):
    rb0 = row0 // ln
    gbias = jnp.concatenate([b_ig, b_fg]).astype(f32)
    gbr = jnp.zeros((1, GATE_PAD), f32).at[0, :2 * N_HEADS].set(gbias)
    gbc = gbias.reshape(2 * N_HEADS, 1)

    def blk(col, width):
        return pl.BlockSpec((ln, width), lambda g, c: (rb0 + g * nc + c, col // width))

    def const(shape):
        return pl.BlockSpec(shape, lambda g, c: (0,) * len(shape))

    def per_seq(shape):
        return pl.BlockSpec((None,) + shape, lambda g, c: (g,) + (0,) * len(shape))

    state_shapes = [jax.ShapeDtypeStruct((ng, N_HEADS, HEAD_DIM, HEAD_DIM), f32),
                    jax.ShapeDtypeStruct((ng, N_HEADS, 1, HEAD_DIM), f32),
                    jax.ShapeDtypeStruct((ng, N_HEADS, 1, HEAD_DIM), f32)]
    state_specs = [per_seq((N_HEADS, HEAD_DIM, HEAD_DIM)),
                   per_seq((N_HEADS, 1, HEAD_DIM)),
                   per_seq((N_HEADS, 1, HEAD_DIM))]
    out, c_new, n_new, m_new = pl.pallas_call(
        _mlstm_body,
        out_shape=[jax.ShapeDtypeStruct((ng * nc * ln, BR_WIDTH), bf16)] + state_shapes,
        grid=(ng, nc),
        in_specs=[blk(U_BQK, B_QK), blk(U_BV, BR_WIDTH), blk(U_BO, BR_WIDTH), blk(U_BZ, BR_WIDTH),
                  blk(U_GATES, GATE_PAD),
                  pl.BlockSpec((None, 2 * N_HEADS, ln), lambda g, c: (g * nc + c, 0, 0)),
                  const((CONV_W, B_QK)), const((1, B_QK)), const((1, GATE_PAD)), const((2 * N_HEADS, 1)),
                  const((1, BR_WIDTH))] + state_specs + [per_seq((CONV_W - 1, B_QK))],
        out_specs=[pl.BlockSpec((ln, BR_WIDTH), lambda g, c: (g * nc + c, 0))] + state_specs,
        scratch_shapes=[pltpu.VMEM((CONV_PAD + ln, B_QK), f32),
                        pltpu.VMEM((N_HEADS, HEAD_DIM, HEAD_DIM), f32),
                        pltpu.VMEM((N_HEADS, 1, HEAD_DIM), f32),
                        pltpu.VMEM((N_HEADS, 1, HEAD_DIM), f32)],
        compiler_params=_params("parallel", "arbitrary"),
        name="mlstm",
    )(u, u, u, u, u, gates_r, conv_w, conv_b.reshape(1, B_QK), gbr, gbc, gn.reshape(1, BR_WIDTH),
      c0, n0.reshape(ng, N_HEADS, 1, HEAD_DIM),
      jnp.broadcast_to(m0[:, :, None, None], (ng, N_HEADS, 1, HEAD_DIM)), conv0)
    return out, c_new, n_new[:, :, 0, :], m_new[:, :, 0, 0]


def _ret_body(q_ref, k_ref, v_ref, z_ref, cc_ref, ss_ref, gn_ref, s0_ref, out_ref, s_out_ref, s_scr):
    ln = q_ref.shape[0]
    c = pl.program_id(1)

    @pl.when(c == 0)
    def _():
        s_scr[...] = s0_ref[...]

    row = lax.broadcasted_iota(jnp.int32, (ln, ln), 0)
    col = lax.broadcasted_iota(jnp.int32, (ln, ln), 1)
    rel = (row - col).astype(f32)
    tpos = lax.broadcasted_iota(jnp.int32, (ln, 1), 0).astype(f32)
    cc = cc_ref[...]
    ss = ss_ref[...]
    for h in range(N_HEADS):
        sl = slice(h * HEAD_DIM, (h + 1) * HEAD_DIM)
        lg = math.log(1.0 - 2.0 ** (-RET_GAMMA_EXP0 - h))
        q = q_ref[:, sl]
        k = k_ref[:, sl]
        qr = q * cc + pltpu.roll(q, DIFF_DH, 1) * ss
        kr = (k * cc + pltpu.roll(k, DIFF_DH, 1) * ss) * HEAD_SCALE
        vb = v_ref[:, sl].astype(bf16)
        decay = jnp.where(rel >= 0.0, jnp.exp(jnp.maximum(rel, 0.0) * lg), 0.0)
        att = lax.dot_general(qr.astype(bf16), kr.astype(bf16), NT_DIMS, preferred_element_type=f32) * decay
        q_dec = qr * jnp.exp((tpos + 1.0) * lg)
        s_old = s_scr[h]
        o = (jnp.dot(att.astype(bf16), vb, preferred_element_type=f32)
             + jnp.dot(q_dec.astype(bf16), s_old.astype(bf16), preferred_element_type=f32))
        k_dec = kr * jnp.exp((ln - 1.0 - tpos) * lg)
        s_scr[h] = math.exp(ln * lg) * s_old + lax.dot_general(k_dec.astype(bf16), vb, TN_DIMS,
                                                               preferred_element_type=f32)
        out_ref[:, sl] = (_head_norm(o, gn_ref[:, sl]) * _silu(z_ref[:, sl])).astype(bf16)

    @pl.when(c == pl.num_programs(1) - 1)
    def _():
        s_out_ref[...] = s_scr[...]


def _rope_tables(pos):
    inv = ROPE_BASE ** (-jnp.arange(0, HEAD_DIM, 2, dtype=f32) / HEAD_DIM)
    ang = pos.astype(f32)[:, None] * inv[None, :]
    cos, sin = jnp.cos(ang), jnp.sin(ang)
    return jnp.concatenate([cos, cos], axis=-1), jnp.concatenate([-sin, sin], axis=-1)


def _retention(u, pos, gn, s0, row0, ng, nc, ln):
    rb0 = row0 // ln
    cc, ss = _rope_tables(pos)

    def blk(col):
        return pl.BlockSpec((ln, BR_WIDTH), lambda g, c: (rb0 + g * nc + c, col // BR_WIDTH))

    rope_spec = pl.BlockSpec((ln, HEAD_DIM), lambda g, c: (c, 0))
    state_spec = pl.BlockSpec((None, N_HEADS, HEAD_DIM, HEAD_DIM), lambda g, c: (g, 0, 0, 0))
    return pl.pallas_call(
        _ret_body,
        out_shape=[jax.ShapeDtypeStruct((ng * nc * ln, BR_WIDTH), bf16),
                   jax.ShapeDtypeStruct((ng, N_HEADS, HEAD_DIM, HEAD_DIM), f32)],
        grid=(ng, nc),
        in_specs=[blk(U_DQ), blk(U_DK), blk(U_DV), blk(U_DZ), rope_spec, rope_spec,
                  pl.BlockSpec((1, BR_WIDTH), lambda g, c: (0, 0)), state_spec],
        out_specs=[pl.BlockSpec((ln, BR_WIDTH), lambda g, c: (g * nc + c, 0)), state_spec],
        scratch_shapes=[pltpu.VMEM((N_HEADS, HEAD_DIM, HEAD_DIM), f32)],
        compiler_params=_params("parallel", "arbitrary"),
        name="retention",
    )(u, u, u, u, cc, ss, gn.reshape(1, BR_WIDTH), s0)


def _merge_body(x_ref, g_ref, oa_ref, ob_ref, oc_ref, od_ref, om_ref,
                wg0_ref, wg1_ref, wg2_ref, wg3_ref, wg4_ref, wb_ref, wo_ref, fg_ref,
                y_ref, h_scr, *, rows, final):
    n = pl.program_id(1)

    @pl.when(n == 0)
    def _():
        _rmsnorm_rows(x_ref, g_ref, h_scr, rows)
        y_ref[...] = jnp.zeros(y_ref.shape, f32)

    h = h_scr[...]
    merged = None
    for i, (o_ref, wg_ref) in enumerate(zip((oa_ref, ob_ref, oc_ref, od_ref, om_ref),
                                            (wg0_ref, wg1_ref, wg2_ref, wg3_ref, wg4_ref))):
        gate = _sigmoid(jnp.dot(h, wg_ref[...], preferred_element_type=f32))
        term = gate * jnp.dot(o_ref[...], wb_ref[i], preferred_element_type=f32)
        merged = term if merged is None else merged + term
    y_ref[...] += jnp.dot(merged.astype(bf16), wo_ref[...], preferred_element_type=f32)

    @pl.when(n == pl.num_programs(1) - 1)
    def _():
        tm = x_ref.shape[0]
        for r in range(0, tm, rows):
            y = x_ref[r:r + rows, :] + y_ref[r:r + rows, :]
            if final:
                y = (y * lax.rsqrt(jnp.mean(y * y, axis=-1, keepdims=True) + EPS)) * fg_ref[...]
            y_ref[r:r + rows, :] = y


def _merge(x, g, outs, wg, wb, wo, fg, *, final, tm_target=704, tn=256):
    m, d = x.shape
    tm = _divisor_tile(m, tm_target, 16)
    rows = _divisor_tile(tm, 256, 8)
    nn = d // tn

    def wg_spec(i):
        return pl.BlockSpec((d, tn), lambda r, n: (0, i * nn + n))

    o_spec = pl.BlockSpec((tm, BR_WIDTH), lambda r, n: (r, 0))
    return pl.pallas_call(
        functools.partial(_merge_body, rows=rows, final=final),
        out_shape=jax.ShapeDtypeStruct((m, d), f32),
        grid=(m // tm, nn),
        in_specs=[pl.BlockSpec((tm, d), lambda r, n: (r, 0)),
                  pl.BlockSpec((1, d), lambda r, n: (0, 0))]
                 + [o_spec] * N_BRANCH
                 + [wg_spec(i) for i in range(N_BRANCH)]
                 + [pl.BlockSpec((N_BRANCH, BR_WIDTH, tn), lambda r, n: (0, 0, n)),
                    pl.BlockSpec((tn, d), lambda r, n: (n, 0)),
                    pl.BlockSpec((1, d), lambda r, n: (0, 0))],
        out_specs=pl.BlockSpec((tm, d), lambda r, n: (r, 0)),
        scratch_shapes=[pltpu.VMEM((tm, d), bf16)],
        compiler_params=_params("parallel", "arbitrary"),
        name="gated_merge",
    )(x, g.reshape(1, d), *outs, wg, wg, wg, wg, wg, wb, wo, fg.reshape(1, d))


LANES = 128
W_SHIFT = W_BZ - W_BI
REPACK_ROWS = 512


def _repack_body(a_ref, b_ref, c_ref, o_ref, *, src_col0, gate_block):
    tc = o_ref.shape[1]
    j = pl.program_id(1)
    a = a_ref[...]
    shifted = jnp.concatenate([a, b_ref[...]], axis=1)[:, W_SHIFT:W_SHIFT + tc]
    col = src_col0 + j * tc + lax.broadcasted_iota(jnp.int32, (1, tc), 1)
    o_ref[...] = jnp.where(col < W_BI, a, shifted).astype(bf16)
    if gate_block:
        @pl.when(j == pl.num_programs(1) - 1)
        def _():
            lane = lax.broadcasted_iota(jnp.int32, (1, LANES), 1)
            o_ref[:, tc - LANES:] = jnp.where(lane < W_SHIFT, c_ref[...], 0.0).astype(bf16)


def _repack(w_in, layer, *, src_col0, width, tc, gate_block, name):
    d = w_in.shape[1]
    assert src_col0 % tc == 0 and width % tc == 0 and d % REPACK_ROWS == 0 and tc % LANES == 0
    j0 = src_col0 // tc
    per = tc // LANES
    return pl.pallas_call(
        functools.partial(_repack_body, src_col0=src_col0, gate_block=gate_block),
        out_shape=jax.ShapeDtypeStruct((d, width), bf16),
        grid=(d // REPACK_ROWS, width // tc),
        in_specs=[pl.BlockSpec((None, REPACK_ROWS, tc), lambda r, j: (layer, r, j0 + j)),
                  pl.BlockSpec((None, REPACK_ROWS, LANES), lambda r, j: (layer, r, (j0 + j + 1) * per)),
                  pl.BlockSpec((None, REPACK_ROWS, LANES), lambda r, j: (layer, r, W_BI // LANES))],
        out_specs=pl.BlockSpec((REPACK_ROWS, tc), lambda r, j: (r, j)),
        compiler_params=_params("parallel", "arbitrary"),
        name=name,
    )(w_in, w_in, w_in)


def _repack_w_in(w_in, layer):
    wu = _repack(w_in, layer, src_col0=0, width=U_WIDTH, tc=896, gate_block=True, name="repack_u")
    wg = _repack(w_in, layer, src_col0=U_GATES, width=N_BRANCH * D_MODEL, tc=512, gate_block=False,
                 name="repack_g")
    return wu, wg


def _gates_rowform(u, row0, nrows, ln):
    g = u[row0:row0 + nrows, U_GATES:U_GATES + 2 * N_HEADS]
    return g.reshape(nrows // ln, ln, 2 * N_HEADS).transpose(0, 2, 1)


def kernel(x_prompt, x_sample, mem_prompt, cache_a_k, cache_a_v, cache_c_k, cache_c_v, cache_mem_k, cache_mem_v, state_b_C, state_b_n, state_b_m, state_b_conv, state_d_S, norm_g, w_in, conv_w, conv_b, b_ig, b_fg, rel_bias, lam_q1, lam_k1, lam_q2, lam_k2, gn_b, subln_c, gn_d, mem_norm_g, w_mk, w_mv, w_branch, w_out, final_g):
    batch, seq, d = x_prompt.shape
    nb, t, _ = x_sample.shape
    depth = w_in.shape[0]
    n_mem = mem_prompt.shape[1]
    past = cache_c_k.shape[2]
    mp = batch * seq
    assert d == D_MODEL and seq % C_BLOCK == 0 and seq % A_QBLOCK == 0 and seq % SCAN_CHUNK == 0
    assert mp % t == 0

    x = jnp.concatenate([x_prompt.reshape(mp, d), x_sample.reshape(nb * t, d)], axis=0)
    mem = mem_prompt.reshape(batch * n_mem, d)
    ca_k, ca_v = _cache_rows(cache_a_k), _cache_rows(cache_a_v)
    cc_k, cc_v = _cache_rows(cache_c_k), _cache_rows(cache_c_v)
    cm_k, cm_v = _cache_rows(cache_mem_k), _cache_rows(cache_mem_v)
    pos_p = jnp.arange(seq)
    pos_s = past + jnp.arange(t)
    a_rows = min(A_WINDOW, seq)
    mem_tq = _divisor_tile(seq, 512, 16)

    sp, ss = [], []
    for l in range(depth):
        lam_init = 0.8 - 0.6 * math.exp(-0.3 * l)
        lamp = jnp.stack([lam_q1[l], lam_k1[l], lam_q2[l], lam_k2[l]]).astype(f32)
        wu, wg = _repack_w_in(w_in, l)
        u = _norm_matmul(x, norm_g[l], wu, tm_target=1056, tn=896, name="in_proj")
        mkv = _norm_matmul(mem, mem_norm_g[l], jnp.concatenate([w_mk[l], w_mv[l]], axis=1).astype(bf16),
                           tm_target=512, tn=512, name="mem_proj")

        oa_p, ak_p, av_p = _band_prompt(u, rel_bias[l], batch, seq, a_rows)
        zeros_c = jnp.zeros((batch, N_HEADS, HEAD_DIM, HEAD_DIM), f32)
        ob_p, c_p, n_p, m_p = _mlstm(
            u, _gates_rowform(u, 0, mp, SCAN_CHUNK), conv_w[l], conv_b[l], b_ig[l], b_fg[l], gn_b[l],
            zeros_c, jnp.zeros((batch, N_HEADS, HEAD_DIM), f32), jnp.zeros((batch, N_HEADS), f32),
            jnp.zeros((batch, CONV_W - 1, B_QK), f32), 0, batch, seq // SCAN_CHUNK, SCAN_CHUNK)
        oc_p, ck_p, cv_p = _diff_prompt(u, lamp, subln_c[l], lam_init, batch, seq)
        od_p, s_p = _retention(u, pos_p, gn_d[l], zeros_c, 0, batch, seq // SCAN_CHUNK, SCAN_CHUNK)
        mem_spec_k = pl.BlockSpec((n_mem, BR_WIDTH), lambda b, i: (b, 0))
        mem_spec_v = pl.BlockSpec((n_mem, BR_WIDTH), lambda b, i: (b, 1))
        om_p = _mem_attn(u, 0, batch, seq, mem_tq, mkv, mkv, mem_spec_k, mem_spec_v, False, "mem_attn_prompt")

        oa_s, ak_s, av_s = _band_sample(u, ca_k, ca_v, l, rel_bias[l], mp, nb, t)
        ob_s, c_s, n_s, m_s = _mlstm(
            u, _gates_rowform(u, mp, nb * t, t), conv_w[l], conv_b[l], b_ig[l], b_fg[l], gn_b[l],
            state_b_C[l].astype(f32), state_b_n[l].astype(f32), state_b_m[l].astype(f32),
            state_b_conv[l].astype(f32), mp, nb, 1, t)
        oc_s, ck_s, cv_s = _diff_sample(u, cc_k, cc_v, l, lamp, subln_c[l], lam_init, mp, nb, t)
        od_s, s_s = _retention(u, pos_s, gn_d[l], state_d_S[l].astype(f32), mp, nb, 1, t)
        cache_spec = pl.BlockSpec((None, None, n_mem * N_HEADS, HEAD_DIM), lambda b, i: (l, b, 0, 0))
        om_s = _mem_attn(u, mp, nb, t, t, cm_k, cm_v, cache_spec, cache_spec, True, "mem_attn_sample")

        outs = [jnp.concatenate([p, s], axis=0) for p, s in
                ((oa_p, oa_s), (ob_p, ob_s), (oc_p, oc_s), (od_p, od_s), (om_p, om_s))]
        x = _merge(x, norm_g[l], outs, wg, w_branch[l].astype(bf16), w_out[l].astype(bf16), final_g,
                   final=(l == depth - 1))

        def seg_p(col, width, first_row):
            return u[:mp, col:col + width].reshape(batch, seq, width)[:, first_row:]

        def seg_s(col, width, first_row):
            return u[mp:, col:col + width].reshape(nb, t, width)[:, first_row:]

        def hd(a):
            return a.reshape(a.shape[:2] + (N_HEADS, HEAD_DIM))

        def rows_hd(a):
            return a.reshape(a.shape[0], a.shape[1] // N_HEADS, N_HEADS, HEAD_DIM)

        sp.append((rows_hd(ak_p), rows_hd(av_p), rows_hd(ck_p), rows_hd(cv_p),
                   hd(mkv[:, :BR_WIDTH].reshape(batch, n_mem, BR_WIDTH)),
                   hd(mkv[:, BR_WIDTH:].reshape(batch, n_mem, BR_WIDTH)),
                   c_p, n_p, m_p, seg_p(U_BQK, B_QK, seq - (CONV_W - 1)), s_p))
        ss.append((rows_hd(ak_s), rows_hd(av_s), rows_hd(ck_s), rows_hd(cv_s),
                   c_s, n_s, m_s, seg_s(U_BQK, B_QK, t - (CONV_W - 1)), s_s))

    y_prompt = x[:mp].reshape(batch, seq, d)
    y_sample = x[mp:].reshape(nb, t, d)
    p_states = tuple(jnp.stack([st[i] for st in sp]) for i in range(11))
    s_states = tuple(jnp.stack([st[i] for st in ss]) for i in range(9))
    return (y_prompt, y_sample) + p_states + s_states
```

```python
import functools
import math

import jax
import jax.numpy as jnp
import numpy as np
from jax import lax
from jax.experimental import pallas as pl
from jax.experimental.pallas import tpu as pltpu

f32 = jnp.float32
bf16 = jnp.bfloat16

D_MODEL = 2048
N_HEADS = 4
HEAD_DIM = 128
BR_WIDTH = N_HEADS * HEAD_DIM
N_BRANCH = 5
CHUNK = 64
A_PREV_CHUNKS = 8
A_WINDOW = A_PREV_CHUNKS * CHUNK
A_BAND = (A_PREV_CHUNKS + 1) * CHUNK
A_REL_CLIP = 128
CONV_W = 4
B_QK = 2 * BR_WIDTH
DIFF_DH = HEAD_DIM // 2
RET_GAMMA_EXP0 = 5.0
ROPE_BASE = 10000.0
EPS = 1e-6
NEG = -1e30
HEAD_SCALE = HEAD_DIM ** -0.5
DIFF_SCALE = DIFF_DH ** -0.5

U_AQ, U_AK, U_AV, U_AZ = 0, 512, 1024, 1536
U_BQK, U_BV, U_BO, U_BZ = 2048, 3072, 3584, 4096
U_CQ, U_CK, U_CV, U_CZ = 4608, 5120, 5632, 6144
U_DQ, U_DK, U_DV, U_DZ = 6656, 7168, 7680, 8192
U_MQ, U_MZ = 8704, 9216
U_GATES = 9728
GATE_PAD = 128
U_WIDTH = U_GATES + GATE_PAD
W_BI = 4096
W_BZ = 4104
W_GATE = 9736

VMEM_LIMIT_BYTES = 56 * 1024 * 1024
A_QBLOCK = 4 * CHUNK
C_BLOCK = 512
SCAN_CHUNK = 256
NT_DIMS = (((1,), (1,)), ((), ()))
TN_DIMS = (((0,), (0,)), ((), ()))


def _divisor_tile(n, target, multiple):
    best = None
    for t in range(multiple, min(n, target) + 1, multiple):
        if n % t == 0:
            best = t
    assert best is not None, (n, target, multiple)
    return best


def _params(*sem):
    return pltpu.CompilerParams(dimension_semantics=sem, vmem_limit_bytes=VMEM_LIMIT_BYTES)


def _sigmoid(x):
    return 1.0 / (1.0 + jnp.exp(-x))


def _silu(x):
    return x * _sigmoid(x)


def _log_sigmoid(x):
    return jnp.minimum(x, 0.0) - jnp.log1p(jnp.exp(-jnp.abs(x)))


def _head_norm(y, gain):
    return y * lax.rsqrt(jnp.mean(y * y, axis=-1, keepdims=True) + EPS) * gain


def _rmsnorm_rows(x_ref, g_ref, h_ref, rows):
    tm = x_ref.shape[0]
    for r in range(0, tm, rows):
        x = x_ref[r:r + rows, :]
        ms = jnp.mean(x * x, axis=-1, keepdims=True)
        h_ref[r:r + rows, :] = ((x * lax.rsqrt(ms + EPS)) * g_ref[...]).astype(bf16)


def _norm_matmul_body(x_ref, g_ref, w_ref, o_ref, h_ref, *, rows, w_rows_are_outputs):
    @pl.when(pl.program_id(1) == 0)
    def _():
        _rmsnorm_rows(x_ref, g_ref, h_ref, rows)

    if w_rows_are_outputs:
        o = lax.dot_general(h_ref[...], w_ref[...], NT_DIMS, preferred_element_type=f32)
    else:
        o = jnp.dot(h_ref[...], w_ref[...], preferred_element_type=f32)
    o_ref[...] = o.astype(o_ref.dtype)


def _norm_matmul(x, g, w, *, tm_target, tn, w_rows_are_outputs, name):
    m, d = x.shape
    n = w.shape[0] if w_rows_are_outputs else w.shape[1]
    tm = _divisor_tile(m, tm_target, 16)
    rows = _divisor_tile(tm, 256, 8)
    assert n % tn == 0
    w_spec = (pl.BlockSpec((tn, d), lambda i, j: (j, 0)) if w_rows_are_outputs
              else pl.BlockSpec((d, tn), lambda i, j: (0, j)))
    return pl.pallas_call(
        functools.partial(_norm_matmul_body, rows=rows, w_rows_are_outputs=w_rows_are_outputs),
        out_shape=jax.ShapeDtypeStruct((m, n), f32),
        grid=(m // tm, n // tn),
        in_specs=[pl.BlockSpec((tm, d), lambda i, j: (i, 0)),
                  pl.BlockSpec((1, d), lambda i, j: (0, 0)),
                  w_spec],
        out_specs=pl.BlockSpec((tm, tn), lambda i, j: (i, j)),
        scratch_shapes=[pltpu.VMEM((tm, d), bf16)],
        compiler_params=_params("parallel", "arbitrary"),
        name=name,
    )(x, g.reshape(1, d), w)


def _store_head_rows(dst_ref, src_ref):
    rows = src_ref.shape[0]
    for h in range(N_HEADS):
        dst_ref[pl.ds(h, rows, stride=N_HEADS), :] = src_ref[:, h * HEAD_DIM:(h + 1) * HEAD_DIM]


def _band_prompt_body(q_ref, k0_ref, k1_ref, k2_ref, v0_ref, v1_ref, v2_ref, z_ref, bias_ref,
                      o_ref, ks_ref, vs_ref, *, n_tail):
    tq = q_ref.shape[0]
    i = pl.program_id(1)

    @pl.when(i >= pl.num_programs(1) - n_tail)
    def _():
        _store_head_rows(ks_ref, k2_ref)
        _store_head_rows(vs_ref, v2_ref)

    w_idx = lax.broadcasted_iota(jnp.int32, (tq, 3 * tq), 1)
    valid = (w_idx + (i - 2) * tq) >= 0
    for h in range(N_HEADS):
        sl = slice(h * HEAD_DIM, (h + 1) * HEAD_DIM)
        q = q_ref[:, sl].astype(bf16)
        kw = jnp.concatenate([k0_ref[:, sl], k1_ref[:, sl], k2_ref[:, sl]], axis=0).astype(bf16)
        vw = jnp.concatenate([v0_ref[:, sl], v1_ref[:, sl], v2_ref[:, sl]], axis=0).astype(bf16)
        s = lax.dot_general(q, kw, NT_DIMS, preferred_element_type=f32) * HEAD_SCALE + bias_ref[h]
        s = jnp.where(valid, s, NEG)
        p = jnp.exp(s - jnp.max(s, axis=-1, keepdims=True))
        l = jnp.sum(p, axis=-1, keepdims=True)
        o = jnp.dot(p.astype(bf16), vw, preferred_element_type=f32) / l
        o_ref[:, sl] = (o * _silu(z_ref[:, sl])).astype(bf16)


def _rel_bias_toeplitz(rel_bias, rows, cols, rel0):
    p = rows + cols
    j = np.arange(p)
    d = np.where(j < cols, j, j - p)
    idx = np.clip(rel0 - d, -A_REL_CLIP, A_REL_CLIP) + A_REL_CLIP
    v = rel_bias.astype(f32)[:, idx]
    flat = jnp.tile(v, (1, rows))[:, :rows * (p - 1)]
    return flat.reshape(rel_bias.shape[0], rows, p - 1)[:, :, :cols]


def _band_bias_prompt(rel_bias):
    r = np.arange(A_QBLOCK)[:, None]
    w = np.arange(3 * A_QBLOCK)[None, :]
    kj = w - CHUNK * (r // CHUNK)
    inside = (kj >= 0) & (kj < A_BAND)
    table = _rel_bias_toeplitz(rel_bias, A_QBLOCK, 3 * A_QBLOCK, 2 * A_QBLOCK)
    return jnp.where(jnp.asarray(inside)[None], table, NEG)


def _band_prompt(u, rel_bias, batch, seq, a_rows):
    tq = A_QBLOCK
    nqb = seq // tq
    assert a_rows % tq == 0
    n_tail = a_rows // tq
    bias = _band_bias_prompt(rel_bias)

    def blk(col, back):
        return pl.BlockSpec((tq, BR_WIDTH),
                            lambda b, i: (b * nqb + jnp.maximum(i - back, 0), col // BR_WIDTH))

    state_shape = jax.ShapeDtypeStruct((batch, a_rows * N_HEADS, HEAD_DIM), f32)
    state_spec = pl.BlockSpec((None, tq * N_HEADS, HEAD_DIM),
                              lambda b, i: (b, jnp.maximum(i - (nqb - n_tail), 0), 0))
    return pl.pallas_call(
        functools.partial(_band_prompt_body, n_tail=n_tail),
        out_shape=[jax.ShapeDtypeStruct((batch * seq, BR_WIDTH), bf16), state_shape, state_shape],
        grid=(batch, nqb),
        in_specs=[blk(U_AQ, 0),
                  blk(U_AK, 2), blk(U_AK, 1), blk(U_AK, 0),
                  blk(U_AV, 2), blk(U_AV, 1), blk(U_AV, 0),
                  blk(U_AZ, 0),
                  pl.BlockSpec((N_HEADS, tq, 3 * tq), lambda b, i: (0, 0, 0))],
        out_specs=[pl.BlockSpec((tq, BR_WIDTH), lambda b, i: (b * nqb + i, 0)), state_spec, state_spec],
        compiler_params=_params("parallel", "arbitrary"),
        name="band_attn_prompt",
    )(u, u, u, u, u, u, u, u, bias)


def _band_sample_body(q_ref, k_ref, v_ref, z_ref, ck_ref, cv_ref, bc_ref, bn_ref, o_ref, ks_ref, vs_ref):
    _store_head_rows(ks_ref, k_ref)
    _store_head_rows(vs_ref, v_ref)
    for h in range(N_HEADS):
        sl = slice(h * HEAD_DIM, (h + 1) * HEAD_DIM)
        q = q_ref[:, sl].astype(bf16)
        s_c = lax.dot_general(q, _head_rows(ck_ref, h, True).astype(bf16), NT_DIMS,
                              preferred_element_type=f32) * HEAD_SCALE + bc_ref[h]
        s_n = lax.dot_general(q, k_ref[:, sl].astype(bf16), NT_DIMS,
                              preferred_element_type=f32) * HEAD_SCALE + bn_ref[h]
        m = jnp.maximum(jnp.max(s_c, axis=-1, keepdims=True), jnp.max(s_n, axis=-1, keepdims=True))
        p_c = jnp.exp(s_c - m)
        p_n = jnp.exp(s_n - m)
        l = jnp.sum(p_c, axis=-1, keepdims=True) + jnp.sum(p_n, axis=-1, keepdims=True)
        o = (jnp.dot(p_c.astype(bf16), _head_rows(cv_ref, h, True).astype(bf16), preferred_element_type=f32)
             + jnp.dot(p_n.astype(bf16), v_ref[:, sl].astype(bf16), preferred_element_type=f32)) / l
        o_ref[:, sl] = (o * _silu(z_ref[:, sl])).astype(bf16)


def _head_rows(ref, h, interleaved):
    if interleaved:
        return ref[pl.ds(h, ref.shape[0] // N_HEADS, stride=N_HEADS), :]
    return ref[:, h * HEAD_DIM:(h + 1) * HEAD_DIM]


def _cache_rows(cache):
    d0, d1, rows = cache.shape[:3]
    return cache.reshape(d0, d1, rows * N_HEADS, HEAD_DIM)


def _mem_body(q_ref, z_ref, k_ref, v_ref, o_ref, *, interleaved):
    for h in range(N_HEADS):
        sl = slice(h * HEAD_DIM, (h + 1) * HEAD_DIM)
        q = q_ref[:, sl].astype(bf16)
        k = _head_rows(k_ref, h, interleaved).astype(bf16)
        v = _head_rows(v_ref, h, interleaved).astype(bf16)
        s = lax.dot_general(q, k, NT_DIMS, preferred_element_type=f32) * HEAD_SCALE
        p = jnp.exp(s - jnp.max(s, axis=-1, keepdims=True))
        l = jnp.sum(p, axis=-1, keepdims=True)
        o = jnp.dot(p.astype(bf16), v, preferred_element_type=f32) / l
        o_ref[:, sl] = (o * _silu(z_ref[:, sl])).astype(bf16)


def _mem_attn(u, row0, nb, t, tq, k_arr, v_arr, k_spec, v_spec, interleaved, name):
    nq = t // tq
    rb0 = row0 // tq

    def blk(col):
        return pl.BlockSpec((tq, BR_WIDTH), lambda b, i: (rb0 + b * nq + i, col // BR_WIDTH))

    return pl.pallas_call(
        functools.partial(_mem_body, interleaved=interleaved),
        out_shape=jax.ShapeDtypeStruct((nb * t, BR_WIDTH), bf16),
        grid=(nb, nq),
        in_specs=[blk(U_MQ), blk(U_MZ), k_spec, v_spec],
        out_specs=pl.BlockSpec((tq, BR_WIDTH), lambda b, i: (b * nq + i, 0)),
        compiler_params=_params("parallel", "arbitrary"),
        name=name,
    )(u, u, k_arr, v_arr)


def _diff_lambda(lamp_ref, lam_init):
    lp = lamp_ref[...]
    a = jnp.sum(lp[0:1] * lp[1:2], axis=-1, keepdims=True)
    b = jnp.sum(lp[2:3] * lp[3:4], axis=-1, keepdims=True)
    return jnp.exp(a) - jnp.exp(b) + lam_init


def _diff_epilogue(o0, o1, lam, gain, z, lam_init):
    y = _head_norm(o0 - lam * o1, gain) * (1.0 - lam_init)
    return (y * _silu(z)).astype(bf16)


def _half_masks():
    lane = lax.broadcasted_iota(jnp.int32, (1, HEAD_DIM), 1)
    return lane < DIFF_DH, lane >= DIFF_DH


def _diff_prompt_body(q_ref, k_ref, v_ref, z_ref, lamp_ref, gain_ref, o_ref, ks_ref, vs_ref,
                      m_scr, l_scr, acc_scr, *, lam_init):
    tq, tk = q_ref.shape[0], k_ref.shape[0]
    i = pl.program_id(1)
    j = pl.program_id(2)

    @pl.when(j == 0)
    def _():
        m_scr[...] = jnp.full(m_scr.shape, NEG, f32)
        l_scr[...] = jnp.zeros(l_scr.shape, f32)
        acc_scr[...] = jnp.zeros(acc_scr.shape, f32)

    lo, hi = _half_masks()

    def step(diagonal):
        if diagonal:
            kc = lax.broadcasted_iota(jnp.int32, (tk, 2 * tq), 0) // CHUNK
            qpos = lax.broadcasted_iota(jnp.int32, (tk, 2 * tq), 1)
            qc = jnp.where(qpos >= tq, qpos - tq, qpos) // CHUNK
            mask = kc <= qc
        for h in range(N_HEADS):
            sl = slice(h * HEAD_DIM, (h + 1) * HEAD_DIM)
            q = q_ref[:, sl] * DIFF_SCALE
            qq = jnp.concatenate([jnp.where(lo, q, 0.0), jnp.where(hi, q, 0.0)], axis=0).astype(bf16)
            k = k_ref[:, sl].astype(bf16)
            vt = v_ref[:, sl].T.astype(bf16)
            s = lax.dot_general(k, qq, NT_DIMS, preferred_element_type=f32)
            if diagonal:
                s = jnp.where(mask, s, NEG)
            m_old = m_scr[h]
            m_new = jnp.maximum(m_old, jnp.max(s, axis=0, keepdims=True))
            p = jnp.exp(s - m_new)
            alpha = jnp.exp(m_old - m_new)
            l_scr[h] = alpha * l_scr[h] + jnp.sum(p, axis=0, keepdims=True)
            acc_scr[h] = alpha * acc_scr[h] + jnp.dot(vt, p.astype(bf16), preferred_element_type=f32)
            m_scr[h] = m_new

    @pl.when(j < i)
    def _():
        step(False)

    @pl.when(j == i)
    def _():
        step(True)
        _store_head_rows(ks_ref, k_ref)
        _store_head_rows(vs_ref, v_ref)
        lam = _diff_lambda(lamp_ref, lam_init)
        for h in range(N_HEADS):
            sl = slice(h * HEAD_DIM, (h + 1) * HEAD_DIM)
            on = acc_scr[h] / l_scr[h]
            o0 = on[:, :tq].T
            o1 = on[:, tq:].T
            o_ref[:, sl] = _diff_epilogue(o0, o1, lam, gain_ref[:, sl], z_ref[:, sl], lam_init)


def _diff_prompt(u, lamp, gain, lam_init, batch, seq):
    t = _divisor_tile(seq, C_BLOCK, 2 * CHUNK)
    nt = seq // t

    def qblk(col):
        return pl.BlockSpec((t, BR_WIDTH), lambda b, i, j: (b * nt + i, col // BR_WIDTH))

    def kblk(col):
        return pl.BlockSpec((t, BR_WIDTH), lambda b, i, j: (b * nt + jnp.minimum(j, i), col // BR_WIDTH))

    state_shape = jax.ShapeDtypeStruct((batch, seq * N_HEADS, HEAD_DIM), f32)
    state_spec = pl.BlockSpec((None, t * N_HEADS, HEAD_DIM), lambda b, i, j: (b, i, 0))
    return pl.pallas_call(
        functools.partial(_diff_prompt_body, lam_init=lam_init),
        out_shape=[jax.ShapeDtypeStruct((batch * seq, BR_WIDTH), bf16), state_shape, state_shape],
        grid=(batch, nt, nt),
        in_specs=[qblk(U_CQ), kblk(U_CK), kblk(U_CV), qblk(U_CZ),
                  pl.BlockSpec((4, DIFF_DH), lambda b, i, j: (0, 0)),
                  pl.BlockSpec((1, BR_WIDTH), lambda b, i, j: (0, 0))],
        out_specs=[pl.BlockSpec((t, BR_WIDTH), lambda b, i, j: (b * nt + i, 0)), state_spec, state_spec],
        scratch_shapes=[pltpu.VMEM((N_HEADS, 1, 2 * t), f32),
                        pltpu.VMEM((N_HEADS, 1, 2 * t), f32),
                        pltpu.VMEM((N_HEADS, HEAD_DIM, 2 * t), f32)],
        compiler_params=_params("parallel", "parallel", "arbitrary"),
        name="diff_attn_prompt",
    )(u, u, u, u, lamp, gain.reshape(1, BR_WIDTH))


def _diff_sample_body(q_ref, k_ref, v_ref, z_ref, ck_ref, cv_ref, lamp_ref, gain_ref, o_ref, ks_ref, vs_ref,
                      *, lam_init):
    t = q_ref.shape[0]
    _store_head_rows(ks_ref, k_ref)
    _store_head_rows(vs_ref, v_ref)
    lo, hi = _half_masks()
    lam = _diff_lambda(lamp_ref, lam_init)
    for h in range(N_HEADS):
        sl = slice(h * HEAD_DIM, (h + 1) * HEAD_DIM)
        q = q_ref[:, sl] * DIFF_SCALE
        qq = jnp.concatenate([jnp.where(lo, q, 0.0), jnp.where(hi, q, 0.0)], axis=0).astype(bf16)
        k = k_ref[:, sl].astype(bf16)
        v = v_ref[:, sl].astype(bf16)
        ck = _head_rows(ck_ref, h, True).astype(bf16)
        cv = _head_rows(cv_ref, h, True).astype(bf16)
        s_c = lax.dot_general(qq, ck, NT_DIMS, preferred_element_type=f32)
        s_n = lax.dot_general(qq, k, NT_DIMS, preferred_element_type=f32)
        m = jnp.maximum(jnp.max(s_c, axis=-1, keepdims=True), jnp.max(s_n, axis=-1, keepdims=True))
        p_c = jnp.exp(s_c - m)
        p_n = jnp.exp(s_n - m)
        l = jnp.sum(p_c, axis=-1, keepdims=True) + jnp.sum(p_n, axis=-1, keepdims=True)
        o = (jnp.dot(p_c.astype(bf16), cv, preferred_element_type=f32)
             + jnp.dot(p_n.astype(bf16), v, preferred_element_type=f32)) / l
        o_ref[:, sl] = _diff_epilogue(o[:t], o[t:], lam, gain_ref[:, sl], z_ref[:, sl], lam_init)


CONV_PAD = 8


def _first_last(chunk_axis):
    if chunk_axis is None:
        def run(f):
            f()
        return run, run
    c = pl.program_id(chunk_axis)
    return pl.when(c == 0), pl.when(c == pl.num_programs(chunk_axis) - 1)


def _mlstm_body(qk_ref, v_ref, og_ref, z_ref, gc_ref, gr_ref, cw_ref, cb_ref, gbr_ref, gbc_ref, gn_ref,
                c0_ref, n0_ref, m0_ref, conv0_ref,
                out_ref, c_out_ref, n_out_ref, m_out_ref,
                cbuf, c_scr, n_scr, m_scr, *, chunk_axis=1):
    ln = qk_ref.shape[0]
    on_first, on_last = _first_last(chunk_axis)
    lo = CONV_PAD - (CONV_W - 1)

    @on_first
    def _():
        cbuf[lo:CONV_PAD, :] = conv0_ref[...]
        c_scr[...] = c0_ref[...]
        n_scr[...] = n0_ref[...]
        m_scr[...] = m0_ref[...]

    cbuf[CONV_PAD:CONV_PAD + ln, :] = qk_ref[...]
    conv = cb_ref[...] + cbuf[lo:lo + ln, :] * cw_ref[0:1, :]
    for jj in range(1, CONV_W):
        conv = conv + cbuf[lo + jj:lo + jj + ln, :] * cw_ref[jj:jj + 1, :]
    tail = cbuf[lo + ln:CONV_PAD + ln, :]
    cbuf[lo:CONV_PAD, :] = tail
    act = _silu(conv)

    gcb = gc_ref[...] + gbr_ref[...]
    grb = gr_ref[...] + gbc_ref[...]
    row = lax.broadcasted_iota(jnp.int32, (ln, ln), 0)
    col = lax.broadcasted_iota(jnp.int32, (ln, ln), 1)
    causal = col <= row

    for h in range(N_HEADS):
        sl = slice(h * HEAD_DIM, (h + 1) * HEAD_DIM)
        q = act[:, h * HEAD_DIM:(h + 1) * HEAD_DIM]
        k = act[:, BR_WIDTH + h * HEAD_DIM:BR_WIDTH + (h + 1) * HEAD_DIM] * HEAD_SCALE
        v = v_ref[:, sl]
        i_col = gcb[:, h:h + 1]
        lf_col = _log_sigmoid(gcb[:, N_HEADS + h:N_HEADS + h + 1])
        i_row = grb[h:h + 1, :]
        lf_row = _log_sigmoid(grb[N_HEADS + h:N_HEADS + h + 1, :])
        b_col = jnp.sum(jnp.where(causal, lf_row, 0.0), axis=1, keepdims=True)
        b_row = jnp.sum(jnp.where(row <= col, lf_col, 0.0), axis=0, keepdims=True)
        b_last = b_col[ln - 1:ln, :]
        m_old = m_scr[h][:, 0:1]
        c_old = c_scr[h]
        n_old = n_scr[h]

        dlog = jnp.where(causal, b_col - b_row + i_row, NEG)
        inter = b_col + m_old
        mt = jnp.maximum(inter, jnp.max(dlog, axis=1, keepdims=True))
        w = jnp.exp(dlog - mt)
        wi = jnp.exp(inter - mt)
        qb = q.astype(bf16)
        vb = v.astype(bf16)
        qk = lax.dot_general(qb, k.astype(bf16), NT_DIMS, preferred_element_type=f32) * w
        num = (jnp.dot(qk.astype(bf16), vb, preferred_element_type=f32)
               + wi * jnp.dot(qb, c_old.astype(bf16), preferred_element_type=f32))
        den = jnp.sum(qk, axis=1, keepdims=True) + wi * jnp.sum(q * n_old, axis=1, keepdims=True)
        hh = num / jnp.maximum(jnp.abs(den), jnp.exp(-mt))

        m_new = mt[ln - 1:ln, :]
        ws = jnp.exp(b_last - b_col + i_col - m_new)
        dec = jnp.exp(b_last + m_old - m_new)
        kw = k * ws
        c_scr[h] = dec * c_old + lax.dot_general(kw.astype(bf16), vb, TN_DIMS, preferred_element_type=f32)
        n_scr[h] = dec * n_old + jnp.sum(kw, axis=0, keepdims=True)
        m_scr[h] = jnp.broadcast_to(m_new, (1, HEAD_DIM))

        y = _head_norm(_sigmoid(og_ref[:, sl]) * hh, gn_ref[:, sl])
        out_ref[:, sl] = (y * _silu(z_ref[:, sl])).astype(bf16)

    @on_last
    def _():
        c_out_ref[...] = c_scr[...]
        n_out_ref[...] = n_scr[...]
        m_out_ref[...] = m_scr[...]


def _mlstm(u, gates_r, conv_w, conv_b, b_ig, b_fg, gn, c0, n0, m0, conv0, row0, ng, nc, ln):
    rb0 = row0 // ln
    gbias = jnp.concatenate([b_ig, b_fg]).astype(f32)
    gbr = jnp.zeros((1, GATE_PAD), f32).at[0, :2 * N_HEADS].set(gbias)
    gbc = gbias.reshape(2 * N_HEADS, 1)

    def blk(col, width):
        return pl.BlockSpec((ln, width), lambda g, c: (rb0 + g * nc + c, col // width))

    def const(shape):
        return pl.BlockSpec(shape, lambda g, c: (0,) * len(shape))

    def per_seq(shape):
        return pl.BlockSpec((None,) + shape, lambda g, c: (g,) + (0,) * len(shape))

    state_shapes = [jax.ShapeDtypeStruct((ng, N_HEADS, HEAD_DIM, HEAD_DIM), f32),
                    jax.ShapeDtypeStruct((ng, N_HEADS, 1, HEAD_DIM), f32),
                    jax.ShapeDtypeStruct((ng, N_HEADS, 1, HEAD_DIM), f32)]
    state_specs = [per_seq((N_HEADS, HEAD_DIM, HEAD_DIM)),
                   per_seq((N_HEADS, 1, HEAD_DIM)),
                   per_seq((N_HEADS, 1, HEAD_DIM))]
    out, c_new, n_new, m_new = pl.pallas_call(
        _mlstm_body,
        out_shape=[jax.ShapeDtypeStruct((ng * nc * ln, BR_WIDTH), bf16)] + state_shapes,
        grid=(ng, nc),
        in_specs=[blk(U_BQK, B_QK), blk(U_BV, BR_WIDTH), blk(U_BO, BR_WIDTH), blk(U_BZ, BR_WIDTH),
                  blk(U_GATES, GATE_PAD),
                  pl.BlockSpec((None, 2 * N_HEADS, ln), lambda g, c: (g * nc + c, 0, 0)),
                  const((CONV_W, B_QK)), const((1, B_QK)), const((1, GATE_PAD)), const((2 * N_HEADS, 1)),
                  const((1, BR_WIDTH))] + state_specs + [per_seq((CONV_W - 1, B_QK))],
        out_specs=[pl.BlockSpec((ln, BR_WIDTH), lambda g, c: (g * nc + c, 0))] + state_specs,
        scratch_shapes=[pltpu.VMEM((CONV_PAD + ln, B_QK), f32),
                        pltpu.VMEM((N_HEADS, HEAD_DIM, HEAD_DIM), f32),
                        pltpu.VMEM((N_HEADS, 1, HEAD_DIM), f32),
                        pltpu.VMEM((N_HEADS, 1, HEAD_DIM), f32)],
        compiler_params=_params("parallel", "arbitrary"),
        name="mlstm",
    )(u, u, u, u, u, gates_r, conv_w, conv_b.reshape(1, B_QK), gbr, gbc, gn.reshape(1, BR_WIDTH),
      c0, n0.reshape(ng, N_HEADS, 1, HEAD_DIM),
      jnp.broadcast_to(m0[:, :, None, None], (ng, N_HEADS, 1, HEAD_DIM)), conv0)
    return out, c_new, n_new[:, :, 0, :], m_new[:, :, 0, 0]


def _ret_body(q_ref, k_ref, v_ref, z_ref, cc_ref, ss_ref, gn_ref, s0_ref, out_ref, s_out_ref, s_scr,
              *, chunk_axis=1):
    ln = q_ref.shape[0]
    on_first, on_last = _first_last(chunk_axis)

    @on_first
    def _():
        s_scr[...] = s0_ref[...]

    row = lax.broadcasted_iota(jnp.int32, (ln, ln), 0)
    col = lax.broadcasted_iota(jnp.int32, (ln, ln), 1)
    rel = (row - col).astype(f32)
    tpos = lax.broadcasted_iota(jnp.int32, (ln, 1), 0).astype(f32)
    cc = cc_ref[...]
    ss = ss_ref[...]
    for h in range(N_HEADS):
        sl = slice(h * HEAD_DIM, (h + 1) * HEAD_DIM)
        lg = math.log(1.0 - 2.0 ** (-RET_GAMMA_EXP0 - h))
        q = q_ref[:, sl]
        k = k_ref[:, sl]
        qr = q * cc + pltpu.roll(q, DIFF_DH, 1) * ss
        kr = (k * cc + pltpu.roll(k, DIFF_DH, 1) * ss) * HEAD_SCALE
        vb = v_ref[:, sl].astype(bf16)
        decay = jnp.where(rel >= 0.0, jnp.exp(jnp.maximum(rel, 0.0) * lg), 0.0)
        att = lax.dot_general(qr.astype(bf16), kr.astype(bf16), NT_DIMS, preferred_element_type=f32) * decay
        q_dec = qr * jnp.exp((tpos + 1.0) * lg)
        s_old = s_scr[h]
        o = (jnp.dot(att.astype(bf16), vb, preferred_element_type=f32)
             + jnp.dot(q_dec.astype(bf16), s_old.astype(bf16), preferred_element_type=f32))
        k_dec = kr * jnp.exp((ln - 1.0 - tpos) * lg)
        s_scr[h] = math.exp(ln * lg) * s_old + lax.dot_general(k_dec.astype(bf16), vb, TN_DIMS,
                                                               preferred_element_type=f32)
        out_ref[:, sl] = (_head_norm(o, gn_ref[:, sl]) * _silu(z_ref[:, sl])).astype(bf16)

    @on_last
    def _():
        s_out_ref[...] = s_scr[...]


def _rope_tables(pos):
    inv = ROPE_BASE ** (-jnp.arange(0, HEAD_DIM, 2, dtype=f32) / HEAD_DIM)
    ang = pos.astype(f32)[:, None] * inv[None, :]
    cos, sin = jnp.cos(ang), jnp.sin(ang)
    return jnp.concatenate([cos, cos], axis=-1), jnp.concatenate([-sin, sin], axis=-1)


def _retention(u, pos, gn, s0, row0, ng, nc, ln):
    rb0 = row0 // ln
    cc, ss = _rope_tables(pos)

    def blk(col):
        return pl.BlockSpec((ln, BR_WIDTH), lambda g, c: (rb0 + g * nc + c, col // BR_WIDTH))

    rope_spec = pl.BlockSpec((ln, HEAD_DIM), lambda g, c: (c, 0))
    state_spec = pl.BlockSpec((None, N_HEADS, HEAD_DIM, HEAD_DIM), lambda g, c: (g, 0, 0, 0))
    return pl.pallas_call(
        _ret_body,
        out_shape=[jax.ShapeDtypeStruct((ng * nc * ln, BR_WIDTH), bf16),
                   jax.ShapeDtypeStruct((ng, N_HEADS, HEAD_DIM, HEAD_DIM), f32)],
        grid=(ng, nc),
        in_specs=[blk(U_DQ), blk(U_DK), blk(U_DV), blk(U_DZ), rope_spec, rope_spec,
                  pl.BlockSpec((1, BR_WIDTH), lambda g, c: (0, 0)), state_spec],
        out_specs=[pl.BlockSpec((ln, BR_WIDTH), lambda g, c: (g * nc + c, 0)), state_spec],
        scratch_shapes=[pltpu.VMEM((N_HEADS, HEAD_DIM, HEAD_DIM), f32)],
        compiler_params=_params("parallel", "arbitrary"),
        name="retention",
    )(u, u, u, u, cc, ss, gn.reshape(1, BR_WIDTH), s0)


def _sample_branches_body(*refs, lam_init):
    it = iter(refs)

    def take(n):
        return [next(it) for _ in range(n)]

    a_in, c_in, m_in, b_in, d_in = take(8), take(8), take(4), take(15), take(8)
    a_out, c_out, m_out, b_out, d_out = take(3), take(3), take(1), take(4), take(2)
    b_scr, d_scr = take(4), take(1)
    _band_sample_body(*a_in, *a_out)
    _diff_sample_body(*c_in, *c_out, lam_init=lam_init)
    _mem_body(*m_in, *m_out, interleaved=True)
    _mlstm_body(*b_in, *b_out, *b_scr, chunk_axis=None)
    _ret_body(*d_in, *d_out, *d_scr, chunk_axis=None)


def _sample_branches(u, layer, caches, rel_bias, lamp, lam_init, gains, conv_w, conv_b, b_ig, b_fg,
                     states, pos, nb, t):
    ca_k, ca_v, cc_k, cc_v, cm_k, cm_v = caches
    subln_c, gn_b, gn_d = gains
    c0, n0, m0, conv0, s0 = states
    nrow = ca_k.shape[2] // N_HEADS
    bias_c = _rel_bias_toeplitz(rel_bias, t, nrow, nrow)
    bias_n = _rel_bias_toeplitz(rel_bias, t, t, 0)
    gbias = jnp.concatenate([b_ig, b_fg]).astype(f32)
    gbr = jnp.zeros((1, GATE_PAD), f32).at[0, :2 * N_HEADS].set(gbias)
    cc, ss = _rope_tables(pos)

    def blk(col, width=BR_WIDTH):
        return pl.BlockSpec((t, width), lambda b: (b, col // width))

    def const(shape):
        return pl.BlockSpec(shape, lambda b: (0,) * len(shape))

    def per_seq(shape):
        return pl.BlockSpec((None,) + shape, lambda b: (b,) + (0,) * len(shape))

    def cache(arr):
        return pl.BlockSpec((None, None) + arr.shape[2:], lambda b: (layer, b, 0, 0))

    mat, vec = (N_HEADS, HEAD_DIM, HEAD_DIM), (N_HEADS, 1, HEAD_DIM)
    kv_state = (t * N_HEADS, HEAD_DIM)
    in_specs = (
        [blk(U_AQ), blk(U_AK), blk(U_AV), blk(U_AZ), cache(ca_k), cache(ca_v),
         const((N_HEADS, t, nrow)), const((N_HEADS, t, t))]
        + [blk(U_CQ), blk(U_CK), blk(U_CV), blk(U_CZ), cache(cc_k), cache(cc_v),
           const((4, DIFF_DH)), const((1, BR_WIDTH))]
        + [blk(U_MQ), blk(U_MZ), cache(cm_k), cache(cm_v)]
        + [blk(U_BQK, B_QK), blk(U_BV), blk(U_BO), blk(U_BZ), blk(U_GATES, GATE_PAD),
           per_seq((2 * N_HEADS, t)), const((CONV_W, B_QK)), const((1, B_QK)), const((1, GATE_PAD)),
           const((2 * N_HEADS, 1)), const((1, BR_WIDTH)), per_seq(mat), per_seq(vec), per_seq(vec),
           per_seq((CONV_W - 1, B_QK))]
        + [blk(U_DQ), blk(U_DK), blk(U_DV), blk(U_DZ), const((t, HEAD_DIM)), const((t, HEAD_DIM)),
           const((1, BR_WIDTH)), per_seq(mat)])
    args = (
        [u, u, u, u, ca_k, ca_v, bias_c, bias_n]
        + [u, u, u, u, cc_k, cc_v, lamp, subln_c.reshape(1, BR_WIDTH)]
        + [u, u, cm_k, cm_v]
        + [u, u, u, u, u, _gates_rowform(u, 0, nb * t, t), conv_w, conv_b.reshape(1, B_QK), gbr,
           gbias.reshape(2 * N_HEADS, 1), gn_b.reshape(1, BR_WIDTH), c0,
           n0.reshape(nb, N_HEADS, 1, HEAD_DIM),
           jnp.broadcast_to(m0[:, :, None, None], (nb, N_HEADS, 1, HEAD_DIM)), conv0]
        + [u, u, u, u, cc, ss, gn_d.reshape(1, BR_WIDTH), s0])
    branch_out = jax.ShapeDtypeStruct((nb * t, BR_WIDTH), bf16)
    kv_out = jax.ShapeDtypeStruct((nb,) + kv_state, f32)
    mat_out = jax.ShapeDtypeStruct((nb,) + mat, f32)
    vec_out = jax.ShapeDtypeStruct((nb,) + vec, f32)
    out_shape = ([branch_out, kv_out, kv_out] * 2 + [branch_out]
                 + [branch_out, mat_out, vec_out, vec_out] + [branch_out, mat_out])
    o_spec = pl.BlockSpec((t, BR_WIDTH), lambda b: (b, 0))
    out_specs = ([o_spec, per_seq(kv_state), per_seq(kv_state)] * 2 + [o_spec]
                 + [o_spec, per_seq(mat), per_seq(vec), per_seq(vec)] + [o_spec, per_seq(mat)])
    (oa, ak, av, oc, ck, cv, om, ob, c_new, n_new, m_new, od, s_new) = pl.pallas_call(
        functools.partial(_sample_branches_body, lam_init=lam_init),
        out_shape=out_shape,
        grid=(nb,),
        in_specs=in_specs,
        out_specs=out_specs,
        scratch_shapes=[pltpu.VMEM((CONV_PAD + t, B_QK), f32), pltpu.VMEM(mat, f32),
                        pltpu.VMEM(vec, f32), pltpu.VMEM(vec, f32), pltpu.VMEM(mat, f32)],
        compiler_params=_params("arbitrary"),
        name="sample_branches",
    )(*args)
    return ((oa, ob, oc, od, om), (ak, av, ck, cv),
            (c_new, n_new[:, :, 0, :], m_new[:, :, 0, 0], s_new))


def _merge_body(x_ref, g_ref, oa_ref, ob_ref, oc_ref, od_ref, om_ref,
                wg0_ref, wg1_ref, wg2_ref, wg3_ref, wg4_ref, wb_ref, wo_ref, fg_ref,
                y_ref, h_scr, *, rows, final):
    n = pl.program_id(1)

    @pl.when(n == 0)
    def _():
        _rmsnorm_rows(x_ref, g_ref, h_scr, rows)
        y_ref[...] = jnp.zeros(y_ref.shape, f32)

    h = h_scr[...]
    merged = None
    for i, (o_ref, wg_ref) in enumerate(zip((oa_ref, ob_ref, oc_ref, od_ref, om_ref),
                                            (wg0_ref, wg1_ref, wg2_ref, wg3_ref, wg4_ref))):
        gate = _sigmoid(lax.dot_general(h, wg_ref[...], NT_DIMS, preferred_element_type=f32))
        term = gate * jnp.dot(o_ref[...], wb_ref[i], preferred_element_type=f32)
        merged = term if merged is None else merged + term
    y_ref[...] += jnp.dot(merged.astype(bf16), wo_ref[...], preferred_element_type=f32)

    @pl.when(n == pl.num_programs(1) - 1)
    def _():
        tm = x_ref.shape[0]
        for r in range(0, tm, rows):
            y = x_ref[r:r + rows, :] + y_ref[r:r + rows, :]
            if final:
                y = (y * lax.rsqrt(jnp.mean(y * y, axis=-1, keepdims=True) + EPS)) * fg_ref[...]
            y_ref[r:r + rows, :] = y


def _merge(x, g, outs, wg, wb, wo, fg, *, final, name, tm_target=704, tn=256):
    m, d = x.shape
    tm = _divisor_tile(m, tm_target, 16)
    rows = _divisor_tile(tm, 256, 8)
    nn = d // tn

    def wg_spec(i):
        return pl.BlockSpec((tn, d), lambda r, n: (i * nn + n, 0))

    o_spec = pl.BlockSpec((tm, BR_WIDTH), lambda r, n: (r, 0))
    return pl.pallas_call(
        functools.partial(_merge_body, rows=rows, final=final),
        out_shape=jax.ShapeDtypeStruct((m, d), f32),
        grid=(m // tm, nn),
        in_specs=[pl.BlockSpec((tm, d), lambda r, n: (r, 0)),
                  pl.BlockSpec((1, d), lambda r, n: (0, 0))]
                 + [o_spec] * N_BRANCH
                 + [wg_spec(i) for i in range(N_BRANCH)]
                 + [pl.BlockSpec((N_BRANCH, BR_WIDTH, tn), lambda r, n: (0, 0, n)),
                    pl.BlockSpec((tn, d), lambda r, n: (n, 0)),
                    pl.BlockSpec((1, d), lambda r, n: (0, 0))],
        out_specs=pl.BlockSpec((tm, d), lambda r, n: (r, 0)),
        scratch_shapes=[pltpu.VMEM((tm, d), bf16)],
        compiler_params=_params("parallel", "arbitrary"),
        name=name,
    )(x, g.reshape(1, d), *outs, wg, wg, wg, wg, wg, wb, wo, fg.reshape(1, d))


F32_SUBLANES = 8


def _cast_rows_body(src_ref, o_ref, *, valid_rows_last):
    tr = o_ref.shape[0]
    x = src_ref[0]
    if valid_rows_last is not None:
        row = lax.broadcasted_iota(jnp.int32, (tr, 1), 0)
        keep = jnp.logical_or(pl.program_id(0) < pl.num_programs(0) - 1, row < valid_rows_last)
        x = jnp.where(keep, x, 0.0)
    o_ref[...] = x.astype(bf16)


def _cast_rows(w_t, layer, *, n_rows, tr, src_row, valid_rows_last, name):
    d = w_t.shape[2]
    assert n_rows % tr == 0
    return pl.pallas_call(
        functools.partial(_cast_rows_body, valid_rows_last=valid_rows_last),
        out_shape=jax.ShapeDtypeStruct((n_rows, d), bf16),
        grid=(n_rows // tr,),
        in_specs=[pl.BlockSpec((pl.Element(1), pl.Element(tr), pl.Element(d)),
                               lambda j: (layer, pl.multiple_of(src_row(j), F32_SUBLANES), 0))],
        out_specs=pl.BlockSpec((tr, d), lambda j: (j, 0)),
        compiler_params=_params("arbitrary"),
        name=name,
    )(w_t)


def _repack_w_in(w_t, layer):
    tr = GATE_PAD
    n_plain, n_main = W_BI // tr, U_GATES // tr

    def u_src(j):
        return jnp.where(j < n_plain, j * tr, jnp.where(j < n_main, j * tr + (W_BZ - W_BI), W_BI))

    wu = _cast_rows(w_t, layer, n_rows=U_WIDTH, tr=tr, src_row=u_src, valid_rows_last=2 * N_HEADS,
                    name="repack_u")
    wg = _cast_rows(w_t, layer, n_rows=N_BRANCH * D_MODEL, tr=512, src_row=lambda j: W_GATE + j * 512,
                    valid_rows_last=None, name="repack_g")
    return wu, wg


def _gates_rowform(u, row0, nrows, ln):
    g = u[row0:row0 + nrows, U_GATES:U_GATES + 2 * N_HEADS]
    return g.reshape(nrows // ln, ln, 2 * N_HEADS).transpose(0, 2, 1)


def kernel(x_prompt, x_sample, mem_prompt, cache_a_k, cache_a_v, cache_c_k, cache_c_v, cache_mem_k, cache_mem_v, state_b_C, state_b_n, state_b_m, state_b_conv, state_d_S, norm_g, w_in, conv_w, conv_b, b_ig, b_fg, rel_bias, lam_q1, lam_k1, lam_q2, lam_k2, gn_b, subln_c, gn_d, mem_norm_g, w_mk, w_mv, w_branch, w_out, final_g):
    batch, seq, d = x_prompt.shape
    nb, t, _ = x_sample.shape
    depth = w_in.shape[0]
    n_mem = mem_prompt.shape[1]
    past = cache_c_k.shape[2]
    mp = batch * seq
    assert d == D_MODEL and seq % C_BLOCK == 0 and seq % A_QBLOCK == 0 and seq % SCAN_CHUNK == 0

    xp = x_prompt.reshape(mp, d)
    xs = x_sample.reshape(nb * t, d)
    mem = mem_prompt.reshape(batch * n_mem, d)
    ca_k, ca_v = _cache_rows(cache_a_k), _cache_rows(cache_a_v)
    cc_k, cc_v = _cache_rows(cache_c_k), _cache_rows(cache_c_v)
    cm_k, cm_v = _cache_rows(cache_mem_k), _cache_rows(cache_mem_v)
    pos_p = jnp.arange(seq)
    pos_s = past + jnp.arange(t)
    a_rows = min(A_WINDOW, seq)
    mem_tq = _divisor_tile(seq, 512, 16)

    w_t = jnp.swapaxes(w_in, 1, 2)

    sp, ss = [], []
    for l in range(depth):
        lam_init = 0.8 - 0.6 * math.exp(-0.3 * l)
        lamp = jnp.stack([lam_q1[l], lam_k1[l], lam_q2[l], lam_k2[l]]).astype(f32)
        wu, wg = _repack_w_in(w_t, l)
        u = _norm_matmul(xp, norm_g[l], wu, tm_target=1024, tn=896, w_rows_are_outputs=True, name="in_proj")
        us = _norm_matmul(xs, norm_g[l], wu, tm_target=1024, tn=896, w_rows_are_outputs=True,
                          name="in_proj_sample")
        mkv = _norm_matmul(mem, mem_norm_g[l], jnp.concatenate([w_mk[l], w_mv[l]], axis=1).astype(bf16),
                           tm_target=512, tn=512, w_rows_are_outputs=False, name="mem_proj")

        oa_p, ak_p, av_p = _band_prompt(u, rel_bias[l], batch, seq, a_rows)
        zeros_c = jnp.zeros((batch, N_HEADS, HEAD_DIM, HEAD_DIM), f32)
        ob_p, c_p, n_p, m_p = _mlstm(
            u, _gates_rowform(u, 0, mp, SCAN_CHUNK), conv_w[l], conv_b[l], b_ig[l], b_fg[l], gn_b[l],
            zeros_c, jnp.zeros((batch, N_HEADS, HEAD_DIM), f32), jnp.zeros((batch, N_HEADS), f32),
            jnp.zeros((batch, CONV_W - 1, B_QK), f32), 0, batch, seq // SCAN_CHUNK, SCAN_CHUNK)
        oc_p, ck_p, cv_p = _diff_prompt(u, lamp, subln_c[l], lam_init, batch, seq)
        od_p, s_p = _retention(u, pos_p, gn_d[l], zeros_c, 0, batch, seq // SCAN_CHUNK, SCAN_CHUNK)
        mem_spec_k = pl.BlockSpec((n_mem, BR_WIDTH), lambda b, i: (b, 0))
        mem_spec_v = pl.BlockSpec((n_mem, BR_WIDTH), lambda b, i: (b, 1))
        om_p = _mem_attn(u, 0, batch, seq, mem_tq, mkv, mkv, mem_spec_k, mem_spec_v, False, "mem_attn_prompt")

        outs_s, (ak_s, av_s, ck_s, cv_s), (c_s, n_s, m_s, s_s) = _sample_branches(
            us, l, (ca_k, ca_v, cc_k, cc_v, cm_k, cm_v), rel_bias[l], lamp, lam_init,
            (subln_c[l], gn_b[l], gn_d[l]), conv_w[l], conv_b[l], b_ig[l], b_fg[l],
            (state_b_C[l].astype(f32), state_b_n[l].astype(f32), state_b_m[l].astype(f32),
             state_b_conv[l].astype(f32), state_d_S[l].astype(f32)), pos_s, nb, t)

        wb, wo = w_branch[l].astype(bf16), w_out[l].astype(bf16)
        final = l == depth - 1
        xp = _merge(xp, norm_g[l], (oa_p, ob_p, oc_p, od_p, om_p), wg, wb, wo, final_g, final=final,
                    name="gated_merge")
        xs = _merge(xs, norm_g[l], outs_s, wg, wb, wo, final_g, final=final, name="gated_merge_sample")

        def seg_p(col, width, first_row):
            return u[:, col:col + width].reshape(batch, seq, width)[:, first_row:]

        def seg_s(col, width, first_row):
            return us[:, col:col + width].reshape(nb, t, width)[:, first_row:]

        def hd(a):
            return a.reshape(a.shape[:2] + (N_HEADS, HEAD_DIM))

        def rows_hd(a):
            return a.reshape(a.shape[0], a.shape[1] // N_HEADS, N_HEADS, HEAD_DIM)

        sp.append((rows_hd(ak_p), rows_hd(av_p), rows_hd(ck_p), rows_hd(cv_p),
                   hd(mkv[:, :BR_WIDTH].reshape(batch, n_mem, BR_WIDTH)),
                   hd(mkv[:, BR_WIDTH:].reshape(batch, n_mem, BR_WIDTH)),
                   c_p, n_p, m_p, seg_p(U_BQK, B_QK, seq - (CONV_W - 1)), s_p))
        ss.append((rows_hd(ak_s), rows_hd(av_s), rows_hd(ck_s), rows_hd(cv_s),
                   c_s, n_s, m_s, seg_s(U_BQK, B_QK, t - (CONV_W - 1)), s_s))

    y_prompt = xp.reshape(batch, seq, d)
    y_sample = xs.reshape(nb, t, d)
    p_states = tuple(jnp.stack([st[i] for st in sp]) for i in range(11))
    s_states = tuple(jnp.stack([st[i] for st in ss]) for i in range(9))
    return (y_prompt, y_sample) + p_states + s_states
```

```python
import functools
import math

import jax
import jax.numpy as jnp
import numpy as np
from jax import lax
from jax.experimental import pallas as pl
from jax.experimental.pallas import tpu as pltpu

f32 = jnp.float32
bf16 = jnp.bfloat16

D_MODEL = 2048
N_HEADS = 4
HEAD_DIM = 128
BR_WIDTH = N_HEADS * HEAD_DIM
N_BRANCH = 5
CHUNK = 64
A_PREV_CHUNKS = 8
A_WINDOW = A_PREV_CHUNKS * CHUNK
A_BAND = (A_PREV_CHUNKS + 1) * CHUNK
A_REL_CLIP = 128
CONV_W = 4
B_QK = 2 * BR_WIDTH
DIFF_DH = HEAD_DIM // 2
RET_GAMMA_EXP0 = 5.0
ROPE_BASE = 10000.0
EPS = 1e-6
NEG = -1e30
HEAD_SCALE = HEAD_DIM ** -0.5
DIFF_SCALE = DIFF_DH ** -0.5

U_AQ, U_AK, U_AV, U_AZ = 0, 512, 1024, 1536
U_BQK, U_BV, U_BO, U_BZ = 2048, 3072, 3584, 4096
U_CQ, U_CK, U_CV, U_CZ = 4608, 5120, 5632, 6144
U_DQ, U_DK, U_DV, U_DZ = 6656, 7168, 7680, 8192
U_MQ, U_MZ = 8704, 9216
U_GATES = 9728
GATE_PAD = 128
MXU_COLS = 256
U_WIDTH = U_GATES + MXU_COLS
U_TILE = 3 * MXU_COLS
W_BI = 4096
W_BZ = 4104
W_GATE = 9736

VMEM_LIMIT_BYTES = 56 * 1024 * 1024
A_QBLOCK = 4 * CHUNK
C_BLOCK = 512
SCAN_CHUNK = 256
NT_DIMS = (((1,), (1,)), ((), ()))
TN_DIMS = (((0,), (0,)), ((), ()))


def _divisor_tile(n, target, multiple):
    best = None
    for t in range(multiple, min(n, target) + 1, multiple):
        if n % t == 0:
            best = t
    assert best is not None, (n, target, multiple)
    return best


def _params(*sem):
    return pltpu.CompilerParams(dimension_semantics=sem, vmem_limit_bytes=VMEM_LIMIT_BYTES)


def _sigmoid(x):
    return 1.0 / (1.0 + jnp.exp(-x))


def _silu(x):
    return x * _sigmoid(x)


def _log_sigmoid(x):
    return jnp.minimum(x, 0.0) - jnp.log1p(jnp.exp(-jnp.abs(x)))


def _head_norm(y, gain):
    return y * lax.rsqrt(jnp.mean(y * y, axis=-1, keepdims=True) + EPS) * gain


def _rmsnorm_rows(x_ref, g_ref, h_ref, rows):
    tm = x_ref.shape[0]
    for r in range(0, tm, rows):
        x = x_ref[r:r + rows, :]
        ms = jnp.mean(x * x, axis=-1, keepdims=True)
        h_ref[r:r + rows, :] = ((x * lax.rsqrt(ms + EPS)) * g_ref[...]).astype(bf16)


def _norm_matmul_body(x_ref, g_ref, w_ref, o_ref, h_ref, *, rows, w_rows_are_outputs):
    @pl.when(pl.program_id(1) == 0)
    def _():
        _rmsnorm_rows(x_ref, g_ref, h_ref, rows)

    if w_rows_are_outputs:
        o = lax.dot_general(h_ref[...], w_ref[...], NT_DIMS, preferred_element_type=f32)
    else:
        o = jnp.dot(h_ref[...], w_ref[...], preferred_element_type=f32)
    o_ref[...] = o.astype(o_ref.dtype)


def _norm_matmul(x, g, w, *, tm_target, tn, w_rows_are_outputs, name):
    m, d = x.shape
    n = w.shape[0] if w_rows_are_outputs else w.shape[1]
    tm = _divisor_tile(m, tm_target, 16)
    rows = _divisor_tile(tm, 256, 8)
    assert n % tn == 0
    w_spec = (pl.BlockSpec((tn, d), lambda i, j: (j, 0)) if w_rows_are_outputs
              else pl.BlockSpec((d, tn), lambda i, j: (0, j)))
    return pl.pallas_call(
        functools.partial(_norm_matmul_body, rows=rows, w_rows_are_outputs=w_rows_are_outputs),
        out_shape=jax.ShapeDtypeStruct((m, n), f32),
        grid=(m // tm, n // tn),
        in_specs=[pl.BlockSpec((tm, d), lambda i, j: (i, 0)),
                  pl.BlockSpec((1, d), lambda i, j: (0, 0)),
                  w_spec],
        out_specs=pl.BlockSpec((tm, tn), lambda i, j: (i, j)),
        scratch_shapes=[pltpu.VMEM((tm, d), bf16)],
        compiler_params=_params("parallel", "arbitrary"),
        name=name,
    )(x, g.reshape(1, d), w)


def _store_head_rows(dst_ref, src_ref):
    rows = src_ref.shape[0]
    for h in range(N_HEADS):
        dst_ref[pl.ds(h, rows, stride=N_HEADS), :] = src_ref[:, h * HEAD_DIM:(h + 1) * HEAD_DIM]


def _band_prompt_body(q_ref, k0_ref, k1_ref, k2_ref, v0_ref, v1_ref, v2_ref, z_ref, bias_ref,
                      o_ref, ks_ref, vs_ref, *, n_tail):
    tq = q_ref.shape[0]
    i = pl.program_id(1)

    @pl.when(i >= pl.num_programs(1) - n_tail)
    def _():
        _store_head_rows(ks_ref, k2_ref)
        _store_head_rows(vs_ref, v2_ref)

    w_idx = lax.broadcasted_iota(jnp.int32, (tq, 3 * tq), 1)
    valid = (w_idx + (i - 2) * tq) >= 0
    for h in range(N_HEADS):
        sl = slice(h * HEAD_DIM, (h + 1) * HEAD_DIM)
        q = q_ref[:, sl].astype(bf16)
        kw = jnp.concatenate([k0_ref[:, sl], k1_ref[:, sl], k2_ref[:, sl]], axis=0).astype(bf16)
        vw = jnp.concatenate([v0_ref[:, sl], v1_ref[:, sl], v2_ref[:, sl]], axis=0).astype(bf16)
        s = lax.dot_general(q, kw, NT_DIMS, preferred_element_type=f32) * HEAD_SCALE + bias_ref[h]
        s = jnp.where(valid, s, NEG)
        p = jnp.exp(s - jnp.max(s, axis=-1, keepdims=True))
        l = jnp.sum(p, axis=-1, keepdims=True)
        o = jnp.dot(p.astype(bf16), vw, preferred_element_type=f32) / l
        o_ref[:, sl] = (o * _silu(z_ref[:, sl])).astype(bf16)


def _rel_bias_toeplitz(rel_bias, rows, cols, rel0):
    p = rows + cols
    j = np.arange(p)
    d = np.where(j < cols, j, j - p)
    idx = np.clip(rel0 - d, -A_REL_CLIP, A_REL_CLIP) + A_REL_CLIP
    v = rel_bias.astype(f32)[:, idx]
    flat = jnp.tile(v, (1, rows))[:, :rows * (p - 1)]
    return flat.reshape(rel_bias.shape[0], rows, p - 1)[:, :, :cols]


def _band_bias_prompt(rel_bias):
    r = np.arange(A_QBLOCK)[:, None]
    w = np.arange(3 * A_QBLOCK)[None, :]
    kj = w - CHUNK * (r // CHUNK)
    inside = (kj >= 0) & (kj < A_BAND)
    table = _rel_bias_toeplitz(rel_bias, A_QBLOCK, 3 * A_QBLOCK, 2 * A_QBLOCK)
    return jnp.where(jnp.asarray(inside)[None], table, NEG)


def _band_prompt(u, rel_bias, batch, seq, a_rows):
    tq = A_QBLOCK
    nqb = seq // tq
    assert a_rows % tq == 0
    n_tail = a_rows // tq
    bias = _band_bias_prompt(rel_bias)

    def blk(col, back):
        return pl.BlockSpec((tq, BR_WIDTH),
                            lambda b, i: (b * nqb + jnp.maximum(i - back, 0), col // BR_WIDTH))

    state_shape = jax.ShapeDtypeStruct((batch, a_rows * N_HEADS, HEAD_DIM), f32)
    state_spec = pl.BlockSpec((None, tq * N_HEADS, HEAD_DIM),
                              lambda b, i: (b, jnp.maximum(i - (nqb - n_tail), 0), 0))
    return pl.pallas_call(
        functools.partial(_band_prompt_body, n_tail=n_tail),
        out_shape=[jax.ShapeDtypeStruct((batch * seq, BR_WIDTH), bf16), state_shape, state_shape],
        grid=(batch, nqb),
        in_specs=[blk(U_AQ, 0),
                  blk(U_AK, 2), blk(U_AK, 1), blk(U_AK, 0),
                  blk(U_AV, 2), blk(U_AV, 1), blk(U_AV, 0),
                  blk(U_AZ, 0),
                  pl.BlockSpec((N_HEADS, tq, 3 * tq), lambda b, i: (0, 0, 0))],
        out_specs=[pl.BlockSpec((tq, BR_WIDTH), lambda b, i: (b * nqb + i, 0)), state_spec, state_spec],
        compiler_params=_params("parallel", "arbitrary"),
        name="band_attn_prompt",
    )(u, u, u, u, u, u, u, u, bias)


def _band_sample_body(q_ref, k_ref, v_ref, z_ref, ck_ref, cv_ref, bc_ref, bn_ref, o_ref, ks_ref, vs_ref):
    _store_head_rows(ks_ref, k_ref)
    _store_head_rows(vs_ref, v_ref)
    for h in range(N_HEADS):
        sl = slice(h * HEAD_DIM, (h + 1) * HEAD_DIM)
        q = q_ref[:, sl].astype(bf16)
        s_c = lax.dot_general(q, _head_rows(ck_ref, h, True).astype(bf16), NT_DIMS,
                              preferred_element_type=f32) * HEAD_SCALE + bc_ref[h]
        s_n = lax.dot_general(q, k_ref[:, sl].astype(bf16), NT_DIMS,
                              preferred_element_type=f32) * HEAD_SCALE + bn_ref[h]
        m = jnp.maximum(jnp.max(s_c, axis=-1, keepdims=True), jnp.max(s_n, axis=-1, keepdims=True))
        p_c = jnp.exp(s_c - m)
        p_n = jnp.exp(s_n - m)
        l = jnp.sum(p_c, axis=-1, keepdims=True) + jnp.sum(p_n, axis=-1, keepdims=True)
        o = (jnp.dot(p_c.astype(bf16), _head_rows(cv_ref, h, True).astype(bf16), preferred_element_type=f32)
             + jnp.dot(p_n.astype(bf16), v_ref[:, sl].astype(bf16), preferred_element_type=f32)) / l
        o_ref[:, sl] = (o * _silu(z_ref[:, sl])).astype(bf16)


def _head_rows(ref, h, interleaved):
    if interleaved:
        return ref[pl.ds(h, ref.shape[0] // N_HEADS, stride=N_HEADS), :]
    return ref[:, h * HEAD_DIM:(h + 1) * HEAD_DIM]


def _cache_rows(cache):
    d0, d1, rows = cache.shape[:3]
    return cache.reshape(d0, d1, rows * N_HEADS, HEAD_DIM)


def _mem_body(q_ref, z_ref, k_ref, v_ref, o_ref, *, interleaved):
    for h in range(N_HEADS):
        sl = slice(h * HEAD_DIM, (h + 1) * HEAD_DIM)
        q = q_ref[:, sl].astype(bf16)
        k = _head_rows(k_ref, h, interleaved).astype(bf16)
        v = _head_rows(v_ref, h, interleaved).astype(bf16)
        s = lax.dot_general(q, k, NT_DIMS, preferred_element_type=f32) * HEAD_SCALE
        p = jnp.exp(s - jnp.max(s, axis=-1, keepdims=True))
        l = jnp.sum(p, axis=-1, keepdims=True)
        o = jnp.dot(p.astype(bf16), v, preferred_element_type=f32) / l
        o_ref[:, sl] = (o * _silu(z_ref[:, sl])).astype(bf16)


def _mem_attn(u, row0, nb, t, tq, k_arr, v_arr, k_spec, v_spec, interleaved, name):
    nq = t // tq
    rb0 = row0 // tq

    def blk(col):
        return pl.BlockSpec((tq, BR_WIDTH), lambda b, i: (rb0 + b * nq + i, col // BR_WIDTH))

    return pl.pallas_call(
        functools.partial(_mem_body, interleaved=interleaved),
        out_shape=jax.ShapeDtypeStruct((nb * t, BR_WIDTH), bf16),
        grid=(nb, nq),
        in_specs=[blk(U_MQ), blk(U_MZ), k_spec, v_spec],
        out_specs=pl.BlockSpec((tq, BR_WIDTH), lambda b, i: (b * nq + i, 0)),
        compiler_params=_params("parallel", "arbitrary"),
        name=name,
    )(u, u, k_arr, v_arr)


def _diff_lambda(lamp_ref, lam_init):
    lp = lamp_ref[...]
    a = jnp.sum(lp[0:1] * lp[1:2], axis=-1, keepdims=True)
    b = jnp.sum(lp[2:3] * lp[3:4], axis=-1, keepdims=True)
    return jnp.exp(a) - jnp.exp(b) + lam_init


def _diff_epilogue(o0, o1, lam, gain, z, lam_init):
    y = _head_norm(o0 - lam * o1, gain) * (1.0 - lam_init)
    return (y * _silu(z)).astype(bf16)


def _half_masks():
    lane = lax.broadcasted_iota(jnp.int32, (1, HEAD_DIM), 1)
    return lane < DIFF_DH, lane >= DIFF_DH


def _diff_prompt_body(q_ref, k_ref, v_ref, z_ref, lamp_ref, gain_ref, o_ref, ks_ref, vs_ref,
                      m_scr, l_scr, acc_scr, *, lam_init):
    tq, tk = q_ref.shape[0], k_ref.shape[0]
    i = pl.program_id(1)
    j = pl.program_id(2)

    @pl.when(j == 0)
    def _():
        m_scr[...] = jnp.full(m_scr.shape, NEG, f32)
        l_scr[...] = jnp.zeros(l_scr.shape, f32)
        acc_scr[...] = jnp.zeros(acc_scr.shape, f32)

    lo, hi = _half_masks()

    def step(diagonal):
        if diagonal:
            kc = lax.broadcasted_iota(jnp.int32, (tk, 2 * tq), 0) // CHUNK
            qpos = lax.broadcasted_iota(jnp.int32, (tk, 2 * tq), 1)
            qc = jnp.where(qpos >= tq, qpos - tq, qpos) // CHUNK
            mask = kc <= qc
        for h in range(N_HEADS):
            sl = slice(h * HEAD_DIM, (h + 1) * HEAD_DIM)
            q = q_ref[:, sl] * DIFF_SCALE
            qq = jnp.concatenate([jnp.where(lo, q, 0.0), jnp.where(hi, q, 0.0)], axis=0).astype(bf16)
            k = k_ref[:, sl].astype(bf16)
            vt = v_ref[:, sl].T.astype(bf16)
            s = lax.dot_general(k, qq, NT_DIMS, preferred_element_type=f32)
            if diagonal:
                s = jnp.where(mask, s, NEG)
            m_old = m_scr[h]
            m_new = jnp.maximum(m_old, jnp.max(s, axis=0, keepdims=True))
            p = jnp.exp(s - m_new)
            alpha = jnp.exp(m_old - m_new)
            l_scr[h] = alpha * l_scr[h] + jnp.sum(p, axis=0, keepdims=True)
            acc_scr[h] = alpha * acc_scr[h] + jnp.dot(vt, p.astype(bf16), preferred_element_type=f32)
            m_scr[h] = m_new

    @pl.when(j < i)
    def _():
        step(False)

    @pl.when(j == i)
    def _():
        step(True)
        _store_head_rows(ks_ref, k_ref)
        _store_head_rows(vs_ref, v_ref)
        lam = _diff_lambda(lamp_ref, lam_init)
        for h in range(N_HEADS):
            sl = slice(h * HEAD_DIM, (h + 1) * HEAD_DIM)
            on = acc_scr[h] / l_scr[h]
            o0 = on[:, :tq].T
            o1 = on[:, tq:].T
            o_ref[:, sl] = _diff_epilogue(o0, o1, lam, gain_ref[:, sl], z_ref[:, sl], lam_init)


def _diff_prompt(u, lamp, gain, lam_init, batch, seq):
    t = _divisor_tile(seq, C_BLOCK, 2 * CHUNK)
    nt = seq // t

    def qblk(col):
        return pl.BlockSpec((t, BR_WIDTH), lambda b, i, j: (b * nt + i, col // BR_WIDTH))

    def kblk(col):
        return pl.BlockSpec((t, BR_WIDTH), lambda b, i, j: (b * nt + jnp.minimum(j, i), col // BR_WIDTH))

    state_shape = jax.ShapeDtypeStruct((batch, seq * N_HEADS, HEAD_DIM), f32)
    state_spec = pl.BlockSpec((None, t * N_HEADS, HEAD_DIM), lambda b, i, j: (b, i, 0))
    return pl.pallas_call(
        functools.partial(_diff_prompt_body, lam_init=lam_init),
        out_shape=[jax.ShapeDtypeStruct((batch * seq, BR_WIDTH), bf16), state_shape, state_shape],
        grid=(batch, nt, nt),
        in_specs=[qblk(U_CQ), kblk(U_CK), kblk(U_CV), qblk(U_CZ),
                  pl.BlockSpec((4, DIFF_DH), lambda b, i, j: (0, 0)),
                  pl.BlockSpec((1, BR_WIDTH), lambda b, i, j: (0, 0))],
        out_specs=[pl.BlockSpec((t, BR_WIDTH), lambda b, i, j: (b * nt + i, 0)), state_spec, state_spec],
        scratch_shapes=[pltpu.VMEM((N_HEADS, 1, 2 * t), f32),
                        pltpu.VMEM((N_HEADS, 1, 2 * t), f32),
                        pltpu.VMEM((N_HEADS, HEAD_DIM, 2 * t), f32)],
        compiler_params=_params("parallel", "parallel", "arbitrary"),
        name="diff_attn_prompt",
    )(u, u, u, u, lamp, gain.reshape(1, BR_WIDTH))


def _diff_sample_body(q_ref, k_ref, v_ref, z_ref, ck_ref, cv_ref, lamp_ref, gain_ref, o_ref, ks_ref, vs_ref,
                      *, lam_init):
    t = q_ref.shape[0]
    _store_head_rows(ks_ref, k_ref)
    _store_head_rows(vs_ref, v_ref)
    lo, hi = _half_masks()
    lam = _diff_lambda(lamp_ref, lam_init)
    for h in range(N_HEADS):
        sl = slice(h * HEAD_DIM, (h + 1) * HEAD_DIM)
        q = q_ref[:, sl] * DIFF_SCALE
        qq = jnp.concatenate([jnp.where(lo, q, 0.0), jnp.where(hi, q, 0.0)], axis=0).astype(bf16)
        k = k_ref[:, sl].astype(bf16)
        v = v_ref[:, sl].astype(bf16)
        ck = _head_rows(ck_ref, h, True).astype(bf16)
        cv = _head_rows(cv_ref, h, True).astype(bf16)
        s_c = lax.dot_general(qq, ck, NT_DIMS, preferred_element_type=f32)
        s_n = lax.dot_general(qq, k, NT_DIMS, preferred_element_type=f32)
        m = jnp.maximum(jnp.max(s_c, axis=-1, keepdims=True), jnp.max(s_n, axis=-1, keepdims=True))
        p_c = jnp.exp(s_c - m)
        p_n = jnp.exp(s_n - m)
        l = jnp.sum(p_c, axis=-1, keepdims=True) + jnp.sum(p_n, axis=-1, keepdims=True)
        o = (jnp.dot(p_c.astype(bf16), cv, preferred_element_type=f32)
             + jnp.dot(p_n.astype(bf16), v, preferred_element_type=f32)) / l
        o_ref[:, sl] = _diff_epilogue(o[:t], o[t:], lam, gain_ref[:, sl], z_ref[:, sl], lam_init)


CONV_PAD = 8


def _first_last(chunk_axis):
    if chunk_axis is None:
        def run(f):
            f()
        return run, run
    c = pl.program_id(chunk_axis)
    return pl.when(c == 0), pl.when(c == pl.num_programs(chunk_axis) - 1)


def _mlstm_body(qk_ref, v_ref, og_ref, z_ref, gc_ref, gr_ref, cw_ref, cb_ref, gbr_ref, gbc_ref, gn_ref,
                c0_ref, n0_ref, m0_ref, conv0_ref,
                out_ref, c_out_ref, n_out_ref, m_out_ref,
                cbuf, c_scr, n_scr, m_scr, *, chunk_axis=1):
    ln = qk_ref.shape[0]
    on_first, on_last = _first_last(chunk_axis)
    lo = CONV_PAD - (CONV_W - 1)

    @on_first
    def _():
        cbuf[lo:CONV_PAD, :] = conv0_ref[...]
        c_scr[...] = c0_ref[...]
        n_scr[...] = n0_ref[...]
        m_scr[...] = m0_ref[...]

    cbuf[CONV_PAD:CONV_PAD + ln, :] = qk_ref[...]
    conv = cb_ref[...] + cbuf[lo:lo + ln, :] * cw_ref[0:1, :]
    for jj in range(1, CONV_W):
        conv = conv + cbuf[lo + jj:lo + jj + ln, :] * cw_ref[jj:jj + 1, :]
    tail = cbuf[lo + ln:CONV_PAD + ln, :]
    cbuf[lo:CONV_PAD, :] = tail
    act = _silu(conv)

    gcb = gc_ref[...] + gbr_ref[...]
    grb = gr_ref[...] + gbc_ref[...]
    row = lax.broadcasted_iota(jnp.int32, (ln, ln), 0)
    col = lax.broadcasted_iota(jnp.int32, (ln, ln), 1)
    causal = col <= row

    for h in range(N_HEADS):
        sl = slice(h * HEAD_DIM, (h + 1) * HEAD_DIM)
        q = act[:, h * HEAD_DIM:(h + 1) * HEAD_DIM]
        k = act[:, BR_WIDTH + h * HEAD_DIM:BR_WIDTH + (h + 1) * HEAD_DIM] * HEAD_SCALE
        v = v_ref[:, sl]
        i_col = gcb[:, h:h + 1]
        lf_col = _log_sigmoid(gcb[:, N_HEADS + h:N_HEADS + h + 1])
        i_row = grb[h:h + 1, :]
        lf_row = _log_sigmoid(grb[N_HEADS + h:N_HEADS + h + 1, :])
        b_col = jnp.sum(jnp.where(causal, lf_row, 0.0), axis=1, keepdims=True)
        b_row = jnp.sum(jnp.where(row <= col, lf_col, 0.0), axis=0, keepdims=True)
        b_last = b_col[ln - 1:ln, :]
        m_old = m_scr[h][:, 0:1]
        c_old = c_scr[h]
        n_old = n_scr[h]

        dlog = jnp.where(causal, b_col - b_row + i_row, NEG)
        inter = b_col + m_old
        mt = jnp.maximum(inter, jnp.max(dlog, axis=1, keepdims=True))
        w = jnp.exp(dlog - mt)
        wi = jnp.exp(inter - mt)
        qb = q.astype(bf16)
        vb = v.astype(bf16)
        qk = lax.dot_general(qb, k.astype(bf16), NT_DIMS, preferred_element_type=f32) * w
        num = (jnp.dot(qk.astype(bf16), vb, preferred_element_type=f32)
               + wi * jnp.dot(qb, c_old.astype(bf16), preferred_element_type=f32))
        den = jnp.sum(qk, axis=1, keepdims=True) + wi * jnp.sum(q * n_old, axis=1, keepdims=True)
        hh = num / jnp.maximum(jnp.abs(den), jnp.exp(-mt))

        m_new = mt[ln - 1:ln, :]
        ws = jnp.exp(b_last - b_col + i_col - m_new)
        dec = jnp.exp(b_last + m_old - m_new)
        kw = k * ws
        c_scr[h] = dec * c_old + lax.dot_general(kw.astype(bf16), vb, TN_DIMS, preferred_element_type=f32)
        n_scr[h] = dec * n_old + jnp.sum(kw, axis=0, keepdims=True)
        m_scr[h] = jnp.broadcast_to(m_new, (1, HEAD_DIM))

        y = _head_norm(_sigmoid(og_ref[:, sl]) * hh, gn_ref[:, sl])
        out_ref[:, sl] = (y * _silu(z_ref[:, sl])).astype(bf16)

    @on_last
    def _():
        c_out_ref[...] = c_scr[...]
        n_out_ref[...] = n_scr[...]
        m_out_ref[...] = m_scr[...]


def _mlstm(u, gates_r, conv_w, conv_b, b_ig, b_fg, gn, c0, n0, m0, conv0, row0, ng, nc, ln):
    rb0 = row0 // ln
    gbias = jnp.concatenate([b_ig, b_fg]).astype(f32)
    gbr = jnp.zeros((1, GATE_PAD), f32).at[0, :2 * N_HEADS].set(gbias)
    gbc = gbias.reshape(2 * N_HEADS, 1)

    def blk(col, width):
        return pl.BlockSpec((ln, width), lambda g, c: (rb0 + g * nc + c, col // width))

    def const(shape):
        return pl.BlockSpec(shape, lambda g, c: (0,) * len(shape))

    def per_seq(shape):
        return pl.BlockSpec((None,) + shape, lambda g, c: (g,) + (0,) * len(shape))

    state_shapes = [jax.ShapeDtypeStruct((ng, N_HEADS, HEAD_DIM, HEAD_DIM), f32),
                    jax.ShapeDtypeStruct((ng, N_HEADS, 1, HEAD_DIM), f32),
                    jax.ShapeDtypeStruct((ng, N_HEADS, 1, HEAD_DIM), f32)]
    state_specs = [per_seq((N_HEADS, HEAD_DIM, HEAD_DIM)),
                   per_seq((N_HEADS, 1, HEAD_DIM)),
                   per_seq((N_HEADS, 1, HEAD_DIM))]
    out, c_new, n_new, m_new = pl.pallas_call(
        _mlstm_body,
        out_shape=[jax.ShapeDtypeStruct((ng * nc * ln, BR_WIDTH), bf16)] + state_shapes,
        grid=(ng, nc),
        in_specs=[blk(U_BQK, B_QK), blk(U_BV, BR_WIDTH), blk(U_BO, BR_WIDTH), blk(U_BZ, BR_WIDTH),
                  blk(U_GATES, GATE_PAD),
                  pl.BlockSpec((None, 2 * N_HEADS, ln), lambda g, c: (g * nc + c, 0, 0)),
                  const((CONV_W, B_QK)), const((1, B_QK)), const((1, GATE_PAD)), const((2 * N_HEADS, 1)),
                  const((1, BR_WIDTH))] + state_specs + [per_seq((CONV_W - 1, B_QK))],
        out_specs=[pl.BlockSpec((ln, BR_WIDTH), lambda g, c: (g * nc + c, 0))] + state_specs,
        scratch_shapes=[pltpu.VMEM((CONV_PAD + ln, B_QK), f32),
                        pltpu.VMEM((N_HEADS, HEAD_DIM, HEAD_DIM), f32),
                        pltpu.VMEM((N_HEADS, 1, HEAD_DIM), f32),
                        pltpu.VMEM((N_HEADS, 1, HEAD_DIM), f32)],
        compiler_params=_params("parallel", "arbitrary"),
        name="mlstm",
    )(u, u, u, u, u, gates_r, conv_w, conv_b.reshape(1, B_QK), gbr, gbc, gn.reshape(1, BR_WIDTH),
      c0, n0.reshape(ng, N_HEADS, 1, HEAD_DIM),
      jnp.broadcast_to(m0[:, :, None, None], (ng, N_HEADS, 1, HEAD_DIM)), conv0)
    return out, c_new, n_new[:, :, 0, :], m_new[:, :, 0, 0]


def _ret_body(q_ref, k_ref, v_ref, z_ref, cc_ref, ss_ref, gn_ref, s0_ref, out_ref, s_out_ref, s_scr,
              *, chunk_axis=1):
    ln = q_ref.shape[0]
    on_first, on_last = _first_last(chunk_axis)

    @on_first
    def _():
        s_scr[...] = s0_ref[...]

    row = lax.broadcasted_iota(jnp.int32, (ln, ln), 0)
    col = lax.broadcasted_iota(jnp.int32, (ln, ln), 1)
    rel = (row - col).astype(f32)
    tpos = lax.broadcasted_iota(jnp.int32, (ln, 1), 0).astype(f32)
    cc = cc_ref[...]
    ss = ss_ref[...]
    for h in range(N_HEADS):
        sl = slice(h * HEAD_DIM, (h + 1) * HEAD_DIM)
        lg = math.log(1.0 - 2.0 ** (-RET_GAMMA_EXP0 - h))
        q = q_ref[:, sl]
        k = k_ref[:, sl]
        qr = q * cc + pltpu.roll(q, DIFF_DH, 1) * ss
        kr = (k * cc + pltpu.roll(k, DIFF_DH, 1) * ss) * HEAD_SCALE
        vb = v_ref[:, sl].astype(bf16)
        decay = jnp.where(rel >= 0.0, jnp.exp(jnp.maximum(rel, 0.0) * lg), 0.0)
        att = lax.dot_general(qr.astype(bf16), kr.astype(bf16), NT_DIMS, preferred_element_type=f32) * decay
        q_dec = qr * jnp.exp((tpos + 1.0) * lg)
        s_old = s_scr[h]
        o = (jnp.dot(att.astype(bf16), vb, preferred_element_type=f32)
             + jnp.dot(q_dec.astype(bf16), s_old.astype(bf16), preferred_element_type=f32))
        k_dec = kr * jnp.exp((ln - 1.0 - tpos) * lg)
        s_scr[h] = math.exp(ln * lg) * s_old + lax.dot_general(k_dec.astype(bf16), vb, TN_DIMS,
                                                               preferred_element_type=f32)
        out_ref[:, sl] = (_head_norm(o, gn_ref[:, sl]) * _silu(z_ref[:, sl])).astype(bf16)

    @on_last
    def _():
        s_out_ref[...] = s_scr[...]


def _rope_tables(pos):
    inv = ROPE_BASE ** (-jnp.arange(0, HEAD_DIM, 2, dtype=f32) / HEAD_DIM)
    ang = pos.astype(f32)[:, None] * inv[None, :]
    cos, sin = jnp.cos(ang), jnp.sin(ang)
    return jnp.concatenate([cos, cos], axis=-1), jnp.concatenate([-sin, sin], axis=-1)


def _retention(u, pos, gn, s0, row0, ng, nc, ln):
    rb0 = row0 // ln
    cc, ss = _rope_tables(pos)

    def blk(col):
        return pl.BlockSpec((ln, BR_WIDTH), lambda g, c: (rb0 + g * nc + c, col // BR_WIDTH))

    rope_spec = pl.BlockSpec((ln, HEAD_DIM), lambda g, c: (c, 0))
    state_spec = pl.BlockSpec((None, N_HEADS, HEAD_DIM, HEAD_DIM), lambda g, c: (g, 0, 0, 0))
    return pl.pallas_call(
        _ret_body,
        out_shape=[jax.ShapeDtypeStruct((ng * nc * ln, BR_WIDTH), bf16),
                   jax.ShapeDtypeStruct((ng, N_HEADS, HEAD_DIM, HEAD_DIM), f32)],
        grid=(ng, nc),
        in_specs=[blk(U_DQ), blk(U_DK), blk(U_DV), blk(U_DZ), rope_spec, rope_spec,
                  pl.BlockSpec((1, BR_WIDTH), lambda g, c: (0, 0)), state_spec],
        out_specs=[pl.BlockSpec((ln, BR_WIDTH), lambda g, c: (g * nc + c, 0)), state_spec],
        scratch_shapes=[pltpu.VMEM((N_HEADS, HEAD_DIM, HEAD_DIM), f32)],
        compiler_params=_params("parallel", "arbitrary"),
        name="retention",
    )(u, u, u, u, cc, ss, gn.reshape(1, BR_WIDTH), s0)


def _sample_branches_body(*refs, lam_init):
    it = iter(refs)

    def take(n):
        return [next(it) for _ in range(n)]

    a_in, c_in, m_in, b_in, d_in = take(8), take(8), take(4), take(15), take(8)
    a_out, c_out, m_out, b_out, d_out = take(3), take(3), take(1), take(4), take(2)
    b_scr, d_scr = take(4), take(1)
    _band_sample_body(*a_in, *a_out)
    _diff_sample_body(*c_in, *c_out, lam_init=lam_init)
    _mem_body(*m_in, *m_out, interleaved=True)
    _mlstm_body(*b_in, *b_out, *b_scr, chunk_axis=None)
    _ret_body(*d_in, *d_out, *d_scr, chunk_axis=None)


def _sample_branches(u, layer, caches, rel_bias, lamp, lam_init, gains, conv_w, conv_b, b_ig, b_fg,
                     states, pos, nb, t):
    ca_k, ca_v, cc_k, cc_v, cm_k, cm_v = caches
    subln_c, gn_b, gn_d = gains
    c0, n0, m0, conv0, s0 = states
    nrow = ca_k.shape[2] // N_HEADS
    bias_c = _rel_bias_toeplitz(rel_bias, t, nrow, nrow)
    bias_n = _rel_bias_toeplitz(rel_bias, t, t, 0)
    gbias = jnp.concatenate([b_ig, b_fg]).astype(f32)
    gbr = jnp.zeros((1, GATE_PAD), f32).at[0, :2 * N_HEADS].set(gbias)
    cc, ss = _rope_tables(pos)

    def blk(col, width=BR_WIDTH):
        return pl.BlockSpec((t, width), lambda b: (b, col // width))

    def const(shape):
        return pl.BlockSpec(shape, lambda b: (0,) * len(shape))

    def per_seq(shape):
        return pl.BlockSpec((None,) + shape, lambda b: (b,) + (0,) * len(shape))

    def cache(arr):
        return pl.BlockSpec((None, None) + arr.shape[2:], lambda b: (layer, b, 0, 0))

    mat, vec = (N_HEADS, HEAD_DIM, HEAD_DIM), (N_HEADS, 1, HEAD_DIM)
    kv_state = (t * N_HEADS, HEAD_DIM)
    in_specs = (
        [blk(U_AQ), blk(U_AK), blk(U_AV), blk(U_AZ), cache(ca_k), cache(ca_v),
         const((N_HEADS, t, nrow)), const((N_HEADS, t, t))]
        + [blk(U_CQ), blk(U_CK), blk(U_CV), blk(U_CZ), cache(cc_k), cache(cc_v),
           const((4, DIFF_DH)), const((1, BR_WIDTH))]
        + [blk(U_MQ), blk(U_MZ), cache(cm_k), cache(cm_v)]
        + [blk(U_BQK, B_QK), blk(U_BV), blk(U_BO), blk(U_BZ), blk(U_GATES, GATE_PAD),
           per_seq((2 * N_HEADS, t)), const((CONV_W, B_QK)), const((1, B_QK)), const((1, GATE_PAD)),
           const((2 * N_HEADS, 1)), const((1, BR_WIDTH)), per_seq(mat), per_seq(vec), per_seq(vec),
           per_seq((CONV_W - 1, B_QK))]
        + [blk(U_DQ), blk(U_DK), blk(U_DV), blk(U_DZ), const((t, HEAD_DIM)), const((t, HEAD_DIM)),
           const((1, BR_WIDTH)), per_seq(mat)])
    args = (
        [u, u, u, u, ca_k, ca_v, bias_c, bias_n]
        + [u, u, u, u, cc_k, cc_v, lamp, subln_c.reshape(1, BR_WIDTH)]
        + [u, u, cm_k, cm_v]
        + [u, u, u, u, u, _gates_rowform(u, 0, nb * t, t), conv_w, conv_b.reshape(1, B_QK), gbr,
           gbias.reshape(2 * N_HEADS, 1), gn_b.reshape(1, BR_WIDTH), c0,
           n0.reshape(nb, N_HEADS, 1, HEAD_DIM),
           jnp.broadcast_to(m0[:, :, None, None], (nb, N_HEADS, 1, HEAD_DIM)), conv0]
        + [u, u, u, u, cc, ss, gn_d.reshape(1, BR_WIDTH), s0])
    branch_out = jax.ShapeDtypeStruct((nb * t, BR_WIDTH), bf16)
    kv_out = jax.ShapeDtypeStruct((nb,) + kv_state, f32)
    mat_out = jax.ShapeDtypeStruct((nb,) + mat, f32)
    vec_out = jax.ShapeDtypeStruct((nb,) + vec, f32)
    out_shape = ([branch_out, kv_out, kv_out] * 2 + [branch_out]
                 + [branch_out, mat_out, vec_out, vec_out] + [branch_out, mat_out])
    o_spec = pl.BlockSpec((t, BR_WIDTH), lambda b: (b, 0))
    out_specs = ([o_spec, per_seq(kv_state), per_seq(kv_state)] * 2 + [o_spec]
                 + [o_spec, per_seq(mat), per_seq(vec), per_seq(vec)] + [o_spec, per_seq(mat)])
    (oa, ak, av, oc, ck, cv, om, ob, c_new, n_new, m_new, od, s_new) = pl.pallas_call(
        functools.partial(_sample_branches_body, lam_init=lam_init),
        out_shape=out_shape,
        grid=(nb,),
        in_specs=in_specs,
        out_specs=out_specs,
        scratch_shapes=[pltpu.VMEM((CONV_PAD + t, B_QK), f32), pltpu.VMEM(mat, f32),
                        pltpu.VMEM(vec, f32), pltpu.VMEM(vec, f32), pltpu.VMEM(mat, f32)],
        compiler_params=_params("arbitrary"),
        name="sample_branches",
    )(*args)
    return ((oa, ob, oc, od, om), (ak, av, ck, cv),
            (c_new, n_new[:, :, 0, :], m_new[:, :, 0, 0], s_new))


def _merge_body(x_ref, g_ref, oa_ref, ob_ref, oc_ref, od_ref, om_ref,
                wg0_ref, wg1_ref, wg2_ref, wg3_ref, wg4_ref, wb_ref, wo_ref, fg_ref,
                y_ref, h_scr, mrg_scr, *, rows, final):
    n = pl.program_id(1)
    n_tiles = pl.num_programs(1) - 1

    @pl.when(n == 0)
    def _():
        _rmsnorm_rows(x_ref, g_ref, h_scr, rows)
        y_ref[...] = jnp.zeros(y_ref.shape, f32)
        mrg_scr[...] = jnp.zeros(mrg_scr.shape, bf16)

    @pl.when(n < n_tiles)
    def _():
        y_ref[...] += jnp.dot(mrg_scr[...], wo_ref[...], preferred_element_type=f32)
        h = h_scr[...]
        merged = None
        for i, (o_ref, wg_ref) in enumerate(zip((oa_ref, ob_ref, oc_ref, od_ref, om_ref),
                                                (wg0_ref, wg1_ref, wg2_ref, wg3_ref, wg4_ref))):
            gate = _sigmoid(lax.dot_general(h, wg_ref[...], NT_DIMS, preferred_element_type=f32))
            term = gate * jnp.dot(o_ref[...], wb_ref[i], preferred_element_type=f32)
            merged = term if merged is None else merged + term
        mrg_scr[...] = merged.astype(bf16)

    @pl.when(n == n_tiles)
    def _():
        y_ref[...] += jnp.dot(mrg_scr[...], wo_ref[...], preferred_element_type=f32)
        tm = x_ref.shape[0]
        for r in range(0, tm, rows):
            y = x_ref[r:r + rows, :] + y_ref[r:r + rows, :]
            if final:
                y = (y * lax.rsqrt(jnp.mean(y * y, axis=-1, keepdims=True) + EPS)) * fg_ref[...]
            y_ref[r:r + rows, :] = y


def _merge(x, g, outs, wg, wb, wo, fg, *, final, name, tm_target=704, tn=256):
    m, d = x.shape
    tm = _divisor_tile(m, tm_target, 16)
    rows = _divisor_tile(tm, 256, 8)
    nn = d // tn

    def cur(n):
        return jnp.minimum(n, nn - 1)

    def wg_spec(i):
        return pl.BlockSpec((tn, d), lambda r, n: (i * nn + cur(n), 0))

    o_spec = pl.BlockSpec((tm, BR_WIDTH), lambda r, n: (r, 0))
    return pl.pallas_call(
        functools.partial(_merge_body, rows=rows, final=final),
        out_shape=jax.ShapeDtypeStruct((m, d), f32),
        grid=(m // tm, nn + 1),
        in_specs=[pl.BlockSpec((tm, d), lambda r, n: (r, 0)),
                  pl.BlockSpec((1, d), lambda r, n: (0, 0))]
                 + [o_spec] * N_BRANCH
                 + [wg_spec(i) for i in range(N_BRANCH)]
                 + [pl.BlockSpec((N_BRANCH, BR_WIDTH, tn), lambda r, n: (0, 0, cur(n))),
                    pl.BlockSpec((tn, d), lambda r, n: (jnp.maximum(n - 1, 0), 0)),
                    pl.BlockSpec((1, d), lambda r, n: (0, 0))],
        out_specs=pl.BlockSpec((tm, d), lambda r, n: (r, 0)),
        scratch_shapes=[pltpu.VMEM((tm, d), bf16), pltpu.VMEM((tm, tn), bf16)],
        compiler_params=_params("parallel", "arbitrary"),
        name=name,
    )(x, g.reshape(1, d), *outs, wg, wg, wg, wg, wg, wb, wo, fg.reshape(1, d))


F32_SUBLANES = 8


def _cast_rows_body(src_ref, o_ref, *, valid_rows_last):
    tr = o_ref.shape[0]
    x = src_ref[0]
    if valid_rows_last is not None:
        row = lax.broadcasted_iota(jnp.int32, (tr, 1), 0)
        keep = jnp.logical_or(pl.program_id(0) < pl.num_programs(0) - 1, row < valid_rows_last)
        x = jnp.where(keep, x, 0.0)
    o_ref[...] = x.astype(bf16)


def _cast_rows(w_t, layer, *, n_rows, tr, src_row, valid_rows_last, name):
    d = w_t.shape[2]
    assert n_rows % tr == 0
    return pl.pallas_call(
        functools.partial(_cast_rows_body, valid_rows_last=valid_rows_last),
        out_shape=jax.ShapeDtypeStruct((n_rows, d), bf16),
        grid=(n_rows // tr,),
        in_specs=[pl.BlockSpec((pl.Element(1), pl.Element(tr), pl.Element(d)),
                               lambda j: (layer, pl.multiple_of(src_row(j), F32_SUBLANES), 0))],
        out_specs=pl.BlockSpec((tr, d), lambda j: (j, 0)),
        compiler_params=_params("arbitrary"),
        name=name,
    )(w_t)


def _repack_w_in(w_t, layer):
    tr = MXU_COLS
    n_plain, n_main = W_BI // tr, U_GATES // tr

    def u_src(j):
        return jnp.where(j < n_plain, j * tr, jnp.where(j < n_main, j * tr + (W_BZ - W_BI), W_BI))

    wu = _cast_rows(w_t, layer, n_rows=U_WIDTH, tr=tr, src_row=u_src, valid_rows_last=2 * N_HEADS,
                    name="repack_u")
    wg = _cast_rows(w_t, layer, n_rows=N_BRANCH * D_MODEL, tr=512, src_row=lambda j: W_GATE + j * 512,
                    valid_rows_last=None, name="repack_g")
    return wu, wg


def _gates_rowform(u, row0, nrows, ln):
    g = u[row0:row0 + nrows, U_GATES:U_GATES + 2 * N_HEADS]
    return g.reshape(nrows // ln, ln, 2 * N_HEADS).transpose(0, 2, 1)


def kernel(x_prompt, x_sample, mem_prompt, cache_a_k, cache_a_v, cache_c_k, cache_c_v, cache_mem_k, cache_mem_v, state_b_C, state_b_n, state_b_m, state_b_conv, state_d_S, norm_g, w_in, conv_w, conv_b, b_ig, b_fg, rel_bias, lam_q1, lam_k1, lam_q2, lam_k2, gn_b, subln_c, gn_d, mem_norm_g, w_mk, w_mv, w_branch, w_out, final_g):
    batch, seq, d = x_prompt.shape
    nb, t, _ = x_sample.shape
    depth = w_in.shape[0]
    n_mem = mem_prompt.shape[1]
    past = cache_c_k.shape[2]
    mp = batch * seq
    assert d == D_MODEL and seq % C_BLOCK == 0 and seq % A_QBLOCK == 0 and seq % SCAN_CHUNK == 0

    xp = x_prompt.reshape(mp, d)
    xs = x_sample.reshape(nb * t, d)
    mem = mem_prompt.reshape(batch * n_mem, d)
    ca_k, ca_v = _cache_rows(cache_a_k), _cache_rows(cache_a_v)
    cc_k, cc_v = _cache_rows(cache_c_k), _cache_rows(cache_c_v)
    cm_k, cm_v = _cache_rows(cache_mem_k), _cache_rows(cache_mem_v)
    pos_p = jnp.arange(seq)
    pos_s = past + jnp.arange(t)
    a_rows = min(A_WINDOW, seq)
    mem_tq = _divisor_tile(seq, 512, 16)

    w_t = jnp.swapaxes(w_in, 1, 2)

    sp, ss = [], []
    for l in range(depth):
        lam_init = 0.8 - 0.6 * math.exp(-0.3 * l)
        lamp = jnp.stack([lam_q1[l], lam_k1[l], lam_q2[l], lam_k2[l]]).astype(f32)
        wu, wg = _repack_w_in(w_t, l)
        u = _norm_matmul(xp, norm_g[l], wu, tm_target=1024, tn=U_TILE, w_rows_are_outputs=True,
                         name="in_proj")
        us = _norm_matmul(xs, norm_g[l], wu, tm_target=1024, tn=U_TILE, w_rows_are_outputs=True,
                          name="in_proj_sample")
        mkv = _norm_matmul(mem, mem_norm_g[l], jnp.concatenate([w_mk[l], w_mv[l]], axis=1).astype(bf16),
                           tm_target=512, tn=512, w_rows_are_outputs=False, name="mem_proj")

        oa_p, ak_p, av_p = _band_prompt(u, rel_bias[l], batch, seq, a_rows)
        zeros_c = jnp.zeros((batch, N_HEADS, HEAD_DIM, HEAD_DIM), f32)
        ob_p, c_p, n_p, m_p = _mlstm(
            u, _gates_rowform(u, 0, mp, SCAN_CHUNK), conv_w[l], conv_b[l], b_ig[l], b_fg[l], gn_b[l],
            zeros_c, jnp.zeros((batch, N_HEADS, HEAD_DIM), f32), jnp.zeros((batch, N_HEADS), f32),
            jnp.zeros((batch, CONV_W - 1, B_QK), f32), 0, batch, seq // SCAN_CHUNK, SCAN_CHUNK)
        oc_p, ck_p, cv_p = _diff_prompt(u, lamp, subln_c[l], lam_init, batch, seq)
        od_p, s_p = _retention(u, pos_p, gn_d[l], zeros_c, 0, batch, seq // SCAN_CHUNK, SCAN_CHUNK)
        mem_spec_k = pl.BlockSpec((n_mem, BR_WIDTH), lambda b, i: (b, 0))
        mem_spec_v = pl.BlockSpec((n_mem, BR_WIDTH), lambda b, i: (b, 1))
        om_p = _mem_attn(u, 0, batch, seq, mem_tq, mkv, mkv, mem_spec_k, mem_spec_v, False, "mem_attn_prompt")

        outs_s, (ak_s, av_s, ck_s, cv_s), (c_s, n_s, m_s, s_s) = _sample_branches(
            us, l, (ca_k, ca_v, cc_k, cc_v, cm_k, cm_v), rel_bias[l], lamp, lam_init,
            (subln_c[l], gn_b[l], gn_d[l]), conv_w[l], conv_b[l], b_ig[l], b_fg[l],
            (state_b_C[l].astype(f32), state_b_n[l].astype(f32), state_b_m[l].astype(f32),
             state_b_conv[l].astype(f32), state_d_S[l].astype(f32)), pos_s, nb, t)

        wb, wo = w_branch[l].astype(bf16), w_out[l].astype(bf16)
        final = l == depth - 1
        xp = _merge(xp, norm_g[l], (oa_p, ob_p, oc_p, od_p, om_p), wg, wb, wo, final_g, final=final,
                    name="gated_merge")
        xs = _merge(xs, norm_g[l], outs_s, wg, wb, wo, final_g, final=final, name="gated_merge_sample")

        def conv_tail(arr, n_seq, rows):
            return jnp.stack([arr[(i + 1) * rows - (CONV_W - 1):(i + 1) * rows, U_BQK:U_BQK + B_QK]
                              for i in range(n_seq)])

        def hd(a):
            return a.reshape(a.shape[:2] + (N_HEADS, HEAD_DIM))

        def rows_hd(a):
            return a.reshape(a.shape[0], a.shape[1] // N_HEADS, N_HEADS, HEAD_DIM)

        sp.append((rows_hd(ak_p), rows_hd(av_p), rows_hd(ck_p), rows_hd(cv_p),
                   hd(mkv[:, :BR_WIDTH].reshape(batch, n_mem, BR_WIDTH)),
                   hd(mkv[:, BR_WIDTH:].reshape(batch, n_mem, BR_WIDTH)),
                   c_p, n_p, m_p, conv_tail(u, batch, seq), s_p))
        ss.append((rows_hd(ak_s), rows_hd(av_s), rows_hd(ck_s), rows_hd(cv_s),
                   c_s, n_s, m_s, us[:, U_BQK:U_BQK + B_QK].reshape(nb, t, B_QK)[:, t - (CONV_W - 1):], s_s))

    y_prompt = xp.reshape(batch, seq, d)
    y_sample = xs.reshape(nb, t, d)
    p_states = tuple(jnp.stack([st[i] for st in sp]) for i in range(11))
    s_states = tuple(jnp.stack([st[i] for st in ss]) for i in range(9))
    return (y_prompt, y_sample) + p_states + s_states
```

```python
import functools
import math

import jax
import jax.numpy as jnp
import numpy as np
from jax import lax
from jax.experimental import pallas as pl
from jax.experimental.pallas import tpu as pltpu

f32 = jnp.float32
bf16 = jnp.bfloat16

D_MODEL = 2048
N_HEADS = 4
HEAD_DIM = 128
BR_WIDTH = N_HEADS * HEAD_DIM
N_BRANCH = 5
CHUNK = 64
A_PREV_CHUNKS = 8
A_WINDOW = A_PREV_CHUNKS * CHUNK
A_BAND = (A_PREV_CHUNKS + 1) * CHUNK
A_REL_CLIP = 128
CONV_W = 4
B_QK = 2 * BR_WIDTH
DIFF_DH = HEAD_DIM // 2
RET_GAMMA_EXP0 = 5.0
ROPE_BASE = 10000.0
EPS = 1e-6
NEG = -1e30
HEAD_SCALE = HEAD_DIM ** -0.5
DIFF_SCALE = DIFF_DH ** -0.5

U_AQ, U_AK, U_AV, U_AZ = 0, 512, 1024, 1536
U_BQK, U_BV, U_BO, U_BZ = 2048, 3072, 3584, 4096
U_CQ, U_CK, U_CV, U_CZ = 4608, 5120, 5632, 6144
U_DQ, U_DK, U_DV, U_DZ = 6656, 7168, 7680, 8192
U_MQ, U_MZ = 8704, 9216
U_GATES = 9728
GATE_PAD = 128
MXU_COLS = 256
U_WIDTH = U_GATES + MXU_COLS
U_TILE = 3 * MXU_COLS
W_BI = 4096
W_BZ = 4104
W_GATE = 9736

VMEM_LIMIT_BYTES = 56 * 1024 * 1024
A_QBLOCK = 4 * CHUNK
C_BLOCK = 512
SCAN_CHUNK = 256
NT_DIMS = (((1,), (1,)), ((), ()))
TN_DIMS = (((0,), (0,)), ((), ()))


def _divisor_tile(n, target, multiple):
    best = None
    for t in range(multiple, min(n, target) + 1, multiple):
        if n % t == 0:
            best = t
    assert best is not None, (n, target, multiple)
    return best


def _params(*sem):
    return pltpu.CompilerParams(dimension_semantics=sem, vmem_limit_bytes=VMEM_LIMIT_BYTES)


def _sigmoid(x):
    return 1.0 / (1.0 + jnp.exp(-x))


def _silu(x):
    return x * _sigmoid(x)


def _log_sigmoid(x):
    return jnp.minimum(x, 0.0) - jnp.log1p(jnp.exp(-jnp.abs(x)))


def _head_norm(y, gain):
    return y * lax.rsqrt(jnp.mean(y * y, axis=-1, keepdims=True) + EPS) * gain


def _rmsnorm_rows(x_ref, g_ref, h_ref, rows):
    tm = x_ref.shape[0]
    for r in range(0, tm, rows):
        x = x_ref[r:r + rows, :]
        ms = jnp.mean(x * x, axis=-1, keepdims=True)
        h_ref[r:r + rows, :] = ((x * lax.rsqrt(ms + EPS)) * g_ref[...]).astype(bf16)


def _norm_matmul_body(x_ref, g_ref, w_ref, o_ref, h_ref, *, rows, w_rows_are_outputs):
    @pl.when(pl.program_id(1) == 0)
    def _():
        _rmsnorm_rows(x_ref, g_ref, h_ref, rows)

    if w_rows_are_outputs:
        o = lax.dot_general(h_ref[...], w_ref[...], NT_DIMS, preferred_element_type=f32)
    else:
        o = jnp.dot(h_ref[...], w_ref[...], preferred_element_type=f32)
    o_ref[...] = o.astype(o_ref.dtype)


def _norm_matmul(x, g, w, *, tm_target, tn, w_rows_are_outputs, name):
    m, d = x.shape
    n = w.shape[0] if w_rows_are_outputs else w.shape[1]
    tm = _divisor_tile(m, tm_target, 16)
    rows = _divisor_tile(tm, 256, 8)
    assert n % tn == 0
    w_spec = (pl.BlockSpec((tn, d), lambda i, j: (j, 0)) if w_rows_are_outputs
              else pl.BlockSpec((d, tn), lambda i, j: (0, j)))
    return pl.pallas_call(
        functools.partial(_norm_matmul_body, rows=rows, w_rows_are_outputs=w_rows_are_outputs),
        out_shape=jax.ShapeDtypeStruct((m, n), f32),
        grid=(m // tm, n // tn),
        in_specs=[pl.BlockSpec((tm, d), lambda i, j: (i, 0)),
                  pl.BlockSpec((1, d), lambda i, j: (0, 0)),
                  w_spec],
        out_specs=pl.BlockSpec((tm, tn), lambda i, j: (i, j)),
        scratch_shapes=[pltpu.VMEM((tm, d), bf16)],
        compiler_params=_params("parallel", "arbitrary"),
        name=name,
    )(x, g.reshape(1, d), w)


def _store_head_rows(dst_ref, src_ref):
    rows = src_ref.shape[0]
    for h in range(N_HEADS):
        dst_ref[pl.ds(h, rows, stride=N_HEADS), :] = src_ref[:, h * HEAD_DIM:(h + 1) * HEAD_DIM]


def _band_prompt_body(q_ref, k0_ref, k1_ref, k2_ref, v0_ref, v1_ref, v2_ref, z_ref, bias_ref,
                      o_ref, ks_ref, vs_ref, *, n_tail):
    tq = q_ref.shape[0]
    i = pl.program_id(1)

    @pl.when(i >= pl.num_programs(1) - n_tail)
    def _():
        _store_head_rows(ks_ref, k2_ref)
        _store_head_rows(vs_ref, v2_ref)

    w_idx = lax.broadcasted_iota(jnp.int32, (tq, 3 * tq), 1)
    valid = (w_idx + (i - 2) * tq) >= 0
    for h in range(N_HEADS):
        sl = slice(h * HEAD_DIM, (h + 1) * HEAD_DIM)
        q = q_ref[:, sl].astype(bf16)
        kw = jnp.concatenate([k0_ref[:, sl], k1_ref[:, sl], k2_ref[:, sl]], axis=0).astype(bf16)
        vw = jnp.concatenate([v0_ref[:, sl], v1_ref[:, sl], v2_ref[:, sl]], axis=0).astype(bf16)
        s = lax.dot_general(q, kw, NT_DIMS, preferred_element_type=f32) * HEAD_SCALE + bias_ref[h]
        s = jnp.where(valid, s, NEG)
        p = jnp.exp(s - jnp.max(s, axis=-1, keepdims=True))
        l = jnp.sum(p, axis=-1, keepdims=True)
        o = jnp.dot(p.astype(bf16), vw, preferred_element_type=f32) / l
        o_ref[:, sl] = (o * _silu(z_ref[:, sl])).astype(bf16)


def _rel_bias_toeplitz(rel_bias, rows, cols, rel0):
    p = rows + cols
    j = np.arange(p)
    d = np.where(j < cols, j, j - p)
    idx = np.clip(rel0 - d, -A_REL_CLIP, A_REL_CLIP) + A_REL_CLIP
    v = rel_bias.astype(f32)[:, idx]
    flat = jnp.tile(v, (1, rows))[:, :rows * (p - 1)]
    return flat.reshape(rel_bias.shape[0], rows, p - 1)[:, :, :cols]


def _band_bias_prompt(rel_bias):
    r = np.arange(A_QBLOCK)[:, None]
    w = np.arange(3 * A_QBLOCK)[None, :]
    kj = w - CHUNK * (r // CHUNK)
    inside = (kj >= 0) & (kj < A_BAND)
    table = _rel_bias_toeplitz(rel_bias, A_QBLOCK, 3 * A_QBLOCK, 2 * A_QBLOCK)
    return jnp.where(jnp.asarray(inside)[None], table, NEG)


def _band_prompt(u, rel_bias, batch, seq, a_rows):
    tq = A_QBLOCK
    nqb = seq // tq
    assert a_rows % tq == 0
    n_tail = a_rows // tq
    bias = _band_bias_prompt(rel_bias)

    def blk(col, back):
        return pl.BlockSpec((tq, BR_WIDTH),
                            lambda b, i: (b * nqb + jnp.maximum(i - back, 0), col // BR_WIDTH))

    state_shape = jax.ShapeDtypeStruct((batch, a_rows * N_HEADS, HEAD_DIM), f32)
    state_spec = pl.BlockSpec((None, tq * N_HEADS, HEAD_DIM),
                              lambda b, i: (b, jnp.maximum(i - (nqb - n_tail), 0), 0))
    return pl.pallas_call(
        functools.partial(_band_prompt_body, n_tail=n_tail),
        out_shape=[jax.ShapeDtypeStruct((batch * seq, BR_WIDTH), bf16), state_shape, state_shape],
        grid=(batch, nqb),
        in_specs=[blk(U_AQ, 0),
                  blk(U_AK, 2), blk(U_AK, 1), blk(U_AK, 0),
                  blk(U_AV, 2), blk(U_AV, 1), blk(U_AV, 0),
                  blk(U_AZ, 0),
                  pl.BlockSpec((N_HEADS, tq, 3 * tq), lambda b, i: (0, 0, 0))],
        out_specs=[pl.BlockSpec((tq, BR_WIDTH), lambda b, i: (b * nqb + i, 0)), state_spec, state_spec],
        compiler_params=_params("parallel", "arbitrary"),
        name="band_attn_prompt",
    )(u, u, u, u, u, u, u, u, bias)


def _band_sample_body(q_ref, k_ref, v_ref, z_ref, ck_ref, cv_ref, bc_ref, bn_ref, o_ref, ks_ref, vs_ref):
    _store_head_rows(ks_ref, k_ref)
    _store_head_rows(vs_ref, v_ref)
    for h in range(N_HEADS):
        sl = slice(h * HEAD_DIM, (h + 1) * HEAD_DIM)
        q = q_ref[:, sl].astype(bf16)
        s_c = lax.dot_general(q, _head_rows(ck_ref, h, True).astype(bf16), NT_DIMS,
                              preferred_element_type=f32) * HEAD_SCALE + bc_ref[h]
        s_n = lax.dot_general(q, k_ref[:, sl].astype(bf16), NT_DIMS,
                              preferred_element_type=f32) * HEAD_SCALE + bn_ref[h]
        m = jnp.maximum(jnp.max(s_c, axis=-1, keepdims=True), jnp.max(s_n, axis=-1, keepdims=True))
        p_c = jnp.exp(s_c - m)
        p_n = jnp.exp(s_n - m)
        l = jnp.sum(p_c, axis=-1, keepdims=True) + jnp.sum(p_n, axis=-1, keepdims=True)
        o = (jnp.dot(p_c.astype(bf16), _head_rows(cv_ref, h, True).astype(bf16), preferred_element_type=f32)
             + jnp.dot(p_n.astype(bf16), v_ref[:, sl].astype(bf16), preferred_element_type=f32)) / l
        o_ref[:, sl] = (o * _silu(z_ref[:, sl])).astype(bf16)


def _head_rows(ref, h, interleaved):
    if interleaved:
        return ref[pl.ds(h, ref.shape[0] // N_HEADS, stride=N_HEADS), :]
    return ref[:, h * HEAD_DIM:(h + 1) * HEAD_DIM]


def _cache_rows(cache):
    d0, d1, rows = cache.shape[:3]
    return cache.reshape(d0, d1, rows * N_HEADS, HEAD_DIM)


def _mem_body(q_ref, z_ref, k_ref, v_ref, o_ref, *, interleaved):
    for h in range(N_HEADS):
        sl = slice(h * HEAD_DIM, (h + 1) * HEAD_DIM)
        q = q_ref[:, sl].astype(bf16)
        k = _head_rows(k_ref, h, interleaved).astype(bf16)
        v = _head_rows(v_ref, h, interleaved).astype(bf16)
        s = lax.dot_general(q, k, NT_DIMS, preferred_element_type=f32) * HEAD_SCALE
        p = jnp.exp(s - jnp.max(s, axis=-1, keepdims=True))
        l = jnp.sum(p, axis=-1, keepdims=True)
        o = jnp.dot(p.astype(bf16), v, preferred_element_type=f32) / l
        o_ref[:, sl] = (o * _silu(z_ref[:, sl])).astype(bf16)


def _mem_attn(u, row0, nb, t, tq, k_arr, v_arr, k_spec, v_spec, interleaved, name):
    nq = t // tq
    rb0 = row0 // tq

    def blk(col):
        return pl.BlockSpec((tq, BR_WIDTH), lambda b, i: (rb0 + b * nq + i, col // BR_WIDTH))

    return pl.pallas_call(
        functools.partial(_mem_body, interleaved=interleaved),
        out_shape=jax.ShapeDtypeStruct((nb * t, BR_WIDTH), bf16),
        grid=(nb, nq),
        in_specs=[blk(U_MQ), blk(U_MZ), k_spec, v_spec],
        out_specs=pl.BlockSpec((tq, BR_WIDTH), lambda b, i: (b * nq + i, 0)),
        compiler_params=_params("parallel", "arbitrary"),
        name=name,
    )(u, u, k_arr, v_arr)


def _diff_lambda(lamp_ref, lam_init):
    lp = lamp_ref[...]
    a = jnp.sum(lp[0:1] * lp[1:2], axis=-1, keepdims=True)
    b = jnp.sum(lp[2:3] * lp[3:4], axis=-1, keepdims=True)
    return jnp.exp(a) - jnp.exp(b) + lam_init


def _diff_epilogue(o0, o1, lam, gain, z, lam_init):
    y = _head_norm(o0 - lam * o1, gain) * (1.0 - lam_init)
    return (y * _silu(z)).astype(bf16)


def _half_masks():
    lane = lax.broadcasted_iota(jnp.int32, (1, HEAD_DIM), 1)
    return lane < DIFF_DH, lane >= DIFF_DH


def _diff_prompt_body(qt_ref, kt_ref, q_ref, k_ref, v_ref, z_ref, lamp_ref, gain_ref, o_ref, ks_ref, vs_ref,
                      m_scr, l_scr, acc_scr, *, lam_init):
    tq, tk = q_ref.shape[0], k_ref.shape[0]
    i = qt_ref[pl.program_id(1)]
    j = kt_ref[pl.program_id(1)]

    @pl.when(j == 0)
    def _():
        m_scr[...] = jnp.full(m_scr.shape, NEG, f32)
        l_scr[...] = jnp.zeros(l_scr.shape, f32)
        acc_scr[...] = jnp.zeros(acc_scr.shape, f32)

    lo, hi = _half_masks()

    def step(diagonal):
        if diagonal:
            kc = lax.broadcasted_iota(jnp.int32, (tk, 2 * tq), 0) // CHUNK
            qpos = lax.broadcasted_iota(jnp.int32, (tk, 2 * tq), 1)
            qc = jnp.where(qpos >= tq, qpos - tq, qpos) // CHUNK
            mask = kc <= qc
        for h in range(N_HEADS):
            sl = slice(h * HEAD_DIM, (h + 1) * HEAD_DIM)
            q = q_ref[:, sl] * DIFF_SCALE
            qq = jnp.concatenate([jnp.where(lo, q, 0.0), jnp.where(hi, q, 0.0)], axis=0).astype(bf16)
            k = k_ref[:, sl].astype(bf16)
            vt = v_ref[:, sl].T.astype(bf16)
            s = lax.dot_general(k, qq, NT_DIMS, preferred_element_type=f32)
            if diagonal:
                s = jnp.where(mask, s, NEG)
            m_old = m_scr[h]
            m_new = jnp.maximum(m_old, jnp.max(s, axis=0, keepdims=True))
            p = jnp.exp(s - m_new)
            alpha = jnp.exp(m_old - m_new)
            l_scr[h] = alpha * l_scr[h] + jnp.sum(p, axis=0, keepdims=True)
            acc_scr[h] = alpha * acc_scr[h] + jnp.dot(vt, p.astype(bf16), preferred_element_type=f32)
            m_scr[h] = m_new

    @pl.when(j < i)
    def _():
        step(False)

    @pl.when(j == i)
    def _():
        step(True)
        _store_head_rows(ks_ref, k_ref)
        _store_head_rows(vs_ref, v_ref)
        lam = _diff_lambda(lamp_ref, lam_init)
        for h in range(N_HEADS):
            sl = slice(h * HEAD_DIM, (h + 1) * HEAD_DIM)
            on = acc_scr[h] / l_scr[h]
            o0 = on[:, :tq].T
            o1 = on[:, tq:].T
            o_ref[:, sl] = _diff_epilogue(o0, o1, lam, gain_ref[:, sl], z_ref[:, sl], lam_init)


def _diff_prompt(u, lamp, gain, lam_init, batch, seq):
    t = _divisor_tile(seq, C_BLOCK, 2 * CHUNK)
    nt = seq // t

    pairs = [(i, j) for i in range(nt) for j in range(i + 1)]
    q_tile = jnp.asarray([p[0] for p in pairs], jnp.int32)
    k_tile = jnp.asarray([p[1] for p in pairs], jnp.int32)

    def qblk(col):
        return pl.BlockSpec((t, BR_WIDTH), lambda b, p, qt, kt: (b * nt + qt[p], col // BR_WIDTH))

    def kblk(col):
        return pl.BlockSpec((t, BR_WIDTH), lambda b, p, qt, kt: (b * nt + kt[p], col // BR_WIDTH))

    state_shape = jax.ShapeDtypeStruct((batch, seq * N_HEADS, HEAD_DIM), f32)
    state_spec = pl.BlockSpec((None, t * N_HEADS, HEAD_DIM), lambda b, p, qt, kt: (b, qt[p], 0))
    return pl.pallas_call(
        functools.partial(_diff_prompt_body, lam_init=lam_init),
        out_shape=[jax.ShapeDtypeStruct((batch * seq, BR_WIDTH), bf16), state_shape, state_shape],
        grid_spec=pltpu.PrefetchScalarGridSpec(
            num_scalar_prefetch=2,
            grid=(batch, len(pairs)),
            in_specs=[qblk(U_CQ), kblk(U_CK), kblk(U_CV), qblk(U_CZ),
                      pl.BlockSpec((4, DIFF_DH), lambda b, p, qt, kt: (0, 0)),
                      pl.BlockSpec((1, BR_WIDTH), lambda b, p, qt, kt: (0, 0))],
            out_specs=[pl.BlockSpec((t, BR_WIDTH), lambda b, p, qt, kt: (b * nt + qt[p], 0)),
                       state_spec, state_spec],
            scratch_shapes=[pltpu.VMEM((N_HEADS, 1, 2 * t), f32),
                            pltpu.VMEM((N_HEADS, 1, 2 * t), f32),
                            pltpu.VMEM((N_HEADS, HEAD_DIM, 2 * t), f32)]),
        compiler_params=_params("parallel", "arbitrary"),
        name="diff_attn_prompt",
    )(q_tile, k_tile, u, u, u, u, lamp, gain.reshape(1, BR_WIDTH))


def _diff_sample_body(q_ref, k_ref, v_ref, z_ref, ck_ref, cv_ref, lamp_ref, gain_ref, o_ref, ks_ref, vs_ref,
                      *, lam_init):
    t = q_ref.shape[0]
    _store_head_rows(ks_ref, k_ref)
    _store_head_rows(vs_ref, v_ref)
    lo, hi = _half_masks()
    lam = _diff_lambda(lamp_ref, lam_init)
    for h in range(N_HEADS):
        sl = slice(h * HEAD_DIM, (h + 1) * HEAD_DIM)
        q = q_ref[:, sl] * DIFF_SCALE
        qq = jnp.concatenate([jnp.where(lo, q, 0.0), jnp.where(hi, q, 0.0)], axis=0).astype(bf16)
        k = k_ref[:, sl].astype(bf16)
        v = v_ref[:, sl].astype(bf16)
        ck = _head_rows(ck_ref, h, True).astype(bf16)
        cv = _head_rows(cv_ref, h, True).astype(bf16)
        s_c = lax.dot_general(qq, ck, NT_DIMS, preferred_element_type=f32)
        s_n = lax.dot_general(qq, k, NT_DIMS, preferred_element_type=f32)
        m = jnp.maximum(jnp.max(s_c, axis=-1, keepdims=True), jnp.max(s_n, axis=-1, keepdims=True))
        p_c = jnp.exp(s_c - m)
        p_n = jnp.exp(s_n - m)
        l = jnp.sum(p_c, axis=-1, keepdims=True) + jnp.sum(p_n, axis=-1, keepdims=True)
        o = (jnp.dot(p_c.astype(bf16), cv, preferred_element_type=f32)
             + jnp.dot(p_n.astype(bf16), v, preferred_element_type=f32)) / l
        o_ref[:, sl] = _diff_epilogue(o[:t], o[t:], lam, gain_ref[:, sl], z_ref[:, sl], lam_init)


CONV_PAD = 8


def _first_last(chunk_axis):
    if chunk_axis is None:
        def run(f):
            f()
        return run, run
    c = pl.program_id(chunk_axis)
    return pl.when(c == 0), pl.when(c == pl.num_programs(chunk_axis) - 1)


def _mlstm_body(qk_ref, v_ref, og_ref, z_ref, gc_ref, gr_ref, cw_ref, cb_ref, gbr_ref, gbc_ref, gn_ref,
                c0_ref, n0_ref, m0_ref, conv0_ref,
                out_ref, c_out_ref, n_out_ref, m_out_ref,
                cbuf, c_scr, n_scr, m_scr, *, chunk_axis=1):
    ln = qk_ref.shape[0]
    on_first, on_last = _first_last(chunk_axis)
    lo = CONV_PAD - (CONV_W - 1)

    @on_first
    def _():
        cbuf[lo:CONV_PAD, :] = conv0_ref[...]
        c_scr[...] = c0_ref[...]
        n_scr[...] = n0_ref[...]
        m_scr[...] = m0_ref[...]

    cbuf[CONV_PAD:CONV_PAD + ln, :] = qk_ref[...]
    conv = cb_ref[...] + cbuf[lo:lo + ln, :] * cw_ref[0:1, :]
    for jj in range(1, CONV_W):
        conv = conv + cbuf[lo + jj:lo + jj + ln, :] * cw_ref[jj:jj + 1, :]
    tail = cbuf[lo + ln:CONV_PAD + ln, :]
    cbuf[lo:CONV_PAD, :] = tail
    act = _silu(conv)

    gcb = gc_ref[...] + gbr_ref[...]
    grb = gr_ref[...] + gbc_ref[...]
    row = lax.broadcasted_iota(jnp.int32, (ln, ln), 0)
    col = lax.broadcasted_iota(jnp.int32, (ln, ln), 1)
    causal = col <= row

    for h in range(N_HEADS):
        sl = slice(h * HEAD_DIM, (h + 1) * HEAD_DIM)
        q = act[:, h * HEAD_DIM:(h + 1) * HEAD_DIM]
        k = act[:, BR_WIDTH + h * HEAD_DIM:BR_WIDTH + (h + 1) * HEAD_DIM] * HEAD_SCALE
        v = v_ref[:, sl]
        i_col = gcb[:, h:h + 1]
        lf_col = _log_sigmoid(gcb[:, N_HEADS + h:N_HEADS + h + 1])
        i_row = grb[h:h + 1, :]
        lf_row = _log_sigmoid(grb[N_HEADS + h:N_HEADS + h + 1, :])
        b_col = jnp.sum(jnp.where(causal, lf_row, 0.0), axis=1, keepdims=True)
        b_row = jnp.sum(jnp.where(row <= col, lf_col, 0.0), axis=0, keepdims=True)
        b_last = b_col[ln - 1:ln, :]
        m_old = m_scr[h][:, 0:1]
        c_old = c_scr[h]
        n_old = n_scr[h]

        dlog = jnp.where(causal, b_col - b_row + i_row, NEG)
        inter = b_col + m_old
        mt = jnp.maximum(inter, jnp.max(dlog, axis=1, keepdims=True))
        w = jnp.exp(dlog - mt)
        wi = jnp.exp(inter - mt)
        qb = q.astype(bf16)
        vb = v.astype(bf16)
        qk = lax.dot_general(qb, k.astype(bf16), NT_DIMS, preferred_element_type=f32) * w
        num = (jnp.dot(qk.astype(bf16), vb, preferred_element_type=f32)
               + wi * jnp.dot(qb, c_old.astype(bf16), preferred_element_type=f32))
        den = jnp.sum(qk, axis=1, keepdims=True) + wi * jnp.sum(q * n_old, axis=1, keepdims=True)
        hh = num / jnp.maximum(jnp.abs(den), jnp.exp(-mt))

        m_new = mt[ln - 1:ln, :]
        ws = jnp.exp(b_last - b_col + i_col - m_new)
        dec = jnp.exp(b_last + m_old - m_new)
        kw = k * ws
        c_scr[h] = dec * c_old + lax.dot_general(kw.astype(bf16), vb, TN_DIMS, preferred_element_type=f32)
        n_scr[h] = dec * n_old + jnp.sum(kw, axis=0, keepdims=True)
        m_scr[h] = jnp.broadcast_to(m_new, (1, HEAD_DIM))

        y = _head_norm(_sigmoid(og_ref[:, sl]) * hh, gn_ref[:, sl])
        out_ref[:, sl] = (y * _silu(z_ref[:, sl])).astype(bf16)

    @on_last
    def _():
        c_out_ref[...] = c_scr[...]
        n_out_ref[...] = n_scr[...]
        m_out_ref[...] = m_scr[...]


def _mlstm(u, gates_r, conv_w, conv_b, b_ig, b_fg, gn, c0, n0, m0, conv0, row0, ng, nc, ln):
    rb0 = row0 // ln
    gbias = jnp.concatenate([b_ig, b_fg]).astype(f32)
    gbr = jnp.zeros((1, GATE_PAD), f32).at[0, :2 * N_HEADS].set(gbias)
    gbc = gbias.reshape(2 * N_HEADS, 1)

    def blk(col, width):
        return pl.BlockSpec((ln, width), lambda g, c: (rb0 + g * nc + c, col // width))

    def const(shape):
        return pl.BlockSpec(shape, lambda g, c: (0,) * len(shape))

    def per_seq(shape):
        return pl.BlockSpec((None,) + shape, lambda g, c: (g,) + (0,) * len(shape))

    state_shapes = [jax.ShapeDtypeStruct((ng, N_HEADS, HEAD_DIM, HEAD_DIM), f32),
                    jax.ShapeDtypeStruct((ng, N_HEADS, 1, HEAD_DIM), f32),
                    jax.ShapeDtypeStruct((ng, N_HEADS, 1, HEAD_DIM), f32)]
    state_specs = [per_seq((N_HEADS, HEAD_DIM, HEAD_DIM)),
                   per_seq((N_HEADS, 1, HEAD_DIM)),
                   per_seq((N_HEADS, 1, HEAD_DIM))]
    out, c_new, n_new, m_new = pl.pallas_call(
        _mlstm_body,
        out_shape=[jax.ShapeDtypeStruct((ng * nc * ln, BR_WIDTH), bf16)] + state_shapes,
        grid=(ng, nc),
        in_specs=[blk(U_BQK, B_QK), blk(U_BV, BR_WIDTH), blk(U_BO, BR_WIDTH), blk(U_BZ, BR_WIDTH),
                  blk(U_GATES, GATE_PAD),
                  pl.BlockSpec((None, 2 * N_HEADS, ln), lambda g, c: (g * nc + c, 0, 0)),
                  const((CONV_W, B_QK)), const((1, B_QK)), const((1, GATE_PAD)), const((2 * N_HEADS, 1)),
                  const((1, BR_WIDTH))] + state_specs + [per_seq((CONV_W - 1, B_QK))],
        out_specs=[pl.BlockSpec((ln, BR_WIDTH), lambda g, c: (g * nc + c, 0))] + state_specs,
        scratch_shapes=[pltpu.VMEM((CONV_PAD + ln, B_QK), f32),
                        pltpu.VMEM((N_HEADS, HEAD_DIM, HEAD_DIM), f32),
                        pltpu.VMEM((N_HEADS, 1, HEAD_DIM), f32),
                        pltpu.VMEM((N_HEADS, 1, HEAD_DIM), f32)],
        compiler_params=_params("parallel", "arbitrary"),
        name="mlstm",
    )(u, u, u, u, u, gates_r, conv_w, conv_b.reshape(1, B_QK), gbr, gbc, gn.reshape(1, BR_WIDTH),
      c0, n0.reshape(ng, N_HEADS, 1, HEAD_DIM),
      jnp.broadcast_to(m0[:, :, None, None], (ng, N_HEADS, 1, HEAD_DIM)), conv0)
    return out, c_new, n_new[:, :, 0, :], m_new[:, :, 0, 0]


def _ret_body(q_ref, k_ref, v_ref, z_ref, cc_ref, ss_ref, gn_ref, s0_ref, out_ref, s_out_ref, s_scr,
              *, chunk_axis=1):
    ln = q_ref.shape[0]
    on_first, on_last = _first_last(chunk_axis)

    @on_first
    def _():
        s_scr[...] = s0_ref[...]

    row = lax.broadcasted_iota(jnp.int32, (ln, ln), 0)
    col = lax.broadcasted_iota(jnp.int32, (ln, ln), 1)
    rel = (row - col).astype(f32)
    tpos = lax.broadcasted_iota(jnp.int32, (ln, 1), 0).astype(f32)
    cc = cc_ref[...]
    ss = ss_ref[...]
    for h in range(N_HEADS):
        sl = slice(h * HEAD_DIM, (h + 1) * HEAD_DIM)
        lg = math.log(1.0 - 2.0 ** (-RET_GAMMA_EXP0 - h))
        q = q_ref[:, sl]
        k = k_ref[:, sl]
        qr = q * cc + pltpu.roll(q, DIFF_DH, 1) * ss
        kr = (k * cc + pltpu.roll(k, DIFF_DH, 1) * ss) * HEAD_SCALE
        vb = v_ref[:, sl].astype(bf16)
        decay = jnp.where(rel >= 0.0, jnp.exp(jnp.maximum(rel, 0.0) * lg), 0.0)
        att = lax.dot_general(qr.astype(bf16), kr.astype(bf16), NT_DIMS, preferred_element_type=f32) * decay
        q_dec = qr * jnp.exp((tpos + 1.0) * lg)
        s_old = s_scr[h]
        o = (jnp.dot(att.astype(bf16), vb, preferred_element_type=f32)
             + jnp.dot(q_dec.astype(bf16), s_old.astype(bf16), preferred_element_type=f32))
        k_dec = kr * jnp.exp((ln - 1.0 - tpos) * lg)
        s_scr[h] = math.exp(ln * lg) * s_old + lax.dot_general(k_dec.astype(bf16), vb, TN_DIMS,
                                                               preferred_element_type=f32)
        out_ref[:, sl] = (_head_norm(o, gn_ref[:, sl]) * _silu(z_ref[:, sl])).astype(bf16)

    @on_last
    def _():
        s_out_ref[...] = s_scr[...]


def _rope_tables(pos):
    inv = ROPE_BASE ** (-jnp.arange(0, HEAD_DIM, 2, dtype=f32) / HEAD_DIM)
    ang = pos.astype(f32)[:, None] * inv[None, :]
    cos, sin = jnp.cos(ang), jnp.sin(ang)
    return jnp.concatenate([cos, cos], axis=-1), jnp.concatenate([-sin, sin], axis=-1)


def _retention(u, pos, gn, s0, row0, ng, nc, ln):
    rb0 = row0 // ln
    cc, ss = _rope_tables(pos)

    def blk(col):
        return pl.BlockSpec((ln, BR_WIDTH), lambda g, c: (rb0 + g * nc + c, col // BR_WIDTH))

    rope_spec = pl.BlockSpec((ln, HEAD_DIM), lambda g, c: (c, 0))
    state_spec = pl.BlockSpec((None, N_HEADS, HEAD_DIM, HEAD_DIM), lambda g, c: (g, 0, 0, 0))
    return pl.pallas_call(
        _ret_body,
        out_shape=[jax.ShapeDtypeStruct((ng * nc * ln, BR_WIDTH), bf16),
                   jax.ShapeDtypeStruct((ng, N_HEADS, HEAD_DIM, HEAD_DIM), f32)],
        grid=(ng, nc),
        in_specs=[blk(U_DQ), blk(U_DK), blk(U_DV), blk(U_DZ), rope_spec, rope_spec,
                  pl.BlockSpec((1, BR_WIDTH), lambda g, c: (0, 0)), state_spec],
        out_specs=[pl.BlockSpec((ln, BR_WIDTH), lambda g, c: (g * nc + c, 0)), state_spec],
        scratch_shapes=[pltpu.VMEM((N_HEADS, HEAD_DIM, HEAD_DIM), f32)],
        compiler_params=_params("parallel", "arbitrary"),
        name="retention",
    )(u, u, u, u, cc, ss, gn.reshape(1, BR_WIDTH), s0)


def _sample_branches_body(*refs, lam_init):
    it = iter(refs)

    def take(n):
        return [next(it) for _ in range(n)]

    a_in, c_in, m_in, b_in, d_in = take(8), take(8), take(4), take(15), take(8)
    a_out, c_out, m_out, b_out, d_out = take(3), take(3), take(1), take(4), take(2)
    b_scr, d_scr = take(4), take(1)
    _band_sample_body(*a_in, *a_out)
    _diff_sample_body(*c_in, *c_out, lam_init=lam_init)
    _mem_body(*m_in, *m_out, interleaved=True)
    _mlstm_body(*b_in, *b_out, *b_scr, chunk_axis=None)
    _ret_body(*d_in, *d_out, *d_scr, chunk_axis=None)


def _sample_branches(u, layer, caches, rel_bias, lamp, lam_init, gains, conv_w, conv_b, b_ig, b_fg,
                     states, pos, nb, t):
    ca_k, ca_v, cc_k, cc_v, cm_k, cm_v = caches
    subln_c, gn_b, gn_d = gains
    c0, n0, m0, conv0, s0 = states
    nrow = ca_k.shape[2] // N_HEADS
    bias_c = _rel_bias_toeplitz(rel_bias, t, nrow, nrow)
    bias_n = _rel_bias_toeplitz(rel_bias, t, t, 0)
    gbias = jnp.concatenate([b_ig, b_fg]).astype(f32)
    gbr = jnp.zeros((1, GATE_PAD), f32).at[0, :2 * N_HEADS].set(gbias)
    cc, ss = _rope_tables(pos)

    def blk(col, width=BR_WIDTH):
        return pl.BlockSpec((t, width), lambda b: (b, col // width))

    def const(shape):
        return pl.BlockSpec(shape, lambda b: (0,) * len(shape))

    def per_seq(shape):
        return pl.BlockSpec((None,) + shape, lambda b: (b,) + (0,) * len(shape))

    def cache(arr):
        return pl.BlockSpec((None, None) + arr.shape[2:], lambda b: (layer, b, 0, 0))

    mat, vec = (N_HEADS, HEAD_DIM, HEAD_DIM), (N_HEADS, 1, HEAD_DIM)
    kv_state = (t * N_HEADS, HEAD_DIM)
    in_specs = (
        [blk(U_AQ), blk(U_AK), blk(U_AV), blk(U_AZ), cache(ca_k), cache(ca_v),
         const((N_HEADS, t, nrow)), const((N_HEADS, t, t))]
        + [blk(U_CQ), blk(U_CK), blk(U_CV), blk(U_CZ), cache(cc_k), cache(cc_v),
           const((4, DIFF_DH)), const((1, BR_WIDTH))]
        + [blk(U_MQ), blk(U_MZ), cache(cm_k), cache(cm_v)]
        + [blk(U_BQK, B_QK), blk(U_BV), blk(U_BO), blk(U_BZ), blk(U_GATES, GATE_PAD),
           per_seq((2 * N_HEADS, t)), const((CONV_W, B_QK)), const((1, B_QK)), const((1, GATE_PAD)),
           const((2 * N_HEADS, 1)), const((1, BR_WIDTH)), per_seq(mat), per_seq(vec), per_seq(vec),
           per_seq((CONV_W - 1, B_QK))]
        + [blk(U_DQ), blk(U_DK), blk(U_DV), blk(U_DZ), const((t, HEAD_DIM)), const((t, HEAD_DIM)),
           const((1, BR_WIDTH)), per_seq(mat)])
    args = (
        [u, u, u, u, ca_k, ca_v, bias_c, bias_n]
        + [u, u, u, u, cc_k, cc_v, lamp, subln_c.reshape(1, BR_WIDTH)]
        + [u, u, cm_k, cm_v]
        + [u, u, u, u, u, _gates_rowform(u, 0, nb * t, t), conv_w, conv_b.reshape(1, B_QK), gbr,
           gbias.reshape(2 * N_HEADS, 1), gn_b.reshape(1, BR_WIDTH), c0,
           n0.reshape(nb, N_HEADS, 1, HEAD_DIM),
           jnp.broadcast_to(m0[:, :, None, None], (nb, N_HEADS, 1, HEAD_DIM)), conv0]
        + [u, u, u, u, cc, ss, gn_d.reshape(1, BR_WIDTH), s0])
    branch_out = jax.ShapeDtypeStruct((nb * t, BR_WIDTH), bf16)
    kv_out = jax.ShapeDtypeStruct((nb,) + kv_state, f32)
    mat_out = jax.ShapeDtypeStruct((nb,) + mat, f32)
    vec_out = jax.ShapeDtypeStruct((nb,) + vec, f32)
    out_shape = ([branch_out, kv_out, kv_out] * 2 + [branch_out]
                 + [branch_out, mat_out, vec_out, vec_out] + [branch_out, mat_out])
    o_spec = pl.BlockSpec((t, BR_WIDTH), lambda b: (b, 0))
    out_specs = ([o_spec, per_seq(kv_state), per_seq(kv_state)] * 2 + [o_spec]
                 + [o_spec, per_seq(mat), per_seq(vec), per_seq(vec)] + [o_spec, per_seq(mat)])
    (oa, ak, av, oc, ck, cv, om, ob, c_new, n_new, m_new, od, s_new) = pl.pallas_call(
        functools.partial(_sample_branches_body, lam_init=lam_init),
        out_shape=out_shape,
        grid=(nb,),
        in_specs=in_specs,
        out_specs=out_specs,
        scratch_shapes=[pltpu.VMEM((CONV_PAD + t, B_QK), f32), pltpu.VMEM(mat, f32),
                        pltpu.VMEM(vec, f32), pltpu.VMEM(vec, f32), pltpu.VMEM(mat, f32)],
        compiler_params=_params("arbitrary"),
        name="sample_branches",
    )(*args)
    return ((oa, ob, oc, od, om), (ak, av, ck, cv),
            (c_new, n_new[:, :, 0, :], m_new[:, :, 0, 0], s_new))


def _merge_body(x_ref, g_ref, oa_ref, ob_ref, oc_ref, od_ref, om_ref,
                wg0_ref, wg1_ref, wg2_ref, wg3_ref, wg4_ref, wb_ref, wo_ref, fg_ref,
                wo_last_ref, y_ref, h_scr, mrg_scr, *, rows, final):
    n = pl.program_id(1)

    @pl.when(n == 0)
    def _():
        _rmsnorm_rows(x_ref, g_ref, h_scr, rows)
        y_ref[...] = jnp.zeros(y_ref.shape, f32)
        mrg_scr[...] = jnp.zeros(mrg_scr.shape, bf16)

    y_ref[...] += jnp.dot(mrg_scr[...], wo_ref[...], preferred_element_type=f32)
    h = h_scr[...]
    merged = None
    for i, (o_ref, wg_ref) in enumerate(zip((oa_ref, ob_ref, oc_ref, od_ref, om_ref),
                                            (wg0_ref, wg1_ref, wg2_ref, wg3_ref, wg4_ref))):
        gate = _sigmoid(lax.dot_general(h, wg_ref[...], NT_DIMS, preferred_element_type=f32))
        term = gate * jnp.dot(o_ref[...], wb_ref[i], preferred_element_type=f32)
        merged = term if merged is None else merged + term
    mrg_scr[...] = merged.astype(bf16)

    @pl.when(n == pl.num_programs(1) - 1)
    def _():
        y_ref[...] += jnp.dot(mrg_scr[...], wo_last_ref[...], preferred_element_type=f32)
        tm = x_ref.shape[0]
        for r in range(0, tm, rows):
            y = x_ref[r:r + rows, :] + y_ref[r:r + rows, :]
            if final:
                y = (y * lax.rsqrt(jnp.mean(y * y, axis=-1, keepdims=True) + EPS)) * fg_ref[...]
            y_ref[r:r + rows, :] = y


def _merge(x, g, outs, wg, wb, wo, fg, *, final, name, tm_target=704, tn=256):
    m, d = x.shape
    tm = _divisor_tile(m, tm_target, 16)
    rows = _divisor_tile(tm, 256, 8)
    nn = d // tn

    def wg_spec(i):
        return pl.BlockSpec((tn, d), lambda r, n: (i * nn + n, 0))

    o_spec = pl.BlockSpec((tm, BR_WIDTH), lambda r, n: (r, 0))
    return pl.pallas_call(
        functools.partial(_merge_body, rows=rows, final=final),
        out_shape=jax.ShapeDtypeStruct((m, d), f32),
        grid=(m // tm, nn),
        in_specs=[pl.BlockSpec((tm, d), lambda r, n: (r, 0)),
                  pl.BlockSpec((1, d), lambda r, n: (0, 0))]
                 + [o_spec] * N_BRANCH
                 + [wg_spec(i) for i in range(N_BRANCH)]
                 + [pl.BlockSpec((N_BRANCH, BR_WIDTH, tn), lambda r, n: (0, 0, n)),
                    pl.BlockSpec((tn, d), lambda r, n: (jnp.maximum(n - 1, 0), 0)),
                    pl.BlockSpec((1, d), lambda r, n: (0, 0)),
                    pl.BlockSpec((tn, d), lambda r, n: (nn - 1, 0))],
        out_specs=pl.BlockSpec((tm, d), lambda r, n: (r, 0)),
        scratch_shapes=[pltpu.VMEM((tm, d), bf16), pltpu.VMEM((tm, tn), bf16)],
        compiler_params=_params("parallel", "arbitrary"),
        name=name,
    )(x, g.reshape(1, d), *outs, wg, wg, wg, wg, wg, wb, wo, fg.reshape(1, d), wo)


F32_SUBLANES = 8


def _cast_rows_body(src_ref, o_ref, *, valid_rows_last):
    tr = o_ref.shape[0]
    x = src_ref[0]
    if valid_rows_last is not None:
        row = lax.broadcasted_iota(jnp.int32, (tr, 1), 0)
        keep = jnp.logical_or(pl.program_id(0) < pl.num_programs(0) - 1, row < valid_rows_last)
        x = jnp.where(keep, x, 0.0)
    o_ref[...] = x.astype(bf16)


def _cast_rows(w_t, layer, *, n_rows, tr, src_row, valid_rows_last, name):
    d = w_t.shape[2]
    assert n_rows % tr == 0
    return pl.pallas_call(
        functools.partial(_cast_rows_body, valid_rows_last=valid_rows_last),
        out_shape=jax.ShapeDtypeStruct((n_rows, d), bf16),
        grid=(n_rows // tr,),
        in_specs=[pl.BlockSpec((pl.Element(1), pl.Element(tr), pl.Element(d)),
                               lambda j: (layer, pl.multiple_of(src_row(j), F32_SUBLANES), 0))],
        out_specs=pl.BlockSpec((tr, d), lambda j: (j, 0)),
        compiler_params=_params("arbitrary"),
        name=name,
    )(w_t)


def _repack_w_in(w_t, layer):
    tr = MXU_COLS
    n_plain, n_main = W_BI // tr, U_GATES // tr

    def u_src(j):
        return jnp.where(j < n_plain, j * tr, jnp.where(j < n_main, j * tr + (W_BZ - W_BI), W_BI))

    wu = _cast_rows(w_t, layer, n_rows=U_WIDTH, tr=tr, src_row=u_src, valid_rows_last=2 * N_HEADS,
                    name="repack_u")
    wg = _cast_rows(w_t, layer, n_rows=N_BRANCH * D_MODEL, tr=512, src_row=lambda j: W_GATE + j * 512,
                    valid_rows_last=None, name="repack_g")
    return wu, wg


def _gates_rowform(u, row0, nrows, ln):
    g = u[row0:row0 + nrows, U_GATES:U_GATES + 2 * N_HEADS]
    return g.reshape(nrows // ln, ln, 2 * N_HEADS).transpose(0, 2, 1)


def kernel(x_prompt, x_sample, mem_prompt, cache_a_k, cache_a_v, cache_c_k, cache_c_v, cache_mem_k, cache_mem_v, state_b_C, state_b_n, state_b_m, state_b_conv, state_d_S, norm_g, w_in, conv_w, conv_b, b_ig, b_fg, rel_bias, lam_q1, lam_k1, lam_q2, lam_k2, gn_b, subln_c, gn_d, mem_norm_g, w_mk, w_mv, w_branch, w_out, final_g):
    batch, seq, d = x_prompt.shape
    nb, t, _ = x_sample.shape
    depth = w_in.shape[0]
    n_mem = mem_prompt.shape[1]
    past = cache_c_k.shape[2]
    mp = batch * seq
    assert d == D_MODEL and seq % C_BLOCK == 0 and seq % A_QBLOCK == 0 and seq % SCAN_CHUNK == 0

    xp = x_prompt.reshape(mp, d)
    xs = x_sample.reshape(nb * t, d)
    mem = mem_prompt.reshape(batch * n_mem, d)
    ca_k, ca_v = _cache_rows(cache_a_k), _cache_rows(cache_a_v)
    cc_k, cc_v = _cache_rows(cache_c_k), _cache_rows(cache_c_v)
    cm_k, cm_v = _cache_rows(cache_mem_k), _cache_rows(cache_mem_v)
    pos_p = jnp.arange(seq)
    pos_s = past + jnp.arange(t)
    a_rows = min(A_WINDOW, seq)
    mem_tq = _divisor_tile(seq, 512, 16)

    w_t = jnp.swapaxes(w_in, 1, 2)

    sp, ss = [], []
    for l in range(depth):
        lam_init = 0.8 - 0.6 * math.exp(-0.3 * l)
        lamp = jnp.stack([lam_q1[l], lam_k1[l], lam_q2[l], lam_k2[l]]).astype(f32)
        wu, wg = _repack_w_in(w_t, l)
        u = _norm_matmul(xp, norm_g[l], wu, tm_target=1024, tn=U_TILE, w_rows_are_outputs=True,
                         name="in_proj")
        us = _norm_matmul(xs, norm_g[l], wu, tm_target=1024, tn=U_TILE, w_rows_are_outputs=True,
                          name="in_proj_sample")
        mkv = _norm_matmul(mem, mem_norm_g[l], jnp.concatenate([w_mk[l], w_mv[l]], axis=1).astype(bf16),
                           tm_target=512, tn=512, w_rows_are_outputs=False, name="mem_proj")

        oa_p, ak_p, av_p = _band_prompt(u, rel_bias[l], batch, seq, a_rows)
        zeros_c = jnp.zeros((batch, N_HEADS, HEAD_DIM, HEAD_DIM), f32)
        ob_p, c_p, n_p, m_p = _mlstm(
            u, _gates_rowform(u, 0, mp, SCAN_CHUNK), conv_w[l], conv_b[l], b_ig[l], b_fg[l], gn_b[l],
            zeros_c, jnp.zeros((batch, N_HEADS, HEAD_DIM), f32), jnp.zeros((batch, N_HEADS), f32),
            jnp.zeros((batch, CONV_W - 1, B_QK), f32), 0, batch, seq // SCAN_CHUNK, SCAN_CHUNK)
        oc_p, ck_p, cv_p = _diff_prompt(u, lamp, subln_c[l], lam_init, batch, seq)
        od_p, s_p = _retention(u, pos_p, gn_d[l], zeros_c, 0, batch, seq // SCAN_CHUNK, SCAN_CHUNK)
        mem_spec_k = pl.BlockSpec((n_mem, BR_WIDTH), lambda b, i: (b, 0))
        mem_spec_v = pl.BlockSpec((n_mem, BR_WIDTH), lambda b, i: (b, 1))
        om_p = _mem_attn(u, 0, batch, seq, mem_tq, mkv, mkv, mem_spec_k, mem_spec_v, False, "mem_attn_prompt")

        outs_s, (ak_s, av_s, ck_s, cv_s), (c_s, n_s, m_s, s_s) = _sample_branches(
            us, l, (ca_k, ca_v, cc_k, cc_v, cm_k, cm_v), rel_bias[l], lamp, lam_init,
            (subln_c[l], gn_b[l], gn_d[l]), conv_w[l], conv_b[l], b_ig[l], b_fg[l],
            (state_b_C[l].astype(f32), state_b_n[l].astype(f32), state_b_m[l].astype(f32),
             state_b_conv[l].astype(f32), state_d_S[l].astype(f32)), pos_s, nb, t)

        wb, wo = w_branch[l].astype(bf16), w_out[l].astype(bf16)
        final = l == depth - 1
        xp = _merge(xp, norm_g[l], (oa_p, ob_p, oc_p, od_p, om_p), wg, wb, wo, final_g, final=final,
                    name="gated_merge")
        xs = _merge(xs, norm_g[l], outs_s, wg, wb, wo, final_g, final=final, name="gated_merge_sample")

        def conv_tail(arr, n_seq, rows):
            return jnp.stack([arr[(i + 1) * rows - (CONV_W - 1):(i + 1) * rows, U_BQK:U_BQK + B_QK]
                              for i in range(n_seq)])

        def hd(a):
            return a.reshape(a.shape[:2] + (N_HEADS, HEAD_DIM))

        def rows_hd(a):
            return a.reshape(a.shape[0], a.shape[1] // N_HEADS, N_HEADS, HEAD_DIM)

        sp.append((rows_hd(ak_p), rows_hd(av_p), rows_hd(ck_p), rows_hd(cv_p),
                   hd(mkv[:, :BR_WIDTH].reshape(batch, n_mem, BR_WIDTH)),
                   hd(mkv[:, BR_WIDTH:].reshape(batch, n_mem, BR_WIDTH)),
                   c_p, n_p, m_p, conv_tail(u, batch, seq), s_p))
        ss.append((rows_hd(ak_s), rows_hd(av_s), rows_hd(ck_s), rows_hd(cv_s),
                   c_s, n_s, m_s, us[:, U_BQK:U_BQK + B_QK].reshape(nb, t, B_QK)[:, t - (CONV_W - 1):], s_s))

    y_prompt = xp.reshape(batch, seq, d)
    y_sample = xs.reshape(nb, t, d)
    p_states = tuple(jnp.stack([st[i] for st in sp]) for i in range(11))
    s_states = tuple(jnp.stack([st[i] for st in ss]) for i in range(9))
    return (y_prompt, y_sample) + p_states + s_states
```

```python
import functools
import math

import jax
import jax.numpy as jnp
import numpy as np
from jax import lax
from jax.experimental import pallas as pl
from jax.experimental.pallas import tpu as pltpu

f32 = jnp.float32
bf16 = jnp.bfloat16

D_MODEL = 2048
N_HEADS = 4
HEAD_DIM = 128
BR_WIDTH = N_HEADS * HEAD_DIM
N_BRANCH = 5
CHUNK = 64
A_PREV_CHUNKS = 8
A_WINDOW = A_PREV_CHUNKS * CHUNK
A_BAND = (A_PREV_CHUNKS + 1) * CHUNK
A_REL_CLIP = 128
CONV_W = 4
B_QK = 2 * BR_WIDTH
DIFF_DH = HEAD_DIM // 2
RET_GAMMA_EXP0 = 5.0
ROPE_BASE = 10000.0
EPS = 1e-6
NEG = -1e30
HEAD_SCALE = HEAD_DIM ** -0.5
DIFF_SCALE = DIFF_DH ** -0.5

U_AQ, U_AK, U_AV, U_AZ = 0, 512, 1024, 1536
U_BQK, U_BV, U_BO, U_BZ = 2048, 3072, 3584, 4096
U_CQ, U_CK, U_CV, U_CZ = 4608, 5120, 5632, 6144
U_DQ, U_DK, U_DV, U_DZ = 6656, 7168, 7680, 8192
U_MQ, U_MZ = 8704, 9216
U_GATES = 9728
GATE_PAD = 128
MXU_COLS = 256
U_WIDTH = U_GATES + MXU_COLS
U_TILE = 3 * MXU_COLS
W_BI = 4096
W_BZ = 4104
W_GATE = 9736

VMEM_LIMIT_BYTES = 56 * 1024 * 1024
A_QBLOCK = 4 * CHUNK
C_BLOCK = 512
SCAN_CHUNK = 256
NT_DIMS = (((1,), (1,)), ((), ()))
TN_DIMS = (((0,), (0,)), ((), ()))


def _divisor_tile(n, target, multiple):
    best = None
    for t in range(multiple, min(n, target) + 1, multiple):
        if n % t == 0:
            best = t
    assert best is not None, (n, target, multiple)
    return best


def _params(*sem):
    return pltpu.CompilerParams(dimension_semantics=sem, vmem_limit_bytes=VMEM_LIMIT_BYTES)


def _sigmoid(x):
    return 1.0 / (1.0 + jnp.exp(-x))


def _silu(x):
    return x * _sigmoid(x)


def _log_sigmoid(x):
    return jnp.minimum(x, 0.0) - jnp.log1p(jnp.exp(-jnp.abs(x)))


def _head_norm(y, gain):
    return y * lax.rsqrt(jnp.mean(y * y, axis=-1, keepdims=True) + EPS) * gain


def _rmsnorm_rows(x_ref, g_ref, h_ref, rows):
    tm = x_ref.shape[0]
    for r in range(0, tm, rows):
        x = x_ref[r:r + rows, :]
        ms = jnp.mean(x * x, axis=-1, keepdims=True)
        h_ref[r:r + rows, :] = ((x * lax.rsqrt(ms + EPS)) * g_ref[...]).astype(bf16)


def _norm_matmul_body(x_ref, g_ref, w_ref, o_ref, h_ref, *, rows, w_rows_are_outputs):
    @pl.when(pl.program_id(1) == 0)
    def _():
        _rmsnorm_rows(x_ref, g_ref, h_ref, rows)

    if w_rows_are_outputs:
        o = lax.dot_general(h_ref[...], w_ref[...], NT_DIMS, preferred_element_type=f32)
    else:
        o = jnp.dot(h_ref[...], w_ref[...], preferred_element_type=f32)
    o_ref[...] = o.astype(o_ref.dtype)


def _norm_matmul(x, g, w, *, tm_target, tn, w_rows_are_outputs, name):
    m, d = x.shape
    n = w.shape[0] if w_rows_are_outputs else w.shape[1]
    tm = _divisor_tile(m, tm_target, 16)
    rows = _divisor_tile(tm, 256, 8)
    assert n % tn == 0
    w_spec = (pl.BlockSpec((tn, d), lambda i, j: (j, 0)) if w_rows_are_outputs
              else pl.BlockSpec((d, tn), lambda i, j: (0, j)))
    return pl.pallas_call(
        functools.partial(_norm_matmul_body, rows=rows, w_rows_are_outputs=w_rows_are_outputs),
        out_shape=jax.ShapeDtypeStruct((m, n), f32),
        grid=(m // tm, n // tn),
        in_specs=[pl.BlockSpec((tm, d), lambda i, j: (i, 0)),
                  pl.BlockSpec((1, d), lambda i, j: (0, 0)),
                  w_spec],
        out_specs=pl.BlockSpec((tm, tn), lambda i, j: (i, j)),
        scratch_shapes=[pltpu.VMEM((tm, d), bf16)],
        compiler_params=_params("parallel", "arbitrary"),
        name=name,
    )(x, g.reshape(1, d), w)


def _store_head_rows(dst_ref, src_ref):
    rows = src_ref.shape[0]
    for h in range(N_HEADS):
        dst_ref[pl.ds(h, rows, stride=N_HEADS), :] = src_ref[:, h * HEAD_DIM:(h + 1) * HEAD_DIM]


def _band_prompt_body(q_ref, k0_ref, k1_ref, k2_ref, v0_ref, v1_ref, v2_ref, z_ref, bias_ref,
                      o_ref, ks_ref, vs_ref, *, n_tail):
    tq = q_ref.shape[0]
    i = pl.program_id(1)

    @pl.when(i >= pl.num_programs(1) - n_tail)
    def _():
        _store_head_rows(ks_ref, k2_ref)
        _store_head_rows(vs_ref, v2_ref)

    w_idx = lax.broadcasted_iota(jnp.int32, (tq, 3 * tq), 1)
    valid = (w_idx + (i - 2) * tq) >= 0
    for h in range(N_HEADS):
        sl = slice(h * HEAD_DIM, (h + 1) * HEAD_DIM)
        q = q_ref[:, sl].astype(bf16)
        kw = jnp.concatenate([k0_ref[:, sl], k1_ref[:, sl], k2_ref[:, sl]], axis=0).astype(bf16)
        vw = jnp.concatenate([v0_ref[:, sl], v1_ref[:, sl], v2_ref[:, sl]], axis=0).astype(bf16)
        s = lax.dot_general(q, kw, NT_DIMS, preferred_element_type=f32) * HEAD_SCALE + bias_ref[h]
        s = jnp.where(valid, s, NEG)
        p = jnp.exp(s - jnp.max(s, axis=-1, keepdims=True))
        l = jnp.sum(p, axis=-1, keepdims=True)
        o = jnp.dot(p.astype(bf16), vw, preferred_element_type=f32) / l
        o_ref[:, sl] = (o * _silu(z_ref[:, sl])).astype(bf16)


def _rel_bias_toeplitz(rel_bias, rows, cols, rel0):
    p = rows + cols
    j = np.arange(p)
    d = np.where(j < cols, j, j - p)
    idx = np.clip(rel0 - d, -A_REL_CLIP, A_REL_CLIP) + A_REL_CLIP
    v = rel_bias.astype(f32)[:, idx]
    flat = jnp.tile(v, (1, rows))[:, :rows * (p - 1)]
    return flat.reshape(rel_bias.shape[0], rows, p - 1)[:, :, :cols]


def _band_bias_prompt(rel_bias):
    r = np.arange(A_QBLOCK)[:, None]
    w = np.arange(3 * A_QBLOCK)[None, :]
    kj = w - CHUNK * (r // CHUNK)
    inside = (kj >= 0) & (kj < A_BAND)
    table = _rel_bias_toeplitz(rel_bias, A_QBLOCK, 3 * A_QBLOCK, 2 * A_QBLOCK)
    return jnp.where(jnp.asarray(inside)[None], table, NEG)


def _band_prompt(u, rel_bias, batch, seq, a_rows):
    tq = A_QBLOCK
    nqb = seq // tq
    assert a_rows % tq == 0
    n_tail = a_rows // tq
    bias = _band_bias_prompt(rel_bias)

    def blk(col, back):
        return pl.BlockSpec((tq, BR_WIDTH),
                            lambda b, i: (b * nqb + jnp.maximum(i - back, 0), col // BR_WIDTH))

    state_shape = jax.ShapeDtypeStruct((batch, a_rows * N_HEADS, HEAD_DIM), f32)
    state_spec = pl.BlockSpec((None, tq * N_HEADS, HEAD_DIM),
                              lambda b, i: (b, jnp.maximum(i - (nqb - n_tail), 0), 0))
    return pl.pallas_call(
        functools.partial(_band_prompt_body, n_tail=n_tail),
        out_shape=[jax.ShapeDtypeStruct((batch * seq, BR_WIDTH), bf16), state_shape, state_shape],
        grid=(batch, nqb),
        in_specs=[blk(U_AQ, 0),
                  blk(U_AK, 2), blk(U_AK, 1), blk(U_AK, 0),
                  blk(U_AV, 2), blk(U_AV, 1), blk(U_AV, 0),
                  blk(U_AZ, 0),
                  pl.BlockSpec((N_HEADS, tq, 3 * tq), lambda b, i: (0, 0, 0))],
        out_specs=[pl.BlockSpec((tq, BR_WIDTH), lambda b, i: (b * nqb + i, 0)), state_spec, state_spec],
        compiler_params=_params("parallel", "arbitrary"),
        name="band_attn_prompt",
    )(u, u, u, u, u, u, u, u, bias)


def _stack_heads(x):
    return jnp.concatenate([x[:, h * HEAD_DIM:(h + 1) * HEAD_DIM] for h in range(N_HEADS)], axis=0)


def _sample_attend(qs, t, ck_ref, cv_ref, k_new, v_new, scale, bias_c=None, bias_n=None):
    rows = qs.shape[0]
    row_head = (lax.broadcasted_iota(jnp.int32, (rows, 1), 0) // t) % N_HEADS
    ck = ck_ref[...].astype(bf16)
    s = lax.dot_general(qs, ck, NT_DIMS, preferred_element_type=f32) * scale
    if bias_c is not None:
        s = s + bias_c
    col_head = lax.broadcasted_iota(jnp.int32, (1, ck.shape[0]), 1) % N_HEADS
    parts = [(jnp.where(col_head == row_head, s, NEG), cv_ref[...].astype(bf16))]
    if k_new is not None:
        s = lax.dot_general(qs, _stack_heads(k_new).astype(bf16), NT_DIMS, preferred_element_type=f32) * scale
        if bias_n is not None:
            s = s + bias_n
        col_head = lax.broadcasted_iota(jnp.int32, (1, N_HEADS * t), 1) // t
        parts.append((jnp.where(col_head == row_head, s, NEG), _stack_heads(v_new).astype(bf16)))
    m = functools.reduce(jnp.maximum, [jnp.max(s, axis=-1, keepdims=True) for s, _ in parts])
    l, o = 0.0, 0.0
    for s, v in parts:
        p = jnp.exp(s - m)
        l = l + jnp.sum(p, axis=-1, keepdims=True)
        o = o + jnp.dot(p.astype(bf16), v, preferred_element_type=f32)
    return o / l


def _band_sample_body(q_ref, k_ref, v_ref, z_ref, ck_ref, cv_ref, bc_ref, bn_ref, o_ref, ks_ref, vs_ref):
    t = q_ref.shape[0]
    _store_head_rows(ks_ref, k_ref)
    _store_head_rows(vs_ref, v_ref)
    o = _sample_attend(_stack_heads(q_ref[...]).astype(bf16), t, ck_ref, cv_ref, k_ref[...], v_ref[...],
                       HEAD_SCALE, bc_ref[...], bn_ref[...])
    for h in range(N_HEADS):
        sl = slice(h * HEAD_DIM, (h + 1) * HEAD_DIM)
        o_ref[:, sl] = (o[h * t:(h + 1) * t] * _silu(z_ref[:, sl])).astype(bf16)


def _mem_sample_body(q_ref, z_ref, ck_ref, cv_ref, o_ref):
    t = q_ref.shape[0]
    o = _sample_attend(_stack_heads(q_ref[...]).astype(bf16), t, ck_ref, cv_ref, None, None, HEAD_SCALE)
    for h in range(N_HEADS):
        sl = slice(h * HEAD_DIM, (h + 1) * HEAD_DIM)
        o_ref[:, sl] = (o[h * t:(h + 1) * t] * _silu(z_ref[:, sl])).astype(bf16)


def _cache_rows(cache):
    d0, d1, rows = cache.shape[:3]
    return cache.reshape(d0, d1, rows * N_HEADS, HEAD_DIM)


def _mem_body(q_ref, z_ref, k_ref, v_ref, o_ref):
    for h in range(N_HEADS):
        sl = slice(h * HEAD_DIM, (h + 1) * HEAD_DIM)
        q = q_ref[:, sl].astype(bf16)
        k = k_ref[:, sl].astype(bf16)
        v = v_ref[:, sl].astype(bf16)
        s = lax.dot_general(q, k, NT_DIMS, preferred_element_type=f32) * HEAD_SCALE
        p = jnp.exp(s - jnp.max(s, axis=-1, keepdims=True))
        l = jnp.sum(p, axis=-1, keepdims=True)
        o = jnp.dot(p.astype(bf16), v, preferred_element_type=f32) / l
        o_ref[:, sl] = (o * _silu(z_ref[:, sl])).astype(bf16)


def _mem_attn_prompt(u, mkv, nb, t, tq, n_mem):
    nq = t // tq

    def blk(col):
        return pl.BlockSpec((tq, BR_WIDTH), lambda b, i: (b * nq + i, col // BR_WIDTH))

    return pl.pallas_call(
        _mem_body,
        out_shape=jax.ShapeDtypeStruct((nb * t, BR_WIDTH), bf16),
        grid=(nb, nq),
        in_specs=[blk(U_MQ), blk(U_MZ),
                  pl.BlockSpec((n_mem, BR_WIDTH), lambda b, i: (b, 0)),
                  pl.BlockSpec((n_mem, BR_WIDTH), lambda b, i: (b, 1))],
        out_specs=pl.BlockSpec((tq, BR_WIDTH), lambda b, i: (b * nq + i, 0)),
        compiler_params=_params("parallel", "arbitrary"),
        name="mem_attn_prompt",
    )(u, u, mkv, mkv)


def _diff_lambda(lamp_ref, lam_init):
    lp = lamp_ref[...]
    a = jnp.sum(lp[0:1] * lp[1:2], axis=-1, keepdims=True)
    b = jnp.sum(lp[2:3] * lp[3:4], axis=-1, keepdims=True)
    return jnp.exp(a) - jnp.exp(b) + lam_init


def _diff_epilogue(o0, o1, lam, gain, z, lam_init):
    y = _head_norm(o0 - lam * o1, gain) * (1.0 - lam_init)
    return (y * _silu(z)).astype(bf16)


def _half_masks():
    lane = lax.broadcasted_iota(jnp.int32, (1, HEAD_DIM), 1)
    return lane < DIFF_DH, lane >= DIFF_DH


def _diff_prompt_body(qt_ref, kt_ref, q_ref, k_ref, v_ref, z_ref, lamp_ref, gain_ref, o_ref, ks_ref, vs_ref,
                      m_scr, l_scr, acc_scr, *, lam_init):
    tq, tk = q_ref.shape[0], k_ref.shape[0]
    i = qt_ref[pl.program_id(1)]
    j = kt_ref[pl.program_id(1)]

    @pl.when(j == 0)
    def _():
        m_scr[...] = jnp.full(m_scr.shape, NEG, f32)
        l_scr[...] = jnp.zeros(l_scr.shape, f32)
        acc_scr[...] = jnp.zeros(acc_scr.shape, f32)

    lo, hi = _half_masks()

    def step(diagonal):
        if diagonal:
            kc = lax.broadcasted_iota(jnp.int32, (tk, 2 * tq), 0) // CHUNK
            qpos = lax.broadcasted_iota(jnp.int32, (tk, 2 * tq), 1)
            qc = jnp.where(qpos >= tq, qpos - tq, qpos) // CHUNK
            mask = kc <= qc
        for h in range(N_HEADS):
            sl = slice(h * HEAD_DIM, (h + 1) * HEAD_DIM)
            q = q_ref[:, sl] * DIFF_SCALE
            qq = jnp.concatenate([jnp.where(lo, q, 0.0), jnp.where(hi, q, 0.0)], axis=0).astype(bf16)
            k = k_ref[:, sl].astype(bf16)
            vt = v_ref[:, sl].T.astype(bf16)
            s = lax.dot_general(k, qq, NT_DIMS, preferred_element_type=f32)
            if diagonal:
                s = jnp.where(mask, s, NEG)
            m_old = m_scr[h]
            m_new = jnp.maximum(m_old, jnp.max(s, axis=0, keepdims=True))
            p = jnp.exp(s - m_new)
            alpha = jnp.exp(m_old - m_new)
            l_scr[h] = alpha * l_scr[h] + jnp.sum(p, axis=0, keepdims=True)
            acc_scr[h] = alpha * acc_scr[h] + jnp.dot(vt, p.astype(bf16), preferred_element_type=f32)
            m_scr[h] = m_new

    @pl.when(j < i)
    def _():
        step(False)

    @pl.when(j == i)
    def _():
        step(True)
        _store_head_rows(ks_ref, k_ref)
        _store_head_rows(vs_ref, v_ref)
        lam = _diff_lambda(lamp_ref, lam_init)
        for h in range(N_HEADS):
            sl = slice(h * HEAD_DIM, (h + 1) * HEAD_DIM)
            on = acc_scr[h] / l_scr[h]
            o0 = on[:, :tq].T
            o1 = on[:, tq:].T
            o_ref[:, sl] = _diff_epilogue(o0, o1, lam, gain_ref[:, sl], z_ref[:, sl], lam_init)


def _diff_prompt(u, lamp, gain, lam_init, batch, seq):
    t = _divisor_tile(seq, C_BLOCK, 2 * CHUNK)
    nt = seq // t

    pairs = [(i, j) for i in range(nt) for j in range(i + 1)]
    q_tile = jnp.asarray([p[0] for p in pairs], jnp.int32)
    k_tile = jnp.asarray([p[1] for p in pairs], jnp.int32)

    def qblk(col):
        return pl.BlockSpec((t, BR_WIDTH), lambda b, p, qt, kt: (b * nt + qt[p], col // BR_WIDTH))

    def kblk(col):
        return pl.BlockSpec((t, BR_WIDTH), lambda b, p, qt, kt: (b * nt + kt[p], col // BR_WIDTH))

    state_shape = jax.ShapeDtypeStruct((batch, seq * N_HEADS, HEAD_DIM), f32)
    state_spec = pl.BlockSpec((None, t * N_HEADS, HEAD_DIM), lambda b, p, qt, kt: (b, qt[p], 0))
    return pl.pallas_call(
        functools.partial(_diff_prompt_body, lam_init=lam_init),
        out_shape=[jax.ShapeDtypeStruct((batch * seq, BR_WIDTH), bf16), state_shape, state_shape],
        grid_spec=pltpu.PrefetchScalarGridSpec(
            num_scalar_prefetch=2,
            grid=(batch, len(pairs)),
            in_specs=[qblk(U_CQ), kblk(U_CK), kblk(U_CV), qblk(U_CZ),
                      pl.BlockSpec((4, DIFF_DH), lambda b, p, qt, kt: (0, 0)),
                      pl.BlockSpec((1, BR_WIDTH), lambda b, p, qt, kt: (0, 0))],
            out_specs=[pl.BlockSpec((t, BR_WIDTH), lambda b, p, qt, kt: (b * nt + qt[p], 0)),
                       state_spec, state_spec],
            scratch_shapes=[pltpu.VMEM((N_HEADS, 1, 2 * t), f32),
                            pltpu.VMEM((N_HEADS, 1, 2 * t), f32),
                            pltpu.VMEM((N_HEADS, HEAD_DIM, 2 * t), f32)]),
        compiler_params=_params("parallel", "arbitrary"),
        name="diff_attn_prompt",
    )(q_tile, k_tile, u, u, u, u, lamp, gain.reshape(1, BR_WIDTH))


def _diff_sample_body(q_ref, k_ref, v_ref, z_ref, ck_ref, cv_ref, lamp_ref, gain_ref, o_ref, ks_ref, vs_ref,
                      *, lam_init):
    t = q_ref.shape[0]
    _store_head_rows(ks_ref, k_ref)
    _store_head_rows(vs_ref, v_ref)
    lo, hi = _half_masks()
    lam = _diff_lambda(lamp_ref, lam_init)
    q = _stack_heads(q_ref[...]) * DIFF_SCALE
    qq = jnp.concatenate([jnp.where(lo, q, 0.0), jnp.where(hi, q, 0.0)], axis=0).astype(bf16)
    o = _sample_attend(qq, t, ck_ref, cv_ref, k_ref[...], v_ref[...], 1.0)
    for h in range(N_HEADS):
        sl = slice(h * HEAD_DIM, (h + 1) * HEAD_DIM)
        o0 = o[h * t:(h + 1) * t]
        o1 = o[(N_HEADS + h) * t:(N_HEADS + h + 1) * t]
        o_ref[:, sl] = _diff_epilogue(o0, o1, lam, gain_ref[:, sl], z_ref[:, sl], lam_init)


CONV_PAD = 8


def _first_last(chunk_axis):
    if chunk_axis is None:
        def run(f):
            f()
        return run, run
    c = pl.program_id(chunk_axis)
    return pl.when(c == 0), pl.when(c == pl.num_programs(chunk_axis) - 1)


def _mlstm_body(qk_ref, v_ref, og_ref, z_ref, gc_ref, gr_ref, cw_ref, cb_ref, gbr_ref, gbc_ref, gn_ref,
                c0_ref, n0_ref, m0_ref, conv0_ref,
                out_ref, c_out_ref, n_out_ref, m_out_ref,
                cbuf, c_scr, n_scr, m_scr, *, chunk_axis=1):
    ln = qk_ref.shape[0]
    on_first, on_last = _first_last(chunk_axis)
    lo = CONV_PAD - (CONV_W - 1)

    @on_first
    def _():
        cbuf[lo:CONV_PAD, :] = conv0_ref[...]
        c_scr[...] = c0_ref[...]
        n_scr[...] = n0_ref[...]
        m_scr[...] = m0_ref[...]

    cbuf[CONV_PAD:CONV_PAD + ln, :] = qk_ref[...]
    conv = cb_ref[...] + cbuf[lo:lo + ln, :] * cw_ref[0:1, :]
    for jj in range(1, CONV_W):
        conv = conv + cbuf[lo + jj:lo + jj + ln, :] * cw_ref[jj:jj + 1, :]
    tail = cbuf[lo + ln:CONV_PAD + ln, :]
    cbuf[lo:CONV_PAD, :] = tail
    act = _silu(conv)

    gcb = gc_ref[...] + gbr_ref[...]
    grb = gr_ref[...] + gbc_ref[...]
    row = lax.broadcasted_iota(jnp.int32, (ln, ln), 0)
    col = lax.broadcasted_iota(jnp.int32, (ln, ln), 1)
    causal = col <= row

    for h in range(N_HEADS):
        sl = slice(h * HEAD_DIM, (h + 1) * HEAD_DIM)
        q = act[:, h * HEAD_DIM:(h + 1) * HEAD_DIM]
        k = act[:, BR_WIDTH + h * HEAD_DIM:BR_WIDTH + (h + 1) * HEAD_DIM] * HEAD_SCALE
        v = v_ref[:, sl]
        i_col = gcb[:, h:h + 1]
        lf_col = _log_sigmoid(gcb[:, N_HEADS + h:N_HEADS + h + 1])
        i_row = grb[h:h + 1, :]
        lf_row = _log_sigmoid(grb[N_HEADS + h:N_HEADS + h + 1, :])
        b_col = jnp.sum(jnp.where(causal, lf_row, 0.0), axis=1, keepdims=True)
        b_row = jnp.sum(jnp.where(row <= col, lf_col, 0.0), axis=0, keepdims=True)
        b_last = b_col[ln - 1:ln, :]
        m_old = m_scr[h][:, 0:1]
        c_old = c_scr[h]
        n_old = n_scr[h]

        dlog = jnp.where(causal, b_col - b_row + i_row, NEG)
        inter = b_col + m_old
        mt = jnp.maximum(inter, jnp.max(dlog, axis=1, keepdims=True))
        w = jnp.exp(dlog - mt)
        wi = jnp.exp(inter - mt)
        qb = q.astype(bf16)
        vb = v.astype(bf16)
        qk = lax.dot_general(qb, k.astype(bf16), NT_DIMS, preferred_element_type=f32) * w
        num = (jnp.dot(qk.astype(bf16), vb, preferred_element_type=f32)
               + wi * jnp.dot(qb, c_old.astype(bf16), preferred_element_type=f32))
        den = jnp.sum(qk, axis=1, keepdims=True) + wi * jnp.sum(q * n_old, axis=1, keepdims=True)
        hh = num / jnp.maximum(jnp.abs(den), jnp.exp(-mt))

        m_new = mt[ln - 1:ln, :]
        ws = jnp.exp(b_last - b_col + i_col - m_new)
        dec = jnp.exp(b_last + m_old - m_new)
        kw = k * ws
        c_scr[h] = dec * c_old + lax.dot_general(kw.astype(bf16), vb, TN_DIMS, preferred_element_type=f32)
        n_scr[h] = dec * n_old + jnp.sum(kw, axis=0, keepdims=True)
        m_scr[h] = jnp.broadcast_to(m_new, (1, HEAD_DIM))

        y = _head_norm(_sigmoid(og_ref[:, sl]) * hh, gn_ref[:, sl])
        out_ref[:, sl] = (y * _silu(z_ref[:, sl])).astype(bf16)

    @on_last
    def _():
        c_out_ref[...] = c_scr[...]
        n_out_ref[...] = n_scr[...]
        m_out_ref[...] = m_scr[...]


def _mlstm(u, gates_r, conv_w, conv_b, b_ig, b_fg, gn, c0, n0, m0, conv0, row0, ng, nc, ln):
    rb0 = row0 // ln
    gbias = jnp.concatenate([b_ig, b_fg]).astype(f32)
    gbr = jnp.zeros((1, GATE_PAD), f32).at[0, :2 * N_HEADS].set(gbias)
    gbc = gbias.reshape(2 * N_HEADS, 1)

    def blk(col, width):
        return pl.BlockSpec((ln, width), lambda g, c: (rb0 + g * nc + c, col // width))

    def const(shape):
        return pl.BlockSpec(shape, lambda g, c: (0,) * len(shape))

    def per_seq(shape):
        return pl.BlockSpec((None,) + shape, lambda g, c: (g,) + (0,) * len(shape))

    state_shapes = [jax.ShapeDtypeStruct((ng, N_HEADS, HEAD_DIM, HEAD_DIM), f32),
                    jax.ShapeDtypeStruct((ng, N_HEADS, 1, HEAD_DIM), f32),
                    jax.ShapeDtypeStruct((ng, N_HEADS, 1, HEAD_DIM), f32)]
    state_specs = [per_seq((N_HEADS, HEAD_DIM, HEAD_DIM)),
                   per_seq((N_HEADS, 1, HEAD_DIM)),
                   per_seq((N_HEADS, 1, HEAD_DIM))]
    out, c_new, n_new, m_new = pl.pallas_call(
        _mlstm_body,
        out_shape=[jax.ShapeDtypeStruct((ng * nc * ln, BR_WIDTH), bf16)] + state_shapes,
        grid=(ng, nc),
        in_specs=[blk(U_BQK, B_QK), blk(U_BV, BR_WIDTH), blk(U_BO, BR_WIDTH), blk(U_BZ, BR_WIDTH),
                  blk(U_GATES, GATE_PAD),
                  pl.BlockSpec((None, 2 * N_HEADS, ln), lambda g, c: (g * nc + c, 0, 0)),
                  const((CONV_W, B_QK)), const((1, B_QK)), const((1, GATE_PAD)), const((2 * N_HEADS, 1)),
                  const((1, BR_WIDTH))] + state_specs + [per_seq((CONV_W - 1, B_QK))],
        out_specs=[pl.BlockSpec((ln, BR_WIDTH), lambda g, c: (g * nc + c, 0))] + state_specs,
        scratch_shapes=[pltpu.VMEM((CONV_PAD + ln, B_QK), f32),
                        pltpu.VMEM((N_HEADS, HEAD_DIM, HEAD_DIM), f32),
                        pltpu.VMEM((N_HEADS, 1, HEAD_DIM), f32),
                        pltpu.VMEM((N_HEADS, 1, HEAD_DIM), f32)],
        compiler_params=_params("parallel", "arbitrary"),
        name="mlstm",
    )(u, u, u, u, u, gates_r, conv_w, conv_b.reshape(1, B_QK), gbr, gbc, gn.reshape(1, BR_WIDTH),
      c0, n0.reshape(ng, N_HEADS, 1, HEAD_DIM),
      jnp.broadcast_to(m0[:, :, None, None], (ng, N_HEADS, 1, HEAD_DIM)), conv0)
    return out, c_new, n_new[:, :, 0, :], m_new[:, :, 0, 0]


def _ret_body(q_ref, k_ref, v_ref, z_ref, cc_ref, ss_ref, gn_ref, s0_ref, out_ref, s_out_ref, s_scr,
              *, chunk_axis=1):
    ln = q_ref.shape[0]
    on_first, on_last = _first_last(chunk_axis)

    @on_first
    def _():
        s_scr[...] = s0_ref[...]

    row = lax.broadcasted_iota(jnp.int32, (ln, ln), 0)
    col = lax.broadcasted_iota(jnp.int32, (ln, ln), 1)
    rel = (row - col).astype(f32)
    tpos = lax.broadcasted_iota(jnp.int32, (ln, 1), 0).astype(f32)
    cc = cc_ref[...]
    ss = ss_ref[...]
    for h in range(N_HEADS):
        sl = slice(h * HEAD_DIM, (h + 1) * HEAD_DIM)
        lg = math.log(1.0 - 2.0 ** (-RET_GAMMA_EXP0 - h))
        q = q_ref[:, sl]
        k = k_ref[:, sl]
        qr = q * cc + pltpu.roll(q, DIFF_DH, 1) * ss
        kr = (k * cc + pltpu.roll(k, DIFF_DH, 1) * ss) * HEAD_SCALE
        vb = v_ref[:, sl].astype(bf16)
        decay = jnp.where(rel >= 0.0, jnp.exp(jnp.maximum(rel, 0.0) * lg), 0.0)
        att = lax.dot_general(qr.astype(bf16), kr.astype(bf16), NT_DIMS, preferred_element_type=f32) * decay
        q_dec = qr * jnp.exp((tpos + 1.0) * lg)
        s_old = s_scr[h]
        o = (jnp.dot(att.astype(bf16), vb, preferred_element_type=f32)
             + jnp.dot(q_dec.astype(bf16), s_old.astype(bf16), preferred_element_type=f32))
        k_dec = kr * jnp.exp((ln - 1.0 - tpos) * lg)
        s_scr[h] = math.exp(ln * lg) * s_old + lax.dot_general(k_dec.astype(bf16), vb, TN_DIMS,
                                                               preferred_element_type=f32)
        out_ref[:, sl] = (_head_norm(o, gn_ref[:, sl]) * _silu(z_ref[:, sl])).astype(bf16)

    @on_last
    def _():
        s_out_ref[...] = s_scr[...]


def _rope_tables(pos):
    inv = ROPE_BASE ** (-jnp.arange(0, HEAD_DIM, 2, dtype=f32) / HEAD_DIM)
    ang = pos.astype(f32)[:, None] * inv[None, :]
    cos, sin = jnp.cos(ang), jnp.sin(ang)
    return jnp.concatenate([cos, cos], axis=-1), jnp.concatenate([-sin, sin], axis=-1)


def _retention(u, pos, gn, s0, row0, ng, nc, ln):
    rb0 = row0 // ln
    cc, ss = _rope_tables(pos)

    def blk(col):
        return pl.BlockSpec((ln, BR_WIDTH), lambda g, c: (rb0 + g * nc + c, col // BR_WIDTH))

    rope_spec = pl.BlockSpec((ln, HEAD_DIM), lambda g, c: (c, 0))
    state_spec = pl.BlockSpec((None, N_HEADS, HEAD_DIM, HEAD_DIM), lambda g, c: (g, 0, 0, 0))
    return pl.pallas_call(
        _ret_body,
        out_shape=[jax.ShapeDtypeStruct((ng * nc * ln, BR_WIDTH), bf16),
                   jax.ShapeDtypeStruct((ng, N_HEADS, HEAD_DIM, HEAD_DIM), f32)],
        grid=(ng, nc),
        in_specs=[blk(U_DQ), blk(U_DK), blk(U_DV), blk(U_DZ), rope_spec, rope_spec,
                  pl.BlockSpec((1, BR_WIDTH), lambda g, c: (0, 0)), state_spec],
        out_specs=[pl.BlockSpec((ln, BR_WIDTH), lambda g, c: (g * nc + c, 0)), state_spec],
        scratch_shapes=[pltpu.VMEM((N_HEADS, HEAD_DIM, HEAD_DIM), f32)],
        compiler_params=_params("parallel", "arbitrary"),
        name="retention",
    )(u, u, u, u, cc, ss, gn.reshape(1, BR_WIDTH), s0)


def _sample_branches_body(*refs, lam_init):
    it = iter(refs)

    def take(n):
        return [next(it) for _ in range(n)]

    a_in, c_in, m_in, b_in, d_in = take(8), take(8), take(4), take(15), take(8)
    a_out, c_out, m_out, b_out, d_out = take(3), take(3), take(1), take(4), take(2)
    b_scr, d_scr = take(4), take(1)
    _band_sample_body(*a_in, *a_out)
    _diff_sample_body(*c_in, *c_out, lam_init=lam_init)
    _mem_sample_body(*m_in, *m_out)
    _mlstm_body(*b_in, *b_out, *b_scr, chunk_axis=None)
    _ret_body(*d_in, *d_out, *d_scr, chunk_axis=None)


def _sample_branches(u, layer, caches, rel_bias, lamp, lam_init, gains, conv_w, conv_b, b_ig, b_fg,
                     states, pos, nb, t):
    ca_k, ca_v, cc_k, cc_v, cm_k, cm_v = caches
    subln_c, gn_b, gn_d = gains
    c0, n0, m0, conv0, s0 = states
    nrow = ca_k.shape[2] // N_HEADS
    bias_c = _rel_bias_toeplitz(rel_bias, t, nrow, nrow)
    bias_c = jnp.repeat(bias_c, N_HEADS, axis=2).reshape(N_HEADS * t, nrow * N_HEADS)
    bias_n = jnp.tile(_rel_bias_toeplitz(rel_bias, t, t, 0), (1, 1, N_HEADS)).reshape(N_HEADS * t, N_HEADS * t)
    gbias = jnp.concatenate([b_ig, b_fg]).astype(f32)
    gbr = jnp.zeros((1, GATE_PAD), f32).at[0, :2 * N_HEADS].set(gbias)
    cc, ss = _rope_tables(pos)

    def blk(col, width=BR_WIDTH):
        return pl.BlockSpec((t, width), lambda b: (b, col // width))

    def const(shape):
        return pl.BlockSpec(shape, lambda b: (0,) * len(shape))

    def per_seq(shape):
        return pl.BlockSpec((None,) + shape, lambda b: (b,) + (0,) * len(shape))

    def cache(arr):
        return pl.BlockSpec((None, None) + arr.shape[2:], lambda b: (layer, b, 0, 0))

    mat, vec = (N_HEADS, HEAD_DIM, HEAD_DIM), (N_HEADS, 1, HEAD_DIM)
    kv_state = (t * N_HEADS, HEAD_DIM)
    in_specs = (
        [blk(U_AQ), blk(U_AK), blk(U_AV), blk(U_AZ), cache(ca_k), cache(ca_v),
         const((N_HEADS * t, nrow * N_HEADS)), const((N_HEADS * t, N_HEADS * t))]
        + [blk(U_CQ), blk(U_CK), blk(U_CV), blk(U_CZ), cache(cc_k), cache(cc_v),
           const((4, DIFF_DH)), const((1, BR_WIDTH))]
        + [blk(U_MQ), blk(U_MZ), cache(cm_k), cache(cm_v)]
        + [blk(U_BQK, B_QK), blk(U_BV), blk(U_BO), blk(U_BZ), blk(U_GATES, GATE_PAD),
           per_seq((2 * N_HEADS, t)), const((CONV_W, B_QK)), const((1, B_QK)), const((1, GATE_PAD)),
           const((2 * N_HEADS, 1)), const((1, BR_WIDTH)), per_seq(mat), per_seq(vec), per_seq(vec),
           per_seq((CONV_W - 1, B_QK))]
        + [blk(U_DQ), blk(U_DK), blk(U_DV), blk(U_DZ), const((t, HEAD_DIM)), const((t, HEAD_DIM)),
           const((1, BR_WIDTH)), per_seq(mat)])
    args = (
        [u, u, u, u, ca_k, ca_v, bias_c, bias_n]
        + [u, u, u, u, cc_k, cc_v, lamp, subln_c.reshape(1, BR_WIDTH)]
        + [u, u, cm_k, cm_v]
        + [u, u, u, u, u, _gates_rowform(u, 0, nb * t, t), conv_w, conv_b.reshape(1, B_QK), gbr,
           gbias.reshape(2 * N_HEADS, 1), gn_b.reshape(1, BR_WIDTH), c0,
           n0.reshape(nb, N_HEADS, 1, HEAD_DIM),
           jnp.broadcast_to(m0[:, :, None, None], (nb, N_HEADS, 1, HEAD_DIM)), conv0]
        + [u, u, u, u, cc, ss, gn_d.reshape(1, BR_WIDTH), s0])
    branch_out = jax.ShapeDtypeStruct((nb * t, BR_WIDTH), bf16)
    kv_out = jax.ShapeDtypeStruct((nb,) + kv_state, f32)
    mat_out = jax.ShapeDtypeStruct((nb,) + mat, f32)
    vec_out = jax.ShapeDtypeStruct((nb,) + vec, f32)
    out_shape = ([branch_out, kv_out, kv_out] * 2 + [branch_out]
                 + [branch_out, mat_out, vec_out, vec_out] + [branch_out, mat_out])
    o_spec = pl.BlockSpec((t, BR_WIDTH), lambda b: (b, 0))
    out_specs = ([o_spec, per_seq(kv_state), per_seq(kv_state)] * 2 + [o_spec]
                 + [o_spec, per_seq(mat), per_seq(vec), per_seq(vec)] + [o_spec, per_seq(mat)])
    (oa, ak, av, oc, ck, cv, om, ob, c_new, n_new, m_new, od, s_new) = pl.pallas_call(
        functools.partial(_sample_branches_body, lam_init=lam_init),
        out_shape=out_shape,
        grid=(nb,),
        in_specs=in_specs,
        out_specs=out_specs,
        scratch_shapes=[pltpu.VMEM((CONV_PAD + t, B_QK), f32), pltpu.VMEM(mat, f32),
                        pltpu.VMEM(vec, f32), pltpu.VMEM(vec, f32), pltpu.VMEM(mat, f32)],
        compiler_params=_params("arbitrary"),
        name="sample_branches",
    )(*args)
    return ((oa, ob, oc, od, om), (ak, av, ck, cv),
            (c_new, n_new[:, :, 0, :], m_new[:, :, 0, 0], s_new))


def _merge_body(x_ref, g_ref, oa_ref, ob_ref, oc_ref, od_ref, om_ref,
                wg0_ref, wg1_ref, wg2_ref, wg3_ref, wg4_ref, wb_ref, wo_ref, fg_ref,
                wo_last_ref, y_ref, h_scr, mrg_scr, *, rows, final):
    n = pl.program_id(1)

    @pl.when(n == 0)
    def _():
        _rmsnorm_rows(x_ref, g_ref, h_scr, rows)
        y_ref[...] = jnp.zeros(y_ref.shape, f32)
        mrg_scr[...] = jnp.zeros(mrg_scr.shape, bf16)

    y_ref[...] += jnp.dot(mrg_scr[...], wo_ref[...], preferred_element_type=f32)
    h = h_scr[...]
    merged = None
    for i, (o_ref, wg_ref) in enumerate(zip((oa_ref, ob_ref, oc_ref, od_ref, om_ref),
                                            (wg0_ref, wg1_ref, wg2_ref, wg3_ref, wg4_ref))):
        gate = _sigmoid(lax.dot_general(h, wg_ref[...], NT_DIMS, preferred_element_type=f32))
        term = gate * jnp.dot(o_ref[...], wb_ref[i], preferred_element_type=f32)
        merged = term if merged is None else merged + term
    mrg_scr[...] = merged.astype(bf16)

    @pl.when(n == pl.num_programs(1) - 1)
    def _():
        y_ref[...] += jnp.dot(mrg_scr[...], wo_last_ref[...], preferred_element_type=f32)
        tm = x_ref.shape[0]
        for r in range(0, tm, rows):
            y = x_ref[r:r + rows, :] + y_ref[r:r + rows, :]
            if final:
                y = (y * lax.rsqrt(jnp.mean(y * y, axis=-1, keepdims=True) + EPS)) * fg_ref[...]
            y_ref[r:r + rows, :] = y


def _merge(x, g, outs, wg, wb, wo, fg, *, final, name, tm_target=704, tn=256):
    m, d = x.shape
    tm = _divisor_tile(m, tm_target, 16)
    rows = _divisor_tile(tm, 256, 8)
    nn = d // tn

    def wg_spec(i):
        return pl.BlockSpec((tn, d), lambda r, n: (i * nn + n, 0))

    o_spec = pl.BlockSpec((tm, BR_WIDTH), lambda r, n: (r, 0))
    return pl.pallas_call(
        functools.partial(_merge_body, rows=rows, final=final),
        out_shape=jax.ShapeDtypeStruct((m, d), f32),
        grid=(m // tm, nn),
        in_specs=[pl.BlockSpec((tm, d), lambda r, n: (r, 0)),
                  pl.BlockSpec((1, d), lambda r, n: (0, 0))]
                 + [o_spec] * N_BRANCH
                 + [wg_spec(i) for i in range(N_BRANCH)]
                 + [pl.BlockSpec((N_BRANCH, BR_WIDTH, tn), lambda r, n: (0, 0, n)),
                    pl.BlockSpec((tn, d), lambda r, n: (jnp.maximum(n - 1, 0), 0)),
                    pl.BlockSpec((1, d), lambda r, n: (0, 0)),
                    pl.BlockSpec((tn, d), lambda r, n: (nn - 1, 0))],
        out_specs=pl.BlockSpec((tm, d), lambda r, n: (r, 0)),
        scratch_shapes=[pltpu.VMEM((tm, d), bf16), pltpu.VMEM((tm, tn), bf16)],
        compiler_params=_params("parallel", "arbitrary"),
        name=name,
    )(x, g.reshape(1, d), *outs, wg, wg, wg, wg, wg, wb, wo, fg.reshape(1, d), wo)


F32_SUBLANES = 8


def _cast_rows_body(src_ref, o_ref, *, valid_rows_last):
    tr = o_ref.shape[0]
    x = src_ref[0]
    if valid_rows_last is not None:
        row = lax.broadcasted_iota(jnp.int32, (tr, 1), 0)
        keep = jnp.logical_or(pl.program_id(0) < pl.num_programs(0) - 1, row < valid_rows_last)
        x = jnp.where(keep, x, 0.0)
    o_ref[...] = x.astype(bf16)


def _cast_rows(w_t, layer, *, n_rows, tr, src_row, valid_rows_last, name):
    d = w_t.shape[2]
    assert n_rows % tr == 0
    return pl.pallas_call(
        functools.partial(_cast_rows_body, valid_rows_last=valid_rows_last),
        out_shape=jax.ShapeDtypeStruct((n_rows, d), bf16),
        grid=(n_rows // tr,),
        in_specs=[pl.BlockSpec((pl.Element(1), pl.Element(tr), pl.Element(d)),
                               lambda j: (layer, pl.multiple_of(src_row(j), F32_SUBLANES), 0))],
        out_specs=pl.BlockSpec((tr, d), lambda j: (j, 0)),
        compiler_params=_params("arbitrary"),
        name=name,
    )(w_t)


def _repack_w_in(w_t, layer):
    tr = MXU_COLS
    n_plain, n_main = W_BI // tr, U_GATES // tr

    def u_src(j):
        return jnp.where(j < n_plain, j * tr, jnp.where(j < n_main, j * tr + (W_BZ - W_BI), W_BI))

    wu = _cast_rows(w_t, layer, n_rows=U_WIDTH, tr=tr, src_row=u_src, valid_rows_last=2 * N_HEADS,
                    name="repack_u")
    wg = _cast_rows(w_t, layer, n_rows=N_BRANCH * D_MODEL, tr=512, src_row=lambda j: W_GATE + j * 512,
                    valid_rows_last=None, name="repack_g")
    return wu, wg


def _gates_rowform(u, row0, nrows, ln):
    g = u[row0:row0 + nrows, U_GATES:U_GATES + 2 * N_HEADS]
    return g.reshape(nrows // ln, ln, 2 * N_HEADS).transpose(0, 2, 1)


def kernel(x_prompt, x_sample, mem_prompt, cache_a_k, cache_a_v, cache_c_k, cache_c_v, cache_mem_k, cache_mem_v, state_b_C, state_b_n, state_b_m, state_b_conv, state_d_S, norm_g, w_in, conv_w, conv_b, b_ig, b_fg, rel_bias, lam_q1, lam_k1, lam_q2, lam_k2, gn_b, subln_c, gn_d, mem_norm_g, w_mk, w_mv, w_branch, w_out, final_g):
    batch, seq, d = x_prompt.shape
    nb, t, _ = x_sample.shape
    depth = w_in.shape[0]
    n_mem = mem_prompt.shape[1]
    past = cache_c_k.shape[2]
    mp = batch * seq
    assert d == D_MODEL and seq % C_BLOCK == 0 and seq % A_QBLOCK == 0 and seq % SCAN_CHUNK == 0

    xp = x_prompt.reshape(mp, d)
    xs = x_sample.reshape(nb * t, d)
    mem = mem_prompt.reshape(batch * n_mem, d)
    ca_k, ca_v = _cache_rows(cache_a_k), _cache_rows(cache_a_v)
    cc_k, cc_v = _cache_rows(cache_c_k), _cache_rows(cache_c_v)
    cm_k, cm_v = _cache_rows(cache_mem_k), _cache_rows(cache_mem_v)
    pos_p = jnp.arange(seq)
    pos_s = past + jnp.arange(t)
    a_rows = min(A_WINDOW, seq)
    mem_tq = _divisor_tile(seq, 512, 16)

    w_t = jnp.swapaxes(w_in, 1, 2)

    sp, ss = [], []
    for l in range(depth):
        lam_init = 0.8 - 0.6 * math.exp(-0.3 * l)
        lamp = jnp.stack([lam_q1[l], lam_k1[l], lam_q2[l], lam_k2[l]]).astype(f32)
        wu, wg = _repack_w_in(w_t, l)
        u = _norm_matmul(xp, norm_g[l], wu, tm_target=1024, tn=U_TILE, w_rows_are_outputs=True,
                         name="in_proj")
        us = _norm_matmul(xs, norm_g[l], wu, tm_target=1024, tn=U_TILE, w_rows_are_outputs=True,
                          name="in_proj_sample")
        mkv = _norm_matmul(mem, mem_norm_g[l], jnp.concatenate([w_mk[l], w_mv[l]], axis=1).astype(bf16),
                           tm_target=512, tn=512, w_rows_are_outputs=False, name="mem_proj")

        oa_p, ak_p, av_p = _band_prompt(u, rel_bias[l], batch, seq, a_rows)
        zeros_c = jnp.zeros((batch, N_HEADS, HEAD_DIM, HEAD_DIM), f32)
        ob_p, c_p, n_p, m_p = _mlstm(
            u, _gates_rowform(u, 0, mp, SCAN_CHUNK), conv_w[l], conv_b[l], b_ig[l], b_fg[l], gn_b[l],
            zeros_c, jnp.zeros((batch, N_HEADS, HEAD_DIM), f32), jnp.zeros((batch, N_HEADS), f32),
            jnp.zeros((batch, CONV_W - 1, B_QK), f32), 0, batch, seq // SCAN_CHUNK, SCAN_CHUNK)
        oc_p, ck_p, cv_p = _diff_prompt(u, lamp, subln_c[l], lam_init, batch, seq)
        od_p, s_p = _retention(u, pos_p, gn_d[l], zeros_c, 0, batch, seq // SCAN_CHUNK, SCAN_CHUNK)
        om_p = _mem_attn_prompt(u, mkv, batch, seq, mem_tq, n_mem)

        outs_s, (ak_s, av_s, ck_s, cv_s), (c_s, n_s, m_s, s_s) = _sample_branches(
            us, l, (ca_k, ca_v, cc_k, cc_v, cm_k, cm_v), rel_bias[l], lamp, lam_init,
            (subln_c[l], gn_b[l], gn_d[l]), conv_w[l], conv_b[l], b_ig[l], b_fg[l],
            (state_b_C[l].astype(f32), state_b_n[l].astype(f32), state_b_m[l].astype(f32),
             state_b_conv[l].astype(f32), state_d_S[l].astype(f32)), pos_s, nb, t)

        wb, wo = w_branch[l].astype(bf16), w_out[l].astype(bf16)
        final = l == depth - 1
        xp = _merge(xp, norm_g[l], (oa_p, ob_p, oc_p, od_p, om_p), wg, wb, wo, final_g, final=final,
                    name="gated_merge")
        xs = _merge(xs, norm_g[l], outs_s, wg, wb, wo, final_g, final=final, name="gated_merge_sample")

        def conv_tail(arr, n_seq, rows):
            return jnp.stack([arr[(i + 1) * rows - (CONV_W - 1):(i + 1) * rows, U_BQK:U_BQK + B_QK]
                              for i in range(n_seq)])

        def hd(a):
            return a.reshape(a.shape[:2] + (N_HEADS, HEAD_DIM))

        def rows_hd(a):
            return a.reshape(a.shape[0], a.shape[1] // N_HEADS, N_HEADS, HEAD_DIM)

        sp.append((rows_hd(ak_p), rows_hd(av_p), rows_hd(ck_p), rows_hd(cv_p),
                   hd(mkv[:, :BR_WIDTH].reshape(batch, n_mem, BR_WIDTH)),
                   hd(mkv[:, BR_WIDTH:].reshape(batch, n_mem, BR_WIDTH)),
                   c_p, n_p, m_p, conv_tail(u, batch, seq), s_p))
        ss.append((rows_hd(ak_s), rows_hd(av_s), rows_hd(ck_s), rows_hd(cv_s),
                   c_s, n_s, m_s, us[:, U_BQK:U_BQK + B_QK].reshape(nb, t, B_QK)[:, t - (CONV_W - 1):], s_s))

    y_prompt = xp.reshape(batch, seq, d)
    y_sample = xs.reshape(nb, t, d)
    p_states = tuple(jnp.stack([st[i] for st in sp]) for i in range(11))
    s_states = tuple(jnp.stack([st[i] for st in ss]) for i in range(9))
    return (y_prompt, y_sample) + p_states + s_states
```

```python
import functools
import math

import jax
import jax.numpy as jnp
import numpy as np
from jax import lax
from jax.experimental import pallas as pl
from jax.experimental.pallas import tpu as pltpu

f32 = jnp.float32
bf16 = jnp.bfloat16

D_MODEL = 2048
N_HEADS = 4
HEAD_DIM = 128
BR_WIDTH = N_HEADS * HEAD_DIM
N_BRANCH = 5
CHUNK = 64
A_PREV_CHUNKS = 8
A_WINDOW = A_PREV_CHUNKS * CHUNK
A_BAND = (A_PREV_CHUNKS + 1) * CHUNK
A_REL_CLIP = 128
CONV_W = 4
B_QK = 2 * BR_WIDTH
DIFF_DH = HEAD_DIM // 2
RET_GAMMA_EXP0 = 5.0
ROPE_BASE = 10000.0
EPS = 1e-6
NEG = -1e30
HEAD_SCALE = HEAD_DIM ** -0.5
DIFF_SCALE = DIFF_DH ** -0.5
LOG2_E = math.log2(math.e)

U_AQ, U_AK, U_AV, U_AZ = 0, 512, 1024, 1536
U_BQK, U_BV, U_BO, U_BZ = 2048, 3072, 3584, 4096
U_CQ, U_CK, U_CV, U_CZ = 4608, 5120, 5632, 6144
U_DQ, U_DK, U_DV, U_DZ = 6656, 7168, 7680, 8192
U_MQ, U_MZ = 8704, 9216
U_GATES = 9728
GATE_PAD = 128
MXU_COLS = 256
U_WIDTH = U_GATES + MXU_COLS
U_TILE = 3 * MXU_COLS
W_BI = 4096
W_BZ = 4104
W_GATE = 9736

VMEM_LIMIT_BYTES = 56 * 1024 * 1024
A_QBLOCK = 4 * CHUNK
C_BLOCK = 512
SCAN_CHUNK = 256
NT_DIMS = (((1,), (1,)), ((), ()))
TN_DIMS = (((0,), (0,)), ((), ()))


def _divisor_tile(n, target, multiple):
    best = None
    for t in range(multiple, min(n, target) + 1, multiple):
        if n % t == 0:
            best = t
    assert best is not None, (n, target, multiple)
    return best


def _params(*sem):
    return pltpu.CompilerParams(dimension_semantics=sem, vmem_limit_bytes=VMEM_LIMIT_BYTES)


def _sigmoid(x):
    return 1.0 / (1.0 + jnp.exp(-x))


def _silu(x):
    return x * _sigmoid(x)


def _log_sigmoid(x):
    return jnp.minimum(x, 0.0) - jnp.log1p(jnp.exp(-jnp.abs(x)))


def _head_norm(y, gain):
    return y * lax.rsqrt(jnp.mean(y * y, axis=-1, keepdims=True) + EPS) * gain


def _rmsnorm_rows(x_ref, g_ref, h_ref, rows):
    tm = x_ref.shape[0]
    for r in range(0, tm, rows):
        x = x_ref[r:r + rows, :]
        ms = jnp.mean(x * x, axis=-1, keepdims=True)
        h_ref[r:r + rows, :] = ((x * lax.rsqrt(ms + EPS)) * g_ref[...]).astype(bf16)


def _norm_matmul_body(x_ref, g_ref, w_ref, o_ref, h_ref, *, rows, w_rows_are_outputs):
    @pl.when(pl.program_id(1) == 0)
    def _():
        _rmsnorm_rows(x_ref, g_ref, h_ref, rows)

    if w_rows_are_outputs:
        o = lax.dot_general(h_ref[...], w_ref[...], NT_DIMS, preferred_element_type=f32)
    else:
        o = jnp.dot(h_ref[...], w_ref[...], preferred_element_type=f32)
    o_ref[...] = o.astype(o_ref.dtype)


def _norm_matmul(x, g, w, *, tm_target, tn, w_rows_are_outputs, name):
    m, d = x.shape
    n = w.shape[0] if w_rows_are_outputs else w.shape[1]
    tm = _divisor_tile(m, tm_target, 16)
    rows = _divisor_tile(tm, 256, 8)
    assert n % tn == 0
    w_spec = (pl.BlockSpec((tn, d), lambda i, j: (j, 0)) if w_rows_are_outputs
              else pl.BlockSpec((d, tn), lambda i, j: (0, j)))
    return pl.pallas_call(
        functools.partial(_norm_matmul_body, rows=rows, w_rows_are_outputs=w_rows_are_outputs),
        out_shape=jax.ShapeDtypeStruct((m, n), f32),
        grid=(m // tm, n // tn),
        in_specs=[pl.BlockSpec((tm, d), lambda i, j: (i, 0)),
                  pl.BlockSpec((1, d), lambda i, j: (0, 0)),
                  w_spec],
        out_specs=pl.BlockSpec((tm, tn), lambda i, j: (i, j)),
        scratch_shapes=[pltpu.VMEM((tm, d), bf16)],
        compiler_params=_params("parallel", "arbitrary"),
        name=name,
    )(x, g.reshape(1, d), w)


def _store_head_rows(dst_ref, src_ref):
    rows = src_ref.shape[0]
    for h in range(N_HEADS):
        dst_ref[pl.ds(h, rows, stride=N_HEADS), :] = src_ref[:, h * HEAD_DIM:(h + 1) * HEAD_DIM]


def _band_prompt_body(q_ref, k0_ref, k1_ref, k2_ref, v0_ref, v1_ref, v2_ref, z_ref, bias_ref,
                      o_ref, ks_ref, vs_ref, *, n_tail):
    tq = q_ref.shape[0]
    i = pl.program_id(1)

    @pl.when(i >= pl.num_programs(1) - n_tail)
    def _():
        _store_head_rows(ks_ref, k2_ref)
        _store_head_rows(vs_ref, v2_ref)

    w_idx = lax.broadcasted_iota(jnp.int32, (tq, 3 * tq), 1)
    valid = (w_idx + (i - 2) * tq) >= 0
    for h in range(N_HEADS):
        sl = slice(h * HEAD_DIM, (h + 1) * HEAD_DIM)
        q = q_ref[:, sl].astype(bf16)
        kw = jnp.concatenate([k0_ref[:, sl], k1_ref[:, sl], k2_ref[:, sl]], axis=0).astype(bf16)
        vw = jnp.concatenate([v0_ref[:, sl], v1_ref[:, sl], v2_ref[:, sl]], axis=0).astype(bf16)
        s = lax.dot_general(q, kw, NT_DIMS, preferred_element_type=f32) * HEAD_SCALE + bias_ref[h]
        s = jnp.where(valid, s, NEG)
        p = jnp.exp(s - jnp.max(s, axis=-1, keepdims=True))
        l = jnp.sum(p, axis=-1, keepdims=True)
        o = jnp.dot(p.astype(bf16), vw, preferred_element_type=f32) / l
        o_ref[:, sl] = (o * _silu(z_ref[:, sl])).astype(bf16)


def _rel_bias_toeplitz(rel_bias, rows, cols, rel0):
    p = rows + cols
    j = np.arange(p)
    d = np.where(j < cols, j, j - p)
    idx = np.clip(rel0 - d, -A_REL_CLIP, A_REL_CLIP) + A_REL_CLIP
    v = rel_bias.astype(f32)[:, idx]
    flat = jnp.tile(v, (1, rows))[:, :rows * (p - 1)]
    return flat.reshape(rel_bias.shape[0], rows, p - 1)[:, :, :cols]


def _band_bias_prompt(rel_bias):
    r = np.arange(A_QBLOCK)[:, None]
    w = np.arange(3 * A_QBLOCK)[None, :]
    kj = w - CHUNK * (r // CHUNK)
    inside = (kj >= 0) & (kj < A_BAND)
    table = _rel_bias_toeplitz(rel_bias, A_QBLOCK, 3 * A_QBLOCK, 2 * A_QBLOCK)
    return jnp.where(jnp.asarray(inside)[None], table, NEG)


def _band_prompt(u, rel_bias, batch, seq, a_rows):
    tq = A_QBLOCK
    nqb = seq // tq
    assert a_rows % tq == 0
    n_tail = a_rows // tq
    bias = _band_bias_prompt(rel_bias)

    def blk(col, back):
        return pl.BlockSpec((tq, BR_WIDTH),
                            lambda b, i: (b * nqb + jnp.maximum(i - back, 0), col // BR_WIDTH))

    state_shape = jax.ShapeDtypeStruct((batch, a_rows * N_HEADS, HEAD_DIM), f32)
    state_spec = pl.BlockSpec((None, tq * N_HEADS, HEAD_DIM),
                              lambda b, i: (b, jnp.maximum(i - (nqb - n_tail), 0), 0))
    return pl.pallas_call(
        functools.partial(_band_prompt_body, n_tail=n_tail),
        out_shape=[jax.ShapeDtypeStruct((batch * seq, BR_WIDTH), bf16), state_shape, state_shape],
        grid=(batch, nqb),
        in_specs=[blk(U_AQ, 0),
                  blk(U_AK, 2), blk(U_AK, 1), blk(U_AK, 0),
                  blk(U_AV, 2), blk(U_AV, 1), blk(U_AV, 0),
                  blk(U_AZ, 0),
                  pl.BlockSpec((N_HEADS, tq, 3 * tq), lambda b, i: (0, 0, 0))],
        out_specs=[pl.BlockSpec((tq, BR_WIDTH), lambda b, i: (b * nqb + i, 0)), state_spec, state_spec],
        compiler_params=_params("parallel", "arbitrary"),
        name="band_attn_prompt",
    )(u, u, u, u, u, u, u, u, bias)


def _stack_heads(x):
    return jnp.concatenate([x[:, h * HEAD_DIM:(h + 1) * HEAD_DIM] for h in range(N_HEADS)], axis=0)


def _sample_attend(qs, t, ck_ref, cv_ref, k_new, v_new, scale, bias_c=None, bias_n=None):
    rows = qs.shape[0]
    row_head = (lax.broadcasted_iota(jnp.int32, (rows, 1), 0) // t) % N_HEADS
    ck = ck_ref[...].astype(bf16)
    s = lax.dot_general(qs, ck, NT_DIMS, preferred_element_type=f32) * scale
    if bias_c is not None:
        s = s + bias_c
    col_head = lax.broadcasted_iota(jnp.int32, (1, ck.shape[0]), 1) % N_HEADS
    parts = [(jnp.where(col_head == row_head, s, NEG), cv_ref[...].astype(bf16))]
    if k_new is not None:
        s = lax.dot_general(qs, _stack_heads(k_new).astype(bf16), NT_DIMS, preferred_element_type=f32) * scale
        if bias_n is not None:
            s = s + bias_n
        col_head = lax.broadcasted_iota(jnp.int32, (1, N_HEADS * t), 1) // t
        parts.append((jnp.where(col_head == row_head, s, NEG), _stack_heads(v_new).astype(bf16)))
    m = functools.reduce(jnp.maximum, [jnp.max(s, axis=-1, keepdims=True) for s, _ in parts])
    l, o = 0.0, 0.0
    for s, v in parts:
        p = jnp.exp(s - m)
        l = l + jnp.sum(p, axis=-1, keepdims=True)
        o = o + jnp.dot(p.astype(bf16), v, preferred_element_type=f32)
    return o / l


def _band_sample_body(q_ref, k_ref, v_ref, z_ref, ck_ref, cv_ref, bc_ref, bn_ref, o_ref, ks_ref, vs_ref):
    t = q_ref.shape[0]
    _store_head_rows(ks_ref, k_ref)
    _store_head_rows(vs_ref, v_ref)
    o = _sample_attend(_stack_heads(q_ref[...]).astype(bf16), t, ck_ref, cv_ref, k_ref[...], v_ref[...],
                       HEAD_SCALE, bc_ref[...], bn_ref[...])
    for h in range(N_HEADS):
        sl = slice(h * HEAD_DIM, (h + 1) * HEAD_DIM)
        o_ref[:, sl] = (o[h * t:(h + 1) * t] * _silu(z_ref[:, sl])).astype(bf16)


def _mem_sample_body(q_ref, z_ref, ck_ref, cv_ref, o_ref):
    t = q_ref.shape[0]
    o = _sample_attend(_stack_heads(q_ref[...]).astype(bf16), t, ck_ref, cv_ref, None, None, HEAD_SCALE)
    for h in range(N_HEADS):
        sl = slice(h * HEAD_DIM, (h + 1) * HEAD_DIM)
        o_ref[:, sl] = (o[h * t:(h + 1) * t] * _silu(z_ref[:, sl])).astype(bf16)


def _cache_rows(cache):
    d0, d1, rows = cache.shape[:3]
    return cache.reshape(d0, d1, rows * N_HEADS, HEAD_DIM)


def _mem_body(q_ref, z_ref, k_ref, v_ref, o_ref):
    for h in range(N_HEADS):
        sl = slice(h * HEAD_DIM, (h + 1) * HEAD_DIM)
        q = q_ref[:, sl].astype(bf16)
        k = k_ref[:, sl].astype(bf16)
        v = v_ref[:, sl].astype(bf16)
        s = lax.dot_general(q, k, NT_DIMS, preferred_element_type=f32) * HEAD_SCALE
        p = jnp.exp(s - jnp.max(s, axis=-1, keepdims=True))
        l = jnp.sum(p, axis=-1, keepdims=True)
        o = jnp.dot(p.astype(bf16), v, preferred_element_type=f32) / l
        o_ref[:, sl] = (o * _silu(z_ref[:, sl])).astype(bf16)


def _mem_attn_prompt(u, mkv, nb, t, tq, n_mem):
    nq = t // tq

    def blk(col):
        return pl.BlockSpec((tq, BR_WIDTH), lambda b, i: (b * nq + i, col // BR_WIDTH))

    return pl.pallas_call(
        _mem_body,
        out_shape=jax.ShapeDtypeStruct((nb * t, BR_WIDTH), bf16),
        grid=(nb, nq),
        in_specs=[blk(U_MQ), blk(U_MZ),
                  pl.BlockSpec((n_mem, BR_WIDTH), lambda b, i: (b, 0)),
                  pl.BlockSpec((n_mem, BR_WIDTH), lambda b, i: (b, 1))],
        out_specs=pl.BlockSpec((tq, BR_WIDTH), lambda b, i: (b * nq + i, 0)),
        compiler_params=_params("parallel", "arbitrary"),
        name="mem_attn_prompt",
    )(u, u, mkv, mkv)


def _diff_lambda(lamp_ref, lam_init):
    lp = lamp_ref[...]
    a = jnp.sum(lp[0:1] * lp[1:2], axis=-1, keepdims=True)
    b = jnp.sum(lp[2:3] * lp[3:4], axis=-1, keepdims=True)
    return jnp.exp(a) - jnp.exp(b) + lam_init


def _diff_epilogue(o0, o1, lam, gain, z, lam_init):
    y = _head_norm(o0 - lam * o1, gain) * (1.0 - lam_init)
    return (y * _silu(z)).astype(bf16)


def _half_masks():
    lane = lax.broadcasted_iota(jnp.int32, (1, HEAD_DIM), 1)
    return lane < DIFF_DH, lane >= DIFF_DH


def _diff_prompt_body(qt_ref, kt_ref, q_ref, k_ref, v_ref, z_ref, lamp_ref, gain_ref, o_ref, ks_ref, vs_ref,
                      m_scr, l_scr, acc_scr, *, lam_init):
    tq, tk = q_ref.shape[0], k_ref.shape[0]
    i = qt_ref[pl.program_id(1)]
    j = kt_ref[pl.program_id(1)]

    @pl.when(j == 0)
    def _():
        m_scr[...] = jnp.full(m_scr.shape, NEG, f32)
        l_scr[...] = jnp.zeros(l_scr.shape, f32)
        acc_scr[...] = jnp.zeros(acc_scr.shape, f32)

    lo, hi = _half_masks()

    def step(diagonal):
        if diagonal:
            kc = lax.broadcasted_iota(jnp.int32, (tk, 2 * tq), 0) // CHUNK
            qpos = lax.broadcasted_iota(jnp.int32, (tk, 2 * tq), 1)
            qc = jnp.where(qpos >= tq, qpos - tq, qpos) // CHUNK
            mask = kc <= qc
        for h in range(N_HEADS):
            sl = slice(h * HEAD_DIM, (h + 1) * HEAD_DIM)
            q = q_ref[:, sl] * (DIFF_SCALE * LOG2_E)
            qq = jnp.concatenate([jnp.where(lo, q, 0.0), jnp.where(hi, q, 0.0)], axis=0).astype(bf16)
            k = k_ref[:, sl].astype(bf16)
            vt = v_ref[:, sl].T.astype(bf16)
            s = lax.dot_general(k, qq, NT_DIMS, preferred_element_type=f32)
            if diagonal:
                s = jnp.where(mask, s, NEG)
            m_old = m_scr[h]
            m_new = jnp.maximum(m_old, jnp.max(s, axis=0, keepdims=True))
            p = jnp.exp2(s - m_new)
            alpha = jnp.exp2(m_old - m_new)
            l_scr[h] = alpha * l_scr[h] + jnp.sum(p, axis=0, keepdims=True)
            acc_scr[h] = alpha * acc_scr[h] + jnp.dot(vt, p.astype(bf16), preferred_element_type=f32)
            m_scr[h] = m_new

    @pl.when(j < i)
    def _():
        step(False)

    @pl.when(j == i)
    def _():
        step(True)
        _store_head_rows(ks_ref, k_ref)
        _store_head_rows(vs_ref, v_ref)
        lam = _diff_lambda(lamp_ref, lam_init)
        for h in range(N_HEADS):
            sl = slice(h * HEAD_DIM, (h + 1) * HEAD_DIM)
            on = acc_scr[h] / l_scr[h]
            o0 = on[:, :tq].T
            o1 = on[:, tq:].T
            o_ref[:, sl] = _diff_epilogue(o0, o1, lam, gain_ref[:, sl], z_ref[:, sl], lam_init)


def _diff_prompt(u, lamp, gain, lam_init, batch, seq):
    t = _divisor_tile(seq, C_BLOCK, 2 * CHUNK)
    nt = seq // t

    pairs = [(i, j) for i in range(nt) for j in range(i + 1)]
    q_tile = jnp.asarray([p[0] for p in pairs], jnp.int32)
    k_tile = jnp.asarray([p[1] for p in pairs], jnp.int32)

    def qblk(col):
        return pl.BlockSpec((t, BR_WIDTH), lambda b, p, qt, kt: (b * nt + qt[p], col // BR_WIDTH))

    def kblk(col):
        return pl.BlockSpec((t, BR_WIDTH), lambda b, p, qt, kt: (b * nt + kt[p], col // BR_WIDTH))

    state_shape = jax.ShapeDtypeStruct((batch, seq * N_HEADS, HEAD_DIM), f32)
    state_spec = pl.BlockSpec((None, t * N_HEADS, HEAD_DIM), lambda b, p, qt, kt: (b, qt[p], 0))
    return pl.pallas_call(
        functools.partial(_diff_prompt_body, lam_init=lam_init),
        out_shape=[jax.ShapeDtypeStruct((batch * seq, BR_WIDTH), bf16), state_shape, state_shape],
        grid_spec=pltpu.PrefetchScalarGridSpec(
            num_scalar_prefetch=2,
            grid=(batch, len(pairs)),
            in_specs=[qblk(U_CQ), kblk(U_CK), kblk(U_CV), qblk(U_CZ),
                      pl.BlockSpec((4, DIFF_DH), lambda b, p, qt, kt: (0, 0)),
                      pl.BlockSpec((1, BR_WIDTH), lambda b, p, qt, kt: (0, 0))],
            out_specs=[pl.BlockSpec((t, BR_WIDTH), lambda b, p, qt, kt: (b * nt + qt[p], 0)),
                       state_spec, state_spec],
            scratch_shapes=[pltpu.VMEM((N_HEADS, 1, 2 * t), f32),
                            pltpu.VMEM((N_HEADS, 1, 2 * t), f32),
                            pltpu.VMEM((N_HEADS, HEAD_DIM, 2 * t), f32)]),
        compiler_params=_params("parallel", "arbitrary"),
        name="diff_attn_prompt",
    )(q_tile, k_tile, u, u, u, u, lamp, gain.reshape(1, BR_WIDTH))


def _diff_sample_body(q_ref, k_ref, v_ref, z_ref, ck_ref, cv_ref, lamp_ref, gain_ref, o_ref, ks_ref, vs_ref,
                      *, lam_init):
    t = q_ref.shape[0]
    _store_head_rows(ks_ref, k_ref)
    _store_head_rows(vs_ref, v_ref)
    lo, hi = _half_masks()
    lam = _diff_lambda(lamp_ref, lam_init)
    q = _stack_heads(q_ref[...]) * DIFF_SCALE
    qq = jnp.concatenate([jnp.where(lo, q, 0.0), jnp.where(hi, q, 0.0)], axis=0).astype(bf16)
    o = _sample_attend(qq, t, ck_ref, cv_ref, k_ref[...], v_ref[...], 1.0)
    for h in range(N_HEADS):
        sl = slice(h * HEAD_DIM, (h + 1) * HEAD_DIM)
        o0 = o[h * t:(h + 1) * t]
        o1 = o[(N_HEADS + h) * t:(N_HEADS + h + 1) * t]
        o_ref[:, sl] = _diff_epilogue(o0, o1, lam, gain_ref[:, sl], z_ref[:, sl], lam_init)


CONV_PAD = 8


def _first_last(chunk_axis):
    if chunk_axis is None:
        def run(f):
            f()
        return run, run
    c = pl.program_id(chunk_axis)
    return pl.when(c == 0), pl.when(c == pl.num_programs(chunk_axis) - 1)


def _mlstm_body(qk_ref, v_ref, og_ref, z_ref, gc_ref, gr_ref, cw_ref, cb_ref, gbr_ref, gbc_ref, gn_ref,
                c0_ref, n0_ref, m0_ref, conv0_ref,
                out_ref, c_out_ref, n_out_ref, m_out_ref,
                cbuf, c_scr, n_scr, m_scr, *, chunk_axis=1):
    ln = qk_ref.shape[0]
    on_first, on_last = _first_last(chunk_axis)
    lo = CONV_PAD - (CONV_W - 1)

    @on_first
    def _():
        cbuf[lo:CONV_PAD, :] = conv0_ref[...]
        c_scr[...] = c0_ref[...]
        n_scr[...] = n0_ref[...]
        m_scr[...] = m0_ref[...]

    cbuf[CONV_PAD:CONV_PAD + ln, :] = qk_ref[...]
    conv = cb_ref[...] + cbuf[lo:lo + ln, :] * cw_ref[0:1, :]
    for jj in range(1, CONV_W):
        conv = conv + cbuf[lo + jj:lo + jj + ln, :] * cw_ref[jj:jj + 1, :]
    tail = cbuf[lo + ln:CONV_PAD + ln, :]
    cbuf[lo:CONV_PAD, :] = tail
    act = _silu(conv)

    gcb = gc_ref[...] + gbr_ref[...]
    grb = gr_ref[...] + gbc_ref[...]
    lfc = _log_sigmoid(gcb)
    lfr = _log_sigmoid(grb)
    row = lax.broadcasted_iota(jnp.int32, (ln, ln), 0)
    col = lax.broadcasted_iota(jnp.int32, (ln, ln), 1)
    causal = col <= row

    for h in range(N_HEADS):
        sl = slice(h * HEAD_DIM, (h + 1) * HEAD_DIM)
        q = act[:, h * HEAD_DIM:(h + 1) * HEAD_DIM]
        k = act[:, BR_WIDTH + h * HEAD_DIM:BR_WIDTH + (h + 1) * HEAD_DIM] * HEAD_SCALE
        v = v_ref[:, sl]
        i_col = gcb[:, h:h + 1]
        lf_col = lfc[:, N_HEADS + h:N_HEADS + h + 1]
        i_row = grb[h:h + 1, :]
        lf_row = lfr[N_HEADS + h:N_HEADS + h + 1, :]
        b_col = jnp.sum(jnp.where(causal, lf_row, 0.0), axis=1, keepdims=True)
        b_row = jnp.sum(jnp.where(row <= col, lf_col, 0.0), axis=0, keepdims=True)
        b_last = b_col[ln - 1:ln, :]
        m_old = m_scr[h][:, 0:1]
        c_old = c_scr[h]
        n_old = n_scr[h]

        dlog = jnp.where(causal, b_col - b_row + i_row, NEG)
        inter = b_col + m_old
        mt = jnp.maximum(inter, jnp.max(dlog, axis=1, keepdims=True))
        w = jnp.exp(dlog - mt)
        wi = jnp.exp(inter - mt)
        qb = q.astype(bf16)
        vb = v.astype(bf16)
        qk = lax.dot_general(qb, k.astype(bf16), NT_DIMS, preferred_element_type=f32) * w
        num = (jnp.dot(qk.astype(bf16), vb, preferred_element_type=f32)
               + wi * jnp.dot(qb, c_old.astype(bf16), preferred_element_type=f32))
        den = jnp.sum(qk, axis=1, keepdims=True) + wi * jnp.sum(q * n_old, axis=1, keepdims=True)
        hh = num / jnp.maximum(jnp.abs(den), jnp.exp(-mt))

        m_new = mt[ln - 1:ln, :]
        ws = jnp.exp(b_last - b_col + i_col - m_new)
        dec = jnp.exp(b_last + m_old - m_new)
        kw = k * ws
        c_scr[h] = dec * c_old + lax.dot_general(kw.astype(bf16), vb, TN_DIMS, preferred_element_type=f32)
        n_scr[h] = dec * n_old + jnp.sum(kw, axis=0, keepdims=True)
        m_scr[h] = jnp.broadcast_to(m_new, (1, HEAD_DIM))

        y = _head_norm(_sigmoid(og_ref[:, sl]) * hh, gn_ref[:, sl])
        out_ref[:, sl] = (y * _silu(z_ref[:, sl])).astype(bf16)

    @on_last
    def _():
        c_out_ref[...] = c_scr[...]
        n_out_ref[...] = n_scr[...]
        m_out_ref[...] = m_scr[...]


def _mlstm(u, gates_r, conv_w, conv_b, b_ig, b_fg, gn, c0, n0, m0, conv0, row0, ng, nc, ln):
    rb0 = row0 // ln
    gbias = jnp.concatenate([b_ig, b_fg]).astype(f32)
    gbr = jnp.zeros((1, GATE_PAD), f32).at[0, :2 * N_HEADS].set(gbias)
    gbc = gbias.reshape(2 * N_HEADS, 1)

    def blk(col, width):
        return pl.BlockSpec((ln, width), lambda g, c: (rb0 + g * nc + c, col // width))

    def const(shape):
        return pl.BlockSpec(shape, lambda g, c: (0,) * len(shape))

    def per_seq(shape):
        return pl.BlockSpec((None,) + shape, lambda g, c: (g,) + (0,) * len(shape))

    state_shapes = [jax.ShapeDtypeStruct((ng, N_HEADS, HEAD_DIM, HEAD_DIM), f32),
                    jax.ShapeDtypeStruct((ng, N_HEADS, 1, HEAD_DIM), f32),
                    jax.ShapeDtypeStruct((ng, N_HEADS, 1, HEAD_DIM), f32)]
    state_specs = [per_seq((N_HEADS, HEAD_DIM, HEAD_DIM)),
                   per_seq((N_HEADS, 1, HEAD_DIM)),
                   per_seq((N_HEADS, 1, HEAD_DIM))]
    out, c_new, n_new, m_new = pl.pallas_call(
        _mlstm_body,
        out_shape=[jax.ShapeDtypeStruct((ng * nc * ln, BR_WIDTH), bf16)] + state_shapes,
        grid=(ng, nc),
        in_specs=[blk(U_BQK, B_QK), blk(U_BV, BR_WIDTH), blk(U_BO, BR_WIDTH), blk(U_BZ, BR_WIDTH),
                  blk(U_GATES, GATE_PAD),
                  pl.BlockSpec((None, 2 * N_HEADS, ln), lambda g, c: (g * nc + c, 0, 0)),
                  const((CONV_W, B_QK)), const((1, B_QK)), const((1, GATE_PAD)), const((2 * N_HEADS, 1)),
                  const((1, BR_WIDTH))] + state_specs + [per_seq((CONV_W - 1, B_QK))],
        out_specs=[pl.BlockSpec((ln, BR_WIDTH), lambda g, c: (g * nc + c, 0))] + state_specs,
        scratch_shapes=[pltpu.VMEM((CONV_PAD + ln, B_QK), f32),
                        pltpu.VMEM((N_HEADS, HEAD_DIM, HEAD_DIM), f32),
                        pltpu.VMEM((N_HEADS, 1, HEAD_DIM), f32),
                        pltpu.VMEM((N_HEADS, 1, HEAD_DIM), f32)],
        compiler_params=_params("parallel", "arbitrary"),
        name="mlstm",
    )(u, u, u, u, u, gates_r, conv_w, conv_b.reshape(1, B_QK), gbr, gbc, gn.reshape(1, BR_WIDTH),
      c0, n0.reshape(ng, N_HEADS, 1, HEAD_DIM),
      jnp.broadcast_to(m0[:, :, None, None], (ng, N_HEADS, 1, HEAD_DIM)), conv0)
    return out, c_new, n_new[:, :, 0, :], m_new[:, :, 0, 0]


def _ret_body(q_ref, k_ref, v_ref, z_ref, cc_ref, ss_ref, gn_ref, s0_ref, out_ref, s_out_ref, s_scr,
              *, chunk_axis=1):
    ln = q_ref.shape[0]
    on_first, on_last = _first_last(chunk_axis)

    @on_first
    def _():
        s_scr[...] = s0_ref[...]

    row = lax.broadcasted_iota(jnp.int32, (ln, ln), 0)
    col = lax.broadcasted_iota(jnp.int32, (ln, ln), 1)
    rel = (row - col).astype(f32)
    tpos = lax.broadcasted_iota(jnp.int32, (ln, 1), 0).astype(f32)
    cc = cc_ref[...]
    ss = ss_ref[...]
    for h in range(N_HEADS):
        sl = slice(h * HEAD_DIM, (h + 1) * HEAD_DIM)
        lg = math.log(1.0 - 2.0 ** (-RET_GAMMA_EXP0 - h))
        q = q_ref[:, sl]
        k = k_ref[:, sl]
        qr = q * cc + pltpu.roll(q, DIFF_DH, 1) * ss
        kr = (k * cc + pltpu.roll(k, DIFF_DH, 1) * ss) * HEAD_SCALE
        vb = v_ref[:, sl].astype(bf16)
        decay = jnp.where(rel >= 0.0, jnp.exp(jnp.maximum(rel, 0.0) * lg), 0.0)
        att = lax.dot_general(qr.astype(bf16), kr.astype(bf16), NT_DIMS, preferred_element_type=f32) * decay
        q_dec = qr * jnp.exp((tpos + 1.0) * lg)
        s_old = s_scr[h]
        o = (jnp.dot(att.astype(bf16), vb, preferred_element_type=f32)
             + jnp.dot(q_dec.astype(bf16), s_old.astype(bf16), preferred_element_type=f32))
        k_dec = kr * jnp.exp((ln - 1.0 - tpos) * lg)
        s_scr[h] = math.exp(ln * lg) * s_old + lax.dot_general(k_dec.astype(bf16), vb, TN_DIMS,
                                                               preferred_element_type=f32)
        out_ref[:, sl] = (_head_norm(o, gn_ref[:, sl]) * _silu(z_ref[:, sl])).astype(bf16)

    @on_last
    def _():
        s_out_ref[...] = s_scr[...]


def _rope_tables(pos):
    inv = ROPE_BASE ** (-jnp.arange(0, HEAD_DIM, 2, dtype=f32) / HEAD_DIM)
    ang = pos.astype(f32)[:, None] * inv[None, :]
    cos, sin = jnp.cos(ang), jnp.sin(ang)
    return jnp.concatenate([cos, cos], axis=-1), jnp.concatenate([-sin, sin], axis=-1)


def _retention(u, pos, gn, s0, row0, ng, nc, ln):
    rb0 = row0 // ln
    cc, ss = _rope_tables(pos)

    def blk(col):
        return pl.BlockSpec((ln, BR_WIDTH), lambda g, c: (rb0 + g * nc + c, col // BR_WIDTH))

    rope_spec = pl.BlockSpec((ln, HEAD_DIM), lambda g, c: (c, 0))
    state_spec = pl.BlockSpec((None, N_HEADS, HEAD_DIM, HEAD_DIM), lambda g, c: (g, 0, 0, 0))
    return pl.pallas_call(
        _ret_body,
        out_shape=[jax.ShapeDtypeStruct((ng * nc * ln, BR_WIDTH), bf16),
                   jax.ShapeDtypeStruct((ng, N_HEADS, HEAD_DIM, HEAD_DIM), f32)],
        grid=(ng, nc),
        in_specs=[blk(U_DQ), blk(U_DK), blk(U_DV), blk(U_DZ), rope_spec, rope_spec,
                  pl.BlockSpec((1, BR_WIDTH), lambda g, c: (0, 0)), state_spec],
        out_specs=[pl.BlockSpec((ln, BR_WIDTH), lambda g, c: (g * nc + c, 0)), state_spec],
        scratch_shapes=[pltpu.VMEM((N_HEADS, HEAD_DIM, HEAD_DIM), f32)],
        compiler_params=_params("parallel", "arbitrary"),
        name="retention",
    )(u, u, u, u, cc, ss, gn.reshape(1, BR_WIDTH), s0)


def _sample_branches_body(*refs, lam_init):
    it = iter(refs)

    def take(n):
        return [next(it) for _ in range(n)]

    a_in, c_in, m_in, b_in, d_in = take(8), take(8), take(4), take(15), take(8)
    a_out, c_out, m_out, b_out, d_out = take(3), take(3), take(1), take(4), take(2)
    b_scr, d_scr = take(4), take(1)
    _band_sample_body(*a_in, *a_out)
    _diff_sample_body(*c_in, *c_out, lam_init=lam_init)
    _mem_sample_body(*m_in, *m_out)
    _mlstm_body(*b_in, *b_out, *b_scr, chunk_axis=None)
    _ret_body(*d_in, *d_out, *d_scr, chunk_axis=None)


def _sample_branches(u, layer, caches, rel_bias, lamp, lam_init, gains, conv_w, conv_b, b_ig, b_fg,
                     states, pos, nb, t):
    ca_k, ca_v, cc_k, cc_v, cm_k, cm_v = caches
    subln_c, gn_b, gn_d = gains
    c0, n0, m0, conv0, s0 = states
    nrow = ca_k.shape[2] // N_HEADS
    bias_c = _rel_bias_toeplitz(rel_bias, t, nrow, nrow)
    bias_c = jnp.repeat(bias_c, N_HEADS, axis=2).reshape(N_HEADS * t, nrow * N_HEADS)
    bias_n = jnp.tile(_rel_bias_toeplitz(rel_bias, t, t, 0), (1, 1, N_HEADS)).reshape(N_HEADS * t, N_HEADS * t)
    gbias = jnp.concatenate([b_ig, b_fg]).astype(f32)
    gbr = jnp.zeros((1, GATE_PAD), f32).at[0, :2 * N_HEADS].set(gbias)
    cc, ss = _rope_tables(pos)

    def blk(col, width=BR_WIDTH):
        return pl.BlockSpec((t, width), lambda b: (b, col // width))

    def const(shape):
        return pl.BlockSpec(shape, lambda b: (0,) * len(shape))

    def per_seq(shape):
        return pl.BlockSpec((None,) + shape, lambda b: (b,) + (0,) * len(shape))

    def cache(arr):
        return pl.BlockSpec((None, None) + arr.shape[2:], lambda b: (layer, b, 0, 0))

    mat, vec = (N_HEADS, HEAD_DIM, HEAD_DIM), (N_HEADS, 1, HEAD_DIM)
    kv_state = (t * N_HEADS, HEAD_DIM)
    in_specs = (
        [blk(U_AQ), blk(U_AK), blk(U_AV), blk(U_AZ), cache(ca_k), cache(ca_v),
         const((N_HEADS * t, nrow * N_HEADS)), const((N_HEADS * t, N_HEADS * t))]
        + [blk(U_CQ), blk(U_CK), blk(U_CV), blk(U_CZ), cache(cc_k), cache(cc_v),
           const((4, DIFF_DH)), const((1, BR_WIDTH))]
        + [blk(U_MQ), blk(U_MZ), cache(cm_k), cache(cm_v)]
        + [blk(U_BQK, B_QK), blk(U_BV), blk(U_BO), blk(U_BZ), blk(U_GATES, GATE_PAD),
           per_seq((2 * N_HEADS, t)), const((CONV_W, B_QK)), const((1, B_QK)), const((1, GATE_PAD)),
           const((2 * N_HEADS, 1)), const((1, BR_WIDTH)), per_seq(mat), per_seq(vec), per_seq(vec),
           per_seq((CONV_W - 1, B_QK))]
        + [blk(U_DQ), blk(U_DK), blk(U_DV), blk(U_DZ), const((t, HEAD_DIM)), const((t, HEAD_DIM)),
           const((1, BR_WIDTH)), per_seq(mat)])
    args = (
        [u, u, u, u, ca_k, ca_v, bias_c, bias_n]
        + [u, u, u, u, cc_k, cc_v, lamp, subln_c.reshape(1, BR_WIDTH)]
        + [u, u, cm_k, cm_v]
        + [u, u, u, u, u, _gates_rowform(u, 0, nb * t, t), conv_w, conv_b.reshape(1, B_QK), gbr,
           gbias.reshape(2 * N_HEADS, 1), gn_b.reshape(1, BR_WIDTH), c0,
           n0.reshape(nb, N_HEADS, 1, HEAD_DIM),
           jnp.broadcast_to(m0[:, :, None, None], (nb, N_HEADS, 1, HEAD_DIM)), conv0]
        + [u, u, u, u, cc, ss, gn_d.reshape(1, BR_WIDTH), s0])
    branch_out = jax.ShapeDtypeStruct((nb * t, BR_WIDTH), bf16)
    kv_out = jax.ShapeDtypeStruct((nb,) + kv_state, f32)
    mat_out = jax.ShapeDtypeStruct((nb,) + mat, f32)
    vec_out = jax.ShapeDtypeStruct((nb,) + vec, f32)
    out_shape = ([branch_out, kv_out, kv_out] * 2 + [branch_out]
                 + [branch_out, mat_out, vec_out, vec_out] + [branch_out, mat_out])
    o_spec = pl.BlockSpec((t, BR_WIDTH), lambda b: (b, 0))
    out_specs = ([o_spec, per_seq(kv_state), per_seq(kv_state)] * 2 + [o_spec]
                 + [o_spec, per_seq(mat), per_seq(vec), per_seq(vec)] + [o_spec, per_seq(mat)])
    (oa, ak, av, oc, ck, cv, om, ob, c_new, n_new, m_new, od, s_new) = pl.pallas_call(
        functools.partial(_sample_branches_body, lam_init=lam_init),
        out_shape=out_shape,
        grid=(nb,),
        in_specs=in_specs,
        out_specs=out_specs,
        scratch_shapes=[pltpu.VMEM((CONV_PAD + t, B_QK), f32), pltpu.VMEM(mat, f32),
                        pltpu.VMEM(vec, f32), pltpu.VMEM(vec, f32), pltpu.VMEM(mat, f32)],
        compiler_params=_params("arbitrary"),
        name="sample_branches",
    )(*args)
    return ((oa, ob, oc, od, om), (ak, av, ck, cv),
            (c_new, n_new[:, :, 0, :], m_new[:, :, 0, 0], s_new))


def _merge_body(x_ref, g_ref, oa_ref, ob_ref, oc_ref, od_ref, om_ref,
                wg0_ref, wg1_ref, wg2_ref, wg3_ref, wg4_ref, wb_ref, wo_ref, fg_ref,
                wo_last_ref, y_ref, h_scr, mrg_scr, *, rows, final):
    n = pl.program_id(1)

    @pl.when(n == 0)
    def _():
        _rmsnorm_rows(x_ref, g_ref, h_scr, rows)
        y_ref[...] = jnp.zeros(y_ref.shape, f32)
        mrg_scr[...] = jnp.zeros(mrg_scr.shape, bf16)

    y_ref[...] += jnp.dot(mrg_scr[...], wo_ref[...], preferred_element_type=f32)
    h = h_scr[...]
    merged = None
    for i, (o_ref, wg_ref) in enumerate(zip((oa_ref, ob_ref, oc_ref, od_ref, om_ref),
                                            (wg0_ref, wg1_ref, wg2_ref, wg3_ref, wg4_ref))):
        gate = _sigmoid(lax.dot_general(h, wg_ref[...], NT_DIMS, preferred_element_type=f32))
        term = gate * jnp.dot(o_ref[...], wb_ref[i], preferred_element_type=f32)
        merged = term if merged is None else merged + term
    mrg_scr[...] = merged.astype(bf16)

    @pl.when(n == pl.num_programs(1) - 1)
    def _():
        y_ref[...] += jnp.dot(mrg_scr[...], wo_last_ref[...], preferred_element_type=f32)
        tm = x_ref.shape[0]
        for r in range(0, tm, rows):
            y = x_ref[r:r + rows, :] + y_ref[r:r + rows, :]
            if final:
                y = (y * lax.rsqrt(jnp.mean(y * y, axis=-1, keepdims=True) + EPS)) * fg_ref[...]
            y_ref[r:r + rows, :] = y


def _merge(x, g, outs, wg, wb, wo, fg, *, final, name, tm_target=704, tn=256):
    m, d = x.shape
    tm = _divisor_tile(m, tm_target, 16)
    rows = _divisor_tile(tm, 256, 8)
    nn = d // tn

    def wg_spec(i):
        return pl.BlockSpec((tn, d), lambda r, n: (i * nn + n, 0))

    o_spec = pl.BlockSpec((tm, BR_WIDTH), lambda r, n: (r, 0))
    return pl.pallas_call(
        functools.partial(_merge_body, rows=rows, final=final),
        out_shape=jax.ShapeDtypeStruct((m, d), f32),
        grid=(m // tm, nn),
        in_specs=[pl.BlockSpec((tm, d), lambda r, n: (r, 0)),
                  pl.BlockSpec((1, d), lambda r, n: (0, 0))]
                 + [o_spec] * N_BRANCH
                 + [wg_spec(i) for i in range(N_BRANCH)]
                 + [pl.BlockSpec((N_BRANCH, BR_WIDTH, tn), lambda r, n: (0, 0, n)),
                    pl.BlockSpec((tn, d), lambda r, n: (jnp.maximum(n - 1, 0), 0)),
                    pl.BlockSpec((1, d), lambda r, n: (0, 0)),
                    pl.BlockSpec((tn, d), lambda r, n: (nn - 1, 0))],
        out_specs=pl.BlockSpec((tm, d), lambda r, n: (r, 0)),
        scratch_shapes=[pltpu.VMEM((tm, d), bf16), pltpu.VMEM((tm, tn), bf16)],
        compiler_params=_params("parallel", "arbitrary"),
        name=name,
    )(x, g.reshape(1, d), *outs, wg, wg, wg, wg, wg, wb, wo, fg.reshape(1, d), wo)


F32_SUBLANES = 8


def _cast_rows_body(src_ref, o_ref, *, valid_rows_last):
    tr = o_ref.shape[0]
    x = src_ref[0]
    if valid_rows_last is not None:
        row = lax.broadcasted_iota(jnp.int32, (tr, 1), 0)
        keep = jnp.logical_or(pl.program_id(0) < pl.num_programs(0) - 1, row < valid_rows_last)
        x = jnp.where(keep, x, 0.0)
    o_ref[...] = x.astype(bf16)


def _cast_rows(w_t, layer, *, n_rows, tr, src_row, valid_rows_last, name):
    d = w_t.shape[2]
    assert n_rows % tr == 0
    return pl.pallas_call(
        functools.partial(_cast_rows_body, valid_rows_last=valid_rows_last),
        out_shape=jax.ShapeDtypeStruct((n_rows, d), bf16),
        grid=(n_rows // tr,),
        in_specs=[pl.BlockSpec((pl.Element(1), pl.Element(tr), pl.Element(d)),
                               lambda j: (layer, pl.multiple_of(src_row(j), F32_SUBLANES), 0))],
        out_specs=pl.BlockSpec((tr, d), lambda j: (j, 0)),
        compiler_params=_params("arbitrary"),
        name=name,
    )(w_t)


def _repack_w_in(w_t, layer):
    tr = MXU_COLS
    n_plain, n_main = W_BI // tr, U_GATES // tr

    def u_src(j):
        return jnp.where(j < n_plain, j * tr, jnp.where(j < n_main, j * tr + (W_BZ - W_BI), W_BI))

    wu = _cast_rows(w_t, layer, n_rows=U_WIDTH, tr=tr, src_row=u_src, valid_rows_last=2 * N_HEADS,
                    name="repack_u")
    wg = _cast_rows(w_t, layer, n_rows=N_BRANCH * D_MODEL, tr=512, src_row=lambda j: W_GATE + j * 512,
                    valid_rows_last=None, name="repack_g")
    return wu, wg


def _gates_rowform(u, row0, nrows, ln):
    g = u[row0:row0 + nrows, U_GATES:U_GATES + 2 * N_HEADS]
    return g.reshape(nrows // ln, ln, 2 * N_HEADS).transpose(0, 2, 1)


def kernel(x_prompt, x_sample, mem_prompt, cache_a_k, cache_a_v, cache_c_k, cache_c_v, cache_mem_k, cache_mem_v, state_b_C, state_b_n, state_b_m, state_b_conv, state_d_S, norm_g, w_in, conv_w, conv_b, b_ig, b_fg, rel_bias, lam_q1, lam_k1, lam_q2, lam_k2, gn_b, subln_c, gn_d, mem_norm_g, w_mk, w_mv, w_branch, w_out, final_g):
    batch, seq, d = x_prompt.shape
    nb, t, _ = x_sample.shape
    depth = w_in.shape[0]
    n_mem = mem_prompt.shape[1]
    past = cache_c_k.shape[2]
    mp = batch * seq
    assert d == D_MODEL and seq % C_BLOCK == 0 and seq % A_QBLOCK == 0 and seq % SCAN_CHUNK == 0

    xp = x_prompt.reshape(mp, d)
    xs = x_sample.reshape(nb * t, d)
    mem = mem_prompt.reshape(batch * n_mem, d)
    ca_k, ca_v = _cache_rows(cache_a_k), _cache_rows(cache_a_v)
    cc_k, cc_v = _cache_rows(cache_c_k), _cache_rows(cache_c_v)
    cm_k, cm_v = _cache_rows(cache_mem_k), _cache_rows(cache_mem_v)
    pos_p = jnp.arange(seq)
    pos_s = past + jnp.arange(t)
    a_rows = min(A_WINDOW, seq)
    mem_tq = _divisor_tile(seq, 512, 16)

    w_t = jnp.swapaxes(w_in, 1, 2)

    sp, ss = [], []
    for l in range(depth):
        lam_init = 0.8 - 0.6 * math.exp(-0.3 * l)
        lamp = jnp.stack([lam_q1[l], lam_k1[l], lam_q2[l], lam_k2[l]]).astype(f32)
        wu, wg = _repack_w_in(w_t, l)
        u = _norm_matmul(xp, norm_g[l], wu, tm_target=1024, tn=U_TILE, w_rows_are_outputs=True,
                         name="in_proj")
        us = _norm_matmul(xs, norm_g[l], wu, tm_target=1024, tn=U_TILE, w_rows_are_outputs=True,
                          name="in_proj_sample")
        mkv = _norm_matmul(mem, mem_norm_g[l], jnp.concatenate([w_mk[l], w_mv[l]], axis=1).astype(bf16),
                           tm_target=512, tn=512, w_rows_are_outputs=False, name="mem_proj")

        oa_p, ak_p, av_p = _band_prompt(u, rel_bias[l], batch, seq, a_rows)
        zeros_c = jnp.zeros((batch, N_HEADS, HEAD_DIM, HEAD_DIM), f32)
        ob_p, c_p, n_p, m_p = _mlstm(
            u, _gates_rowform(u, 0, mp, SCAN_CHUNK), conv_w[l], conv_b[l], b_ig[l], b_fg[l], gn_b[l],
            zeros_c, jnp.zeros((batch, N_HEADS, HEAD_DIM), f32), jnp.zeros((batch, N_HEADS), f32),
            jnp.zeros((batch, CONV_W - 1, B_QK), f32), 0, batch, seq // SCAN_CHUNK, SCAN_CHUNK)
        oc_p, ck_p, cv_p = _diff_prompt(u, lamp, subln_c[l], lam_init, batch, seq)
        od_p, s_p = _retention(u, pos_p, gn_d[l], zeros_c, 0, batch, seq // SCAN_CHUNK, SCAN_CHUNK)
        om_p = _mem_attn_prompt(u, mkv, batch, seq, mem_tq, n_mem)

        outs_s, (ak_s, av_s, ck_s, cv_s), (c_s, n_s, m_s, s_s) = _sample_branches(
            us, l, (ca_k, ca_v, cc_k, cc_v, cm_k, cm_v), rel_bias[l], lamp, lam_init,
            (subln_c[l], gn_b[l], gn_d[l]), conv_w[l], conv_b[l], b_ig[l], b_fg[l],
            (state_b_C[l].astype(f32), state_b_n[l].astype(f32), state_b_m[l].astype(f32),
             state_b_conv[l].astype(f32), state_d_S[l].astype(f32)), pos_s, nb, t)

        wb, wo = w_branch[l].astype(bf16), w_out[l].astype(bf16)
        final = l == depth - 1
        xp = _merge(xp, norm_g[l], (oa_p, ob_p, oc_p, od_p, om_p), wg, wb, wo, final_g, final=final,
                    name="gated_merge")
        xs = _merge(xs, norm_g[l], outs_s, wg, wb, wo, final_g, final=final, name="gated_merge_sample")

        def conv_tail(arr, n_seq, rows):
            return jnp.stack([arr[(i + 1) * rows - (CONV_W - 1):(i + 1) * rows, U_BQK:U_BQK + B_QK]
                              for i in range(n_seq)])

        def hd(a):
            return a.reshape(a.shape[:2] + (N_HEADS, HEAD_DIM))

        def rows_hd(a):
            return a.reshape(a.shape[0], a.shape[1] // N_HEADS, N_HEADS, HEAD_DIM)

        sp.append((rows_hd(ak_p), rows_hd(av_p), rows_hd(ck_p), rows_hd(cv_p),
                   hd(mkv[:, :BR_WIDTH].reshape(batch, n_mem, BR_WIDTH)),
                   hd(mkv[:, BR_WIDTH:].reshape(batch, n_mem, BR_WIDTH)),
                   c_p, n_p, m_p, conv_tail(u, batch, seq), s_p))
        ss.append((rows_hd(ak_s), rows_hd(av_s), rows_hd(ck_s), rows_hd(cv_s),
                   c_s, n_s, m_s, us[:, U_BQK:U_BQK + B_QK].reshape(nb, t, B_QK)[:, t - (CONV_W - 1):], s_s))

    y_prompt = xp.reshape(batch, seq, d)
    y_sample = xs.reshape(nb, t, d)
    p_states = tuple(jnp.stack([st[i] for st in sp]) for i in range(11))
    s_states = tuple(jnp.stack([st[i] for st in ss]) for i in range(9))
    return (y_prompt, y_sample) + p_states + s_states
```

```python
import functools
import math

import jax
import jax.numpy as jnp
import numpy as np
from jax import lax
from jax.experimental import pallas as pl
from jax.experimental.pallas import tpu as pltpu

f32 = jnp.float32
bf16 = jnp.bfloat16

D_MODEL = 2048
N_HEADS = 4
HEAD_DIM = 128
BR_WIDTH = N_HEADS * HEAD_DIM
N_BRANCH = 5
CHUNK = 64
A_PREV_CHUNKS = 8
A_WINDOW = A_PREV_CHUNKS * CHUNK
A_BAND = (A_PREV_CHUNKS + 1) * CHUNK
A_REL_CLIP = 128
CONV_W = 4
B_QK = 2 * BR_WIDTH
DIFF_DH = HEAD_DIM // 2
RET_GAMMA_EXP0 = 5.0
ROPE_BASE = 10000.0
EPS = 1e-6
NEG = -1e30
HEAD_SCALE = HEAD_DIM ** -0.5
DIFF_SCALE = DIFF_DH ** -0.5
LOG2_E = math.log2(math.e)

U_AQ, U_AK, U_AV, U_AZ = 0, 512, 1024, 1536
U_BQK, U_BV, U_BO, U_BZ = 2048, 3072, 3584, 4096
U_CQ, U_CK, U_CV, U_CZ = 4608, 5120, 5632, 6144
U_DQ, U_DK, U_DV, U_DZ = 6656, 7168, 7680, 8192
U_MQ, U_MZ = 8704, 9216
U_GATES = 9728
GATE_PAD = 128
MXU_COLS = 256
U_WIDTH = U_GATES + MXU_COLS
U_TILE = 3 * MXU_COLS
W_BI = 4096
W_BZ = 4104
W_GATE = 9736

VMEM_LIMIT_BYTES = 56 * 1024 * 1024
A_QBLOCK = 4 * CHUNK
C_BLOCK = 512
SCAN_CHUNK = 256
NT_DIMS = (((1,), (1,)), ((), ()))
TN_DIMS = (((0,), (0,)), ((), ()))


def _divisor_tile(n, target, multiple):
    best = None
    for t in range(multiple, min(n, target) + 1, multiple):
        if n % t == 0:
            best = t
    assert best is not None, (n, target, multiple)
    return best


def _params(*sem):
    return pltpu.CompilerParams(dimension_semantics=sem, vmem_limit_bytes=VMEM_LIMIT_BYTES)


def _sigmoid(x):
    return 1.0 / (1.0 + jnp.exp(-x))


def _silu(x):
    return x * _sigmoid(x)


def _log_sigmoid(x):
    return jnp.minimum(x, 0.0) - jnp.log1p(jnp.exp(-jnp.abs(x)))


def _head_norm(y, gain):
    return y * lax.rsqrt(jnp.mean(y * y, axis=-1, keepdims=True) + EPS) * gain


def _rmsnorm_rows(x_ref, g_ref, h_ref, rows):
    tm = x_ref.shape[0]
    for r in range(0, tm, rows):
        x = x_ref[r:r + rows, :]
        ms = jnp.mean(x * x, axis=-1, keepdims=True)
        h_ref[r:r + rows, :] = ((x * lax.rsqrt(ms + EPS)) * g_ref[...]).astype(bf16)


def _norm_matmul_body(x_ref, g_ref, w_ref, o_ref, h_ref, *, rows, w_rows_are_outputs):
    @pl.when(pl.program_id(1) == 0)
    def _():
        _rmsnorm_rows(x_ref, g_ref, h_ref, rows)

    if w_rows_are_outputs:
        o = lax.dot_general(h_ref[...], w_ref[...], NT_DIMS, preferred_element_type=f32)
    else:
        o = jnp.dot(h_ref[...], w_ref[...], preferred_element_type=f32)
    o_ref[...] = o.astype(o_ref.dtype)


def _norm_matmul(x, g, w, *, tm_target, tn, w_rows_are_outputs, name):
    m, d = x.shape
    n = w.shape[0] if w_rows_are_outputs else w.shape[1]
    tm = _divisor_tile(m, tm_target, 16)
    rows = _divisor_tile(tm, 256, 8)
    assert n % tn == 0
    w_spec = (pl.BlockSpec((tn, d), lambda i, j: (j, 0)) if w_rows_are_outputs
              else pl.BlockSpec((d, tn), lambda i, j: (0, j)))
    return pl.pallas_call(
        functools.partial(_norm_matmul_body, rows=rows, w_rows_are_outputs=w_rows_are_outputs),
        out_shape=jax.ShapeDtypeStruct((m, n), f32),
        grid=(m // tm, n // tn),
        in_specs=[pl.BlockSpec((tm, d), lambda i, j: (i, 0)),
                  pl.BlockSpec((1, d), lambda i, j: (0, 0)),
                  w_spec],
        out_specs=pl.BlockSpec((tm, tn), lambda i, j: (i, j)),
        scratch_shapes=[pltpu.VMEM((tm, d), bf16)],
        compiler_params=_params("parallel", "arbitrary"),
        name=name,
    )(x, g.reshape(1, d), w)


def _store_head_rows(dst_ref, src_ref):
    rows = src_ref.shape[0]
    for h in range(N_HEADS):
        dst_ref[pl.ds(h, rows, stride=N_HEADS), :] = src_ref[:, h * HEAD_DIM:(h + 1) * HEAD_DIM]


def _band_prompt_body(q_ref, k0_ref, k1_ref, k2_ref, v0_ref, v1_ref, v2_ref, z_ref, bias_ref,
                      o_ref, ks_ref, vs_ref, *, n_tail):
    tq = q_ref.shape[0]
    i = pl.program_id(1)

    @pl.when(i >= pl.num_programs(1) - n_tail)
    def _():
        _store_head_rows(ks_ref, k2_ref)
        _store_head_rows(vs_ref, v2_ref)

    w_idx = lax.broadcasted_iota(jnp.int32, (tq, 3 * tq), 1)
    valid = (w_idx + (i - 2) * tq) >= 0
    for h in range(N_HEADS):
        sl = slice(h * HEAD_DIM, (h + 1) * HEAD_DIM)
        q = q_ref[:, sl].astype(bf16)
        kw = jnp.concatenate([k0_ref[:, sl], k1_ref[:, sl], k2_ref[:, sl]], axis=0).astype(bf16)
        vw = jnp.concatenate([v0_ref[:, sl], v1_ref[:, sl], v2_ref[:, sl]], axis=0).astype(bf16)
        s = lax.dot_general(q, kw, NT_DIMS, preferred_element_type=f32) * HEAD_SCALE + bias_ref[h]
        s = jnp.where(valid, s, NEG)
        p = jnp.exp(s - jnp.max(s, axis=-1, keepdims=True))
        l = jnp.sum(p, axis=-1, keepdims=True)
        o = jnp.dot(p.astype(bf16), vw, preferred_element_type=f32) / l
        o_ref[:, sl] = (o * _silu(z_ref[:, sl])).astype(bf16)


def _rel_bias_toeplitz(rel_bias, rows, cols, rel0):
    p = rows + cols
    j = np.arange(p)
    d = np.where(j < cols, j, j - p)
    idx = np.clip(rel0 - d, -A_REL_CLIP, A_REL_CLIP) + A_REL_CLIP
    v = rel_bias.astype(f32)[:, idx]
    flat = jnp.tile(v, (1, rows))[:, :rows * (p - 1)]
    return flat.reshape(rel_bias.shape[0], rows, p - 1)[:, :, :cols]


def _band_bias_prompt(rel_bias):
    r = np.arange(A_QBLOCK)[:, None]
    w = np.arange(3 * A_QBLOCK)[None, :]
    kj = w - CHUNK * (r // CHUNK)
    inside = (kj >= 0) & (kj < A_BAND)
    table = _rel_bias_toeplitz(rel_bias, A_QBLOCK, 3 * A_QBLOCK, 2 * A_QBLOCK)
    return jnp.where(jnp.asarray(inside)[None], table, NEG)


def _band_prompt(u, rel_bias, batch, seq, a_rows):
    tq = A_QBLOCK
    nqb = seq // tq
    assert a_rows % tq == 0
    n_tail = a_rows // tq
    bias = _band_bias_prompt(rel_bias)

    def blk(col, back):
        return pl.BlockSpec((tq, BR_WIDTH),
                            lambda b, i: (b * nqb + jnp.maximum(i - back, 0), col // BR_WIDTH))

    state_shape = jax.ShapeDtypeStruct((batch, a_rows * N_HEADS, HEAD_DIM), f32)
    state_spec = pl.BlockSpec((None, tq * N_HEADS, HEAD_DIM),
                              lambda b, i: (b, jnp.maximum(i - (nqb - n_tail), 0), 0))
    return pl.pallas_call(
        functools.partial(_band_prompt_body, n_tail=n_tail),
        out_shape=[jax.ShapeDtypeStruct((batch * seq, BR_WIDTH), bf16), state_shape, state_shape],
        grid=(batch, nqb),
        in_specs=[blk(U_AQ, 0),
                  blk(U_AK, 2), blk(U_AK, 1), blk(U_AK, 0),
                  blk(U_AV, 2), blk(U_AV, 1), blk(U_AV, 0),
                  blk(U_AZ, 0),
                  pl.BlockSpec((N_HEADS, tq, 3 * tq), lambda b, i: (0, 0, 0))],
        out_specs=[pl.BlockSpec((tq, BR_WIDTH), lambda b, i: (b * nqb + i, 0)), state_spec, state_spec],
        compiler_params=_params("parallel", "arbitrary"),
        name="band_attn_prompt",
    )(u, u, u, u, u, u, u, u, bias)


def _stack_heads(x):
    return jnp.concatenate([x[:, h * HEAD_DIM:(h + 1) * HEAD_DIM] for h in range(N_HEADS)], axis=0)


def _sample_attend(qs, t, ck_ref, cv_ref, k_new, v_new, scale, bias_c=None, bias_n=None):
    rows = qs.shape[0]
    row_head = (lax.broadcasted_iota(jnp.int32, (rows, 1), 0) // t) % N_HEADS
    ck = ck_ref[...].astype(bf16)
    s = lax.dot_general(qs, ck, NT_DIMS, preferred_element_type=f32) * scale
    if bias_c is not None:
        s = s + bias_c
    col_head = lax.broadcasted_iota(jnp.int32, (1, ck.shape[0]), 1) % N_HEADS
    parts = [(jnp.where(col_head == row_head, s, NEG), cv_ref[...].astype(bf16))]
    if k_new is not None:
        s = lax.dot_general(qs, _stack_heads(k_new).astype(bf16), NT_DIMS, preferred_element_type=f32) * scale
        if bias_n is not None:
            s = s + bias_n
        col_head = lax.broadcasted_iota(jnp.int32, (1, N_HEADS * t), 1) // t
        parts.append((jnp.where(col_head == row_head, s, NEG), _stack_heads(v_new).astype(bf16)))
    m = functools.reduce(jnp.maximum, [jnp.max(s, axis=-1, keepdims=True) for s, _ in parts])
    l, o = 0.0, 0.0
    for s, v in parts:
        p = jnp.exp(s - m)
        l = l + jnp.sum(p, axis=-1, keepdims=True)
        o = o + jnp.dot(p.astype(bf16), v, preferred_element_type=f32)
    return o / l


def _band_sample_body(q_ref, k_ref, v_ref, z_ref, ck_ref, cv_ref, bc_ref, bn_ref, o_ref, ks_ref, vs_ref):
    t = q_ref.shape[0]
    _store_head_rows(ks_ref, k_ref)
    _store_head_rows(vs_ref, v_ref)
    o = _sample_attend(_stack_heads(q_ref[...]).astype(bf16), t, ck_ref, cv_ref, k_ref[...], v_ref[...],
                       HEAD_SCALE, bc_ref[...], bn_ref[...])
    for h in range(N_HEADS):
        sl = slice(h * HEAD_DIM, (h + 1) * HEAD_DIM)
        o_ref[:, sl] = (o[h * t:(h + 1) * t] * _silu(z_ref[:, sl])).astype(bf16)


def _mem_sample_body(q_ref, z_ref, ck_ref, cv_ref, o_ref):
    t = q_ref.shape[0]
    o = _sample_attend(_stack_heads(q_ref[...]).astype(bf16), t, ck_ref, cv_ref, None, None, HEAD_SCALE)
    for h in range(N_HEADS):
        sl = slice(h * HEAD_DIM, (h + 1) * HEAD_DIM)
        o_ref[:, sl] = (o[h * t:(h + 1) * t] * _silu(z_ref[:, sl])).astype(bf16)


def _cache_rows(cache):
    d0, d1, rows = cache.shape[:3]
    return cache.reshape(d0, d1, rows * N_HEADS, HEAD_DIM)


def _mem_body(q_ref, z_ref, k_ref, v_ref, o_ref):
    for h in range(N_HEADS):
        sl = slice(h * HEAD_DIM, (h + 1) * HEAD_DIM)
        q = q_ref[:, sl].astype(bf16)
        k = k_ref[:, sl].astype(bf16)
        v = v_ref[:, sl].astype(bf16)
        s = lax.dot_general(q, k, NT_DIMS, preferred_element_type=f32) * HEAD_SCALE
        p = jnp.exp(s - jnp.max(s, axis=-1, keepdims=True))
        l = jnp.sum(p, axis=-1, keepdims=True)
        o = jnp.dot(p.astype(bf16), v, preferred_element_type=f32) / l
        o_ref[:, sl] = (o * _silu(z_ref[:, sl])).astype(bf16)


def _mem_attn_prompt(u, mkv, nb, t, tq, n_mem):
    nq = t // tq

    def blk(col):
        return pl.BlockSpec((tq, BR_WIDTH), lambda b, i: (b * nq + i, col // BR_WIDTH))

    return pl.pallas_call(
        _mem_body,
        out_shape=jax.ShapeDtypeStruct((nb * t, BR_WIDTH), bf16),
        grid=(nb, nq),
        in_specs=[blk(U_MQ), blk(U_MZ),
                  pl.BlockSpec((n_mem, BR_WIDTH), lambda b, i: (b, 0)),
                  pl.BlockSpec((n_mem, BR_WIDTH), lambda b, i: (b, 1))],
        out_specs=pl.BlockSpec((tq, BR_WIDTH), lambda b, i: (b * nq + i, 0)),
        compiler_params=_params("parallel", "arbitrary"),
        name="mem_attn_prompt",
    )(u, u, mkv, mkv)


def _diff_lambda(lamp_ref, lam_init):
    lp = lamp_ref[...]
    a = jnp.sum(lp[0:1] * lp[1:2], axis=-1, keepdims=True)
    b = jnp.sum(lp[2:3] * lp[3:4], axis=-1, keepdims=True)
    return jnp.exp(a) - jnp.exp(b) + lam_init


def _diff_epilogue(o0, o1, lam, gain, z, lam_init):
    y = _head_norm(o0 - lam * o1, gain) * (1.0 - lam_init)
    return (y * _silu(z)).astype(bf16)


def _half_masks():
    lane = lax.broadcasted_iota(jnp.int32, (1, HEAD_DIM), 1)
    return lane < DIFF_DH, lane >= DIFF_DH


def _diff_prompt_body(qt_ref, kt_ref, q_ref, k_ref, v_ref, z_ref, lamp_ref, gain_ref, o_ref, ks_ref, vs_ref,
                      m_scr, l_scr, acc_scr, *, lam_init):
    tq, tk = q_ref.shape[0], k_ref.shape[0]
    i = qt_ref[pl.program_id(1)]
    j = kt_ref[pl.program_id(1)]

    @pl.when(j == 0)
    def _():
        m_scr[...] = jnp.full(m_scr.shape, NEG, f32)
        l_scr[...] = jnp.zeros(l_scr.shape, f32)
        acc_scr[...] = jnp.zeros(acc_scr.shape, f32)

    lo, hi = _half_masks()

    def step(diagonal):
        if diagonal:
            kc = lax.broadcasted_iota(jnp.int32, (tk, 2 * tq), 0) // CHUNK
            qpos = lax.broadcasted_iota(jnp.int32, (tk, 2 * tq), 1)
            qc = jnp.where(qpos >= tq, qpos - tq, qpos) // CHUNK
            mask = kc <= qc
        for h in range(N_HEADS):
            sl = slice(h * HEAD_DIM, (h + 1) * HEAD_DIM)
            q = q_ref[:, sl] * (DIFF_SCALE * LOG2_E)
            qq = jnp.concatenate([jnp.where(lo, q, 0.0), jnp.where(hi, q, 0.0)], axis=0).astype(bf16)
            k = k_ref[:, sl].astype(bf16)
            vt = v_ref[:, sl].T.astype(bf16)
            s = lax.dot_general(k, qq, NT_DIMS, preferred_element_type=f32)
            if diagonal:
                s = jnp.where(mask, s, NEG)
            m_old = m_scr[h]
            m_new = jnp.maximum(m_old, jnp.max(s, axis=0, keepdims=True))
            p = jnp.exp2(s - m_new)
            alpha = jnp.exp2(m_old - m_new)
            l_scr[h] = alpha * l_scr[h] + jnp.sum(p, axis=0, keepdims=True)
            acc_scr[h] = alpha * acc_scr[h] + jnp.dot(vt, p.astype(bf16), preferred_element_type=f32)
            m_scr[h] = m_new

    @pl.when(j < i)
    def _():
        step(False)

    @pl.when(j == i)
    def _():
        step(True)
        _store_head_rows(ks_ref, k_ref)
        _store_head_rows(vs_ref, v_ref)
        lam = _diff_lambda(lamp_ref, lam_init)
        for h in range(N_HEADS):
            sl = slice(h * HEAD_DIM, (h + 1) * HEAD_DIM)
            on = acc_scr[h] / l_scr[h]
            o0 = on[:, :tq].T
            o1 = on[:, tq:].T
            o_ref[:, sl] = _diff_epilogue(o0, o1, lam, gain_ref[:, sl], z_ref[:, sl], lam_init)


def _diff_prompt(u, lamp, gain, lam_init, batch, seq):
    t = _divisor_tile(seq, C_BLOCK, 2 * CHUNK)
    nt = seq // t

    pairs = [(i, j) for i in range(nt) for j in range(i + 1)]
    q_tile = jnp.asarray([p[0] for p in pairs], jnp.int32)
    k_tile = jnp.asarray([p[1] for p in pairs], jnp.int32)

    def qblk(col):
        return pl.BlockSpec((t, BR_WIDTH), lambda b, p, qt, kt: (b * nt + qt[p], col // BR_WIDTH))

    def kblk(col):
        return pl.BlockSpec((t, BR_WIDTH), lambda b, p, qt, kt: (b * nt + kt[p], col // BR_WIDTH))

    state_shape = jax.ShapeDtypeStruct((batch, seq * N_HEADS, HEAD_DIM), f32)
    state_spec = pl.BlockSpec((None, t * N_HEADS, HEAD_DIM), lambda b, p, qt, kt: (b, qt[p], 0))
    return pl.pallas_call(
        functools.partial(_diff_prompt_body, lam_init=lam_init),
        out_shape=[jax.ShapeDtypeStruct((batch * seq, BR_WIDTH), bf16), state_shape, state_shape],
        grid_spec=pltpu.PrefetchScalarGridSpec(
            num_scalar_prefetch=2,
            grid=(batch, len(pairs)),
            in_specs=[qblk(U_CQ), kblk(U_CK), kblk(U_CV), qblk(U_CZ),
                      pl.BlockSpec((4, DIFF_DH), lambda b, p, qt, kt: (0, 0)),
                      pl.BlockSpec((1, BR_WIDTH), lambda b, p, qt, kt: (0, 0))],
            out_specs=[pl.BlockSpec((t, BR_WIDTH), lambda b, p, qt, kt: (b * nt + qt[p], 0)),
                       state_spec, state_spec],
            scratch_shapes=[pltpu.VMEM((N_HEADS, 1, 2 * t), f32),
                            pltpu.VMEM((N_HEADS, 1, 2 * t), f32),
                            pltpu.VMEM((N_HEADS, HEAD_DIM, 2 * t), f32)]),
        compiler_params=_params("parallel", "arbitrary"),
        name="diff_attn_prompt",
    )(q_tile, k_tile, u, u, u, u, lamp, gain.reshape(1, BR_WIDTH))


def _diff_sample_body(q_ref, k_ref, v_ref, z_ref, ck_ref, cv_ref, lamp_ref, gain_ref, o_ref, ks_ref, vs_ref,
                      *, lam_init):
    t = q_ref.shape[0]
    _store_head_rows(ks_ref, k_ref)
    _store_head_rows(vs_ref, v_ref)
    lo, hi = _half_masks()
    lam = _diff_lambda(lamp_ref, lam_init)
    q = _stack_heads(q_ref[...]) * DIFF_SCALE
    qq = jnp.concatenate([jnp.where(lo, q, 0.0), jnp.where(hi, q, 0.0)], axis=0).astype(bf16)
    o = _sample_attend(qq, t, ck_ref, cv_ref, k_ref[...], v_ref[...], 1.0)
    for h in range(N_HEADS):
        sl = slice(h * HEAD_DIM, (h + 1) * HEAD_DIM)
        o0 = o[h * t:(h + 1) * t]
        o1 = o[(N_HEADS + h) * t:(N_HEADS + h + 1) * t]
        o_ref[:, sl] = _diff_epilogue(o0, o1, lam, gain_ref[:, sl], z_ref[:, sl], lam_init)


CONV_PAD = 8


def _first_last(chunk_axis):
    if chunk_axis is None:
        def run(f):
            f()
        return run, run
    c = pl.program_id(chunk_axis)
    return pl.when(c == 0), pl.when(c == pl.num_programs(chunk_axis) - 1)


def _mlstm_body(qk_ref, v_ref, og_ref, z_ref, gc_ref, gr_ref, cw_ref, cb_ref, gbr_ref, gbc_ref, gn_ref,
                c0_ref, n0_ref, m0_ref, conv0_ref,
                out_ref, c_out_ref, n_out_ref, m_out_ref,
                cbuf, c_scr, n_scr, m_scr, *, chunk_axis=1):
    ln = qk_ref.shape[0]
    on_first, on_last = _first_last(chunk_axis)
    lo = CONV_PAD - (CONV_W - 1)

    @on_first
    def _():
        cbuf[lo:CONV_PAD, :] = conv0_ref[...]
        c_scr[...] = c0_ref[...]
        n_scr[...] = n0_ref[...]
        m_scr[...] = m0_ref[...]

    cbuf[CONV_PAD:CONV_PAD + ln, :] = qk_ref[...]
    conv = cb_ref[...] + cbuf[lo:lo + ln, :] * cw_ref[0:1, :]
    for jj in range(1, CONV_W):
        conv = conv + cbuf[lo + jj:lo + jj + ln, :] * cw_ref[jj:jj + 1, :]
    tail = cbuf[lo + ln:CONV_PAD + ln, :]
    cbuf[lo:CONV_PAD, :] = tail
    act = _silu(conv)

    gcb = gc_ref[...] + gbr_ref[...]
    grb = gr_ref[...] + gbc_ref[...]
    lfc = _log_sigmoid(gcb)
    lfr = _log_sigmoid(grb)
    row = lax.broadcasted_iota(jnp.int32, (ln, ln), 0)
    col = lax.broadcasted_iota(jnp.int32, (ln, ln), 1)
    causal = col <= row

    for h in range(N_HEADS):
        sl = slice(h * HEAD_DIM, (h + 1) * HEAD_DIM)
        q = act[:, h * HEAD_DIM:(h + 1) * HEAD_DIM]
        k = act[:, BR_WIDTH + h * HEAD_DIM:BR_WIDTH + (h + 1) * HEAD_DIM] * HEAD_SCALE
        v = v_ref[:, sl]
        i_col = gcb[:, h:h + 1]
        lf_col = lfc[:, N_HEADS + h:N_HEADS + h + 1]
        i_row = grb[h:h + 1, :]
        lf_row = lfr[N_HEADS + h:N_HEADS + h + 1, :]
        b_col = jnp.sum(jnp.where(causal, lf_row, 0.0), axis=1, keepdims=True)
        b_row = jnp.sum(jnp.where(row <= col, lf_col, 0.0), axis=0, keepdims=True)
        b_last = b_col[ln - 1:ln, :]
        m_old = m_scr[h][:, 0:1]
        c_old = c_scr[h]
        n_old = n_scr[h]

        dlog = jnp.where(causal, b_col - b_row + i_row, NEG)
        inter = b_col + m_old
        mt = jnp.maximum(inter, jnp.max(dlog, axis=1, keepdims=True))
        w = jnp.exp(dlog - mt)
        wi = jnp.exp(inter - mt)
        qb = q.astype(bf16)
        vb = v.astype(bf16)
        qk = lax.dot_general(qb, k.astype(bf16), NT_DIMS, preferred_element_type=f32) * w
        num = (jnp.dot(qk.astype(bf16), vb, preferred_element_type=f32)
               + wi * jnp.dot(qb, c_old.astype(bf16), preferred_element_type=f32))
        den = jnp.sum(qk, axis=1, keepdims=True) + wi * jnp.sum(q * n_old, axis=1, keepdims=True)
        hh = num / jnp.maximum(jnp.abs(den), jnp.exp(-mt))

        m_new = mt[ln - 1:ln, :]
        ws = jnp.exp(b_last - b_col + i_col - m_new)
        dec = jnp.exp(b_last + m_old - m_new)
        kw = k * ws
        c_scr[h] = dec * c_old + lax.dot_general(kw.astype(bf16), vb, TN_DIMS, preferred_element_type=f32)
        n_scr[h] = dec * n_old + jnp.sum(kw, axis=0, keepdims=True)
        m_scr[h] = jnp.broadcast_to(m_new, (1, HEAD_DIM))

        y = _head_norm(_sigmoid(og_ref[:, sl]) * hh, gn_ref[:, sl])
        out_ref[:, sl] = (y * _silu(z_ref[:, sl])).astype(bf16)

    @on_last
    def _():
        c_out_ref[...] = c_scr[...]
        n_out_ref[...] = n_scr[...]
        m_out_ref[...] = m_scr[...]


def _mlstm(u, gates_r, conv_w, conv_b, b_ig, b_fg, gn, c0, n0, m0, conv0, ng, nc, ln):
    gbias = jnp.concatenate([b_ig, b_fg]).astype(f32)
    gbr = jnp.zeros((1, GATE_PAD), f32).at[0, :2 * N_HEADS].set(gbias)
    gbc = gbias.reshape(2 * N_HEADS, 1)

    def blk(col, width):
        return pl.BlockSpec((ln, width), lambda g, c: (g * nc + c, col // width))

    def const(shape):
        return pl.BlockSpec(shape, lambda g, c: (0,) * len(shape))

    def per_seq(shape):
        return pl.BlockSpec((None,) + shape, lambda g, c: (g,) + (0,) * len(shape))

    state_shapes = [jax.ShapeDtypeStruct((ng, N_HEADS, HEAD_DIM, HEAD_DIM), f32),
                    jax.ShapeDtypeStruct((ng, N_HEADS, 1, HEAD_DIM), f32),
                    jax.ShapeDtypeStruct((ng, N_HEADS, 1, HEAD_DIM), f32)]
    state_specs = [per_seq((N_HEADS, HEAD_DIM, HEAD_DIM)),
                   per_seq((N_HEADS, 1, HEAD_DIM)),
                   per_seq((N_HEADS, 1, HEAD_DIM))]
    out, c_new, n_new, m_new = pl.pallas_call(
        _mlstm_body,
        out_shape=[jax.ShapeDtypeStruct((ng * nc * ln, BR_WIDTH), bf16)] + state_shapes,
        grid=(ng, nc),
        in_specs=[blk(U_BQK, B_QK), blk(U_BV, BR_WIDTH), blk(U_BO, BR_WIDTH), blk(U_BZ, BR_WIDTH),
                  blk(U_GATES, GATE_PAD),
                  pl.BlockSpec((None, 2 * N_HEADS, ln), lambda g, c: (g * nc + c, 0, 0)),
                  const((CONV_W, B_QK)), const((1, B_QK)), const((1, GATE_PAD)), const((2 * N_HEADS, 1)),
                  const((1, BR_WIDTH))] + state_specs + [per_seq((CONV_W - 1, B_QK))],
        out_specs=[pl.BlockSpec((ln, BR_WIDTH), lambda g, c: (g * nc + c, 0))] + state_specs,
        scratch_shapes=[pltpu.VMEM((CONV_PAD + ln, B_QK), f32),
                        pltpu.VMEM((N_HEADS, HEAD_DIM, HEAD_DIM), f32),
                        pltpu.VMEM((N_HEADS, 1, HEAD_DIM), f32),
                        pltpu.VMEM((N_HEADS, 1, HEAD_DIM), f32)],
        compiler_params=_params("parallel", "arbitrary"),
        name="mlstm",
    )(u, u, u, u, u, gates_r, conv_w, conv_b.reshape(1, B_QK), gbr, gbc, gn.reshape(1, BR_WIDTH),
      c0, n0.reshape(ng, N_HEADS, 1, HEAD_DIM),
      jnp.broadcast_to(m0[:, :, None, None], (ng, N_HEADS, 1, HEAD_DIM)), conv0)
    return out, c_new, n_new[:, :, 0, :], m_new[:, :, 0, 0]


def _ret_body(q_ref, k_ref, v_ref, z_ref, cc_ref, ss_ref, gn_ref, s0_ref, out_ref, s_out_ref, s_scr,
              *, chunk_axis=1):
    ln = q_ref.shape[0]
    on_first, on_last = _first_last(chunk_axis)

    @on_first
    def _():
        s_scr[...] = s0_ref[...]

    row = lax.broadcasted_iota(jnp.int32, (ln, ln), 0)
    col = lax.broadcasted_iota(jnp.int32, (ln, ln), 1)
    rel = (row - col).astype(f32)
    tpos = lax.broadcasted_iota(jnp.int32, (ln, 1), 0).astype(f32)
    cc = cc_ref[...]
    ss = ss_ref[...]
    for h in range(N_HEADS):
        sl = slice(h * HEAD_DIM, (h + 1) * HEAD_DIM)
        lg = math.log(1.0 - 2.0 ** (-RET_GAMMA_EXP0 - h))
        q = q_ref[:, sl]
        k = k_ref[:, sl]
        qr = q * cc + pltpu.roll(q, DIFF_DH, 1) * ss
        kr = (k * cc + pltpu.roll(k, DIFF_DH, 1) * ss) * HEAD_SCALE
        vb = v_ref[:, sl].astype(bf16)
        decay = jnp.where(rel >= 0.0, jnp.exp(jnp.maximum(rel, 0.0) * lg), 0.0)
        att = lax.dot_general(qr.astype(bf16), kr.astype(bf16), NT_DIMS, preferred_element_type=f32) * decay
        q_dec = qr * jnp.exp((tpos + 1.0) * lg)
        s_old = s_scr[h]
        o = (jnp.dot(att.astype(bf16), vb, preferred_element_type=f32)
             + jnp.dot(q_dec.astype(bf16), s_old.astype(bf16), preferred_element_type=f32))
        k_dec = kr * jnp.exp((ln - 1.0 - tpos) * lg)
        s_scr[h] = math.exp(ln * lg) * s_old + lax.dot_general(k_dec.astype(bf16), vb, TN_DIMS,
                                                               preferred_element_type=f32)
        out_ref[:, sl] = (_head_norm(o, gn_ref[:, sl]) * _silu(z_ref[:, sl])).astype(bf16)

    @on_last
    def _():
        s_out_ref[...] = s_scr[...]


def _rope_tables(pos):
    inv = ROPE_BASE ** (-jnp.arange(0, HEAD_DIM, 2, dtype=f32) / HEAD_DIM)
    ang = pos.astype(f32)[:, None] * inv[None, :]
    cos, sin = jnp.cos(ang), jnp.sin(ang)
    return jnp.concatenate([cos, cos], axis=-1), jnp.concatenate([-sin, sin], axis=-1)


def _retention(u, pos, gn, s0, ng, nc, ln):
    cc, ss = _rope_tables(pos)

    def blk(col):
        return pl.BlockSpec((ln, BR_WIDTH), lambda g, c: (g * nc + c, col // BR_WIDTH))

    rope_spec = pl.BlockSpec((ln, HEAD_DIM), lambda g, c: (c, 0))
    state_spec = pl.BlockSpec((None, N_HEADS, HEAD_DIM, HEAD_DIM), lambda g, c: (g, 0, 0, 0))
    return pl.pallas_call(
        _ret_body,
        out_shape=[jax.ShapeDtypeStruct((ng * nc * ln, BR_WIDTH), bf16),
                   jax.ShapeDtypeStruct((ng, N_HEADS, HEAD_DIM, HEAD_DIM), f32)],
        grid=(ng, nc),
        in_specs=[blk(U_DQ), blk(U_DK), blk(U_DV), blk(U_DZ), rope_spec, rope_spec,
                  pl.BlockSpec((1, BR_WIDTH), lambda g, c: (0, 0)), state_spec],
        out_specs=[pl.BlockSpec((ln, BR_WIDTH), lambda g, c: (g * nc + c, 0)), state_spec],
        scratch_shapes=[pltpu.VMEM((N_HEADS, HEAD_DIM, HEAD_DIM), f32)],
        compiler_params=_params("parallel", "arbitrary"),
        name="retention",
    )(u, u, u, u, cc, ss, gn.reshape(1, BR_WIDTH), s0)


def _sample_branches_body(*refs, lam_init):
    it = iter(refs)

    def take(n):
        return [next(it) for _ in range(n)]

    a_in, c_in, m_in, b_in, d_in = take(8), take(8), take(4), take(15), take(8)
    a_out, c_out, m_out, b_out, d_out = take(3), take(3), take(1), take(4), take(2)
    b_scr, d_scr = take(4), take(1)
    _band_sample_body(*a_in, *a_out)
    _diff_sample_body(*c_in, *c_out, lam_init=lam_init)
    _mem_sample_body(*m_in, *m_out)
    _mlstm_body(*b_in, *b_out, *b_scr, chunk_axis=None)
    _ret_body(*d_in, *d_out, *d_scr, chunk_axis=None)


def _sample_branches(u, layer, caches, rel_bias, lamp, lam_init, gains, conv_w, conv_b, b_ig, b_fg,
                     states, pos, nb, t):
    ca_k, ca_v, cc_k, cc_v, cm_k, cm_v = caches
    subln_c, gn_b, gn_d = gains
    c0, n0, m0, conv0, s0 = states
    nrow = ca_k.shape[2] // N_HEADS
    bias_c = _rel_bias_toeplitz(rel_bias, t, nrow, nrow)
    bias_c = jnp.repeat(bias_c, N_HEADS, axis=2).reshape(N_HEADS * t, nrow * N_HEADS)
    bias_n = jnp.tile(_rel_bias_toeplitz(rel_bias, t, t, 0), (1, 1, N_HEADS)).reshape(N_HEADS * t, N_HEADS * t)
    gbias = jnp.concatenate([b_ig, b_fg]).astype(f32)
    gbr = jnp.zeros((1, GATE_PAD), f32).at[0, :2 * N_HEADS].set(gbias)
    cc, ss = _rope_tables(pos)

    def blk(col, width=BR_WIDTH):
        return pl.BlockSpec((t, width), lambda b: (b, col // width))

    def const(shape):
        return pl.BlockSpec(shape, lambda b: (0,) * len(shape))

    def per_seq(shape):
        return pl.BlockSpec((None,) + shape, lambda b: (b,) + (0,) * len(shape))

    def cache(arr):
        return pl.BlockSpec((None, None) + arr.shape[2:], lambda b: (layer, b, 0, 0))

    mat, vec = (N_HEADS, HEAD_DIM, HEAD_DIM), (N_HEADS, 1, HEAD_DIM)
    kv_state = (t * N_HEADS, HEAD_DIM)
    in_specs = (
        [blk(U_AQ), blk(U_AK), blk(U_AV), blk(U_AZ), cache(ca_k), cache(ca_v),
         const((N_HEADS * t, nrow * N_HEADS)), const((N_HEADS * t, N_HEADS * t))]
        + [blk(U_CQ), blk(U_CK), blk(U_CV), blk(U_CZ), cache(cc_k), cache(cc_v),
           const((4, DIFF_DH)), const((1, BR_WIDTH))]
        + [blk(U_MQ), blk(U_MZ), cache(cm_k), cache(cm_v)]
        + [blk(U_BQK, B_QK), blk(U_BV), blk(U_BO), blk(U_BZ), blk(U_GATES, GATE_PAD),
           per_seq((2 * N_HEADS, t)), const((CONV_W, B_QK)), const((1, B_QK)), const((1, GATE_PAD)),
           const((2 * N_HEADS, 1)), const((1, BR_WIDTH)), per_seq(mat), per_seq(vec), per_seq(vec),
           per_seq((CONV_W - 1, B_QK))]
        + [blk(U_DQ), blk(U_DK), blk(U_DV), blk(U_DZ), const((t, HEAD_DIM)), const((t, HEAD_DIM)),
           const((1, BR_WIDTH)), per_seq(mat)])
    args = (
        [u, u, u, u, ca_k, ca_v, bias_c, bias_n]
        + [u, u, u, u, cc_k, cc_v, lamp, subln_c.reshape(1, BR_WIDTH)]
        + [u, u, cm_k, cm_v]
        + [u, u, u, u, u, _gates_rowform(u, t), conv_w, conv_b.reshape(1, B_QK), gbr,
           gbias.reshape(2 * N_HEADS, 1), gn_b.reshape(1, BR_WIDTH), c0,
           n0.reshape(nb, N_HEADS, 1, HEAD_DIM),
           jnp.broadcast_to(m0[:, :, None, None], (nb, N_HEADS, 1, HEAD_DIM)), conv0]
        + [u, u, u, u, cc, ss, gn_d.reshape(1, BR_WIDTH), s0])
    branch_out = jax.ShapeDtypeStruct((nb * t, BR_WIDTH), bf16)
    kv_out = jax.ShapeDtypeStruct((nb,) + kv_state, f32)
    mat_out = jax.ShapeDtypeStruct((nb,) + mat, f32)
    vec_out = jax.ShapeDtypeStruct((nb,) + vec, f32)
    out_shape = ([branch_out, kv_out, kv_out] * 2 + [branch_out]
                 + [branch_out, mat_out, vec_out, vec_out] + [branch_out, mat_out])
    o_spec = pl.BlockSpec((t, BR_WIDTH), lambda b: (b, 0))
    out_specs = ([o_spec, per_seq(kv_state), per_seq(kv_state)] * 2 + [o_spec]
                 + [o_spec, per_seq(mat), per_seq(vec), per_seq(vec)] + [o_spec, per_seq(mat)])
    (oa, ak, av, oc, ck, cv, om, ob, c_new, n_new, m_new, od, s_new) = pl.pallas_call(
        functools.partial(_sample_branches_body, lam_init=lam_init),
        out_shape=out_shape,
        grid=(nb,),
        in_specs=in_specs,
        out_specs=out_specs,
        scratch_shapes=[pltpu.VMEM((CONV_PAD + t, B_QK), f32), pltpu.VMEM(mat, f32),
                        pltpu.VMEM(vec, f32), pltpu.VMEM(vec, f32), pltpu.VMEM(mat, f32)],
        compiler_params=_params("arbitrary"),
        name="sample_branches",
    )(*args)
    return ((oa, ob, oc, od, om), (ak, av, ck, cv),
            (c_new, n_new[:, :, 0, :], m_new[:, :, 0, 0], s_new))


def _merge_body(x_ref, g_ref, oa_ref, ob_ref, oc_ref, od_ref, om_ref,
                wg0_ref, wg1_ref, wg2_ref, wg3_ref, wg4_ref, wb_ref, wo_ref, fg_ref,
                wo_last_ref, y_ref, h_scr, mrg_scr, *, rows, final):
    n = pl.program_id(1)

    @pl.when(n == 0)
    def _():
        _rmsnorm_rows(x_ref, g_ref, h_scr, rows)
        y_ref[...] = jnp.zeros(y_ref.shape, f32)
        mrg_scr[...] = jnp.zeros(mrg_scr.shape, bf16)

    y_ref[...] += jnp.dot(mrg_scr[...], wo_ref[...], preferred_element_type=f32)
    h = h_scr[...]
    merged = None
    for i, (o_ref, wg_ref) in enumerate(zip((oa_ref, ob_ref, oc_ref, od_ref, om_ref),
                                            (wg0_ref, wg1_ref, wg2_ref, wg3_ref, wg4_ref))):
        gate = _sigmoid(lax.dot_general(h, wg_ref[...], NT_DIMS, preferred_element_type=f32))
        term = gate * jnp.dot(o_ref[...], wb_ref[i], preferred_element_type=f32)
        merged = term if merged is None else merged + term
    mrg_scr[...] = merged.astype(bf16)

    @pl.when(n == pl.num_programs(1) - 1)
    def _():
        y_ref[...] += jnp.dot(mrg_scr[...], wo_last_ref[...], preferred_element_type=f32)
        tm = x_ref.shape[0]
        for r in range(0, tm, rows):
            y = x_ref[r:r + rows, :] + y_ref[r:r + rows, :]
            if final:
                y = (y * lax.rsqrt(jnp.mean(y * y, axis=-1, keepdims=True) + EPS)) * fg_ref[...]
            y_ref[r:r + rows, :] = y


def _merge(x, g, outs, wg, wb, wo, fg, *, final, name, tm_target=704, tn=256):
    m, d = x.shape
    tm = _divisor_tile(m, tm_target, 16)
    rows = _divisor_tile(tm, 256, 8)
    nn = d // tn

    def wg_spec(i):
        return pl.BlockSpec((tn, d), lambda r, n: (i * nn + n, 0))

    o_spec = pl.BlockSpec((tm, BR_WIDTH), lambda r, n: (r, 0))
    return pl.pallas_call(
        functools.partial(_merge_body, rows=rows, final=final),
        out_shape=jax.ShapeDtypeStruct((m, d), f32),
        grid=(m // tm, nn),
        in_specs=[pl.BlockSpec((tm, d), lambda r, n: (r, 0)),
                  pl.BlockSpec((1, d), lambda r, n: (0, 0))]
                 + [o_spec] * N_BRANCH
                 + [wg_spec(i) for i in range(N_BRANCH)]
                 + [pl.BlockSpec((N_BRANCH, BR_WIDTH, tn), lambda r, n: (0, 0, n)),
                    pl.BlockSpec((tn, d), lambda r, n: (jnp.maximum(n - 1, 0), 0)),
                    pl.BlockSpec((1, d), lambda r, n: (0, 0)),
                    pl.BlockSpec((tn, d), lambda r, n: (nn - 1, 0))],
        out_specs=pl.BlockSpec((tm, d), lambda r, n: (r, 0)),
        scratch_shapes=[pltpu.VMEM((tm, d), bf16), pltpu.VMEM((tm, tn), bf16)],
        compiler_params=_params("parallel", "arbitrary"),
        name=name,
    )(x, g.reshape(1, d), *outs, wg, wg, wg, wg, wg, wb, wo, fg.reshape(1, d), wo)


F32_SUBLANES = 8


def _cast_rows_body(src_ref, o_ref, *, valid_rows_last):
    tr = o_ref.shape[0]
    x = src_ref[0]
    if valid_rows_last is not None:
        row = lax.broadcasted_iota(jnp.int32, (tr, 1), 0)
        keep = jnp.logical_or(pl.program_id(0) < pl.num_programs(0) - 1, row < valid_rows_last)
        x = jnp.where(keep, x, 0.0)
    o_ref[...] = x.astype(bf16)


def _cast_rows(w_t, layer, *, n_rows, tr, src_row, valid_rows_last, name):
    d = w_t.shape[2]
    assert n_rows % tr == 0
    return pl.pallas_call(
        functools.partial(_cast_rows_body, valid_rows_last=valid_rows_last),
        out_shape=jax.ShapeDtypeStruct((n_rows, d), bf16),
        grid=(n_rows // tr,),
        in_specs=[pl.BlockSpec((pl.Element(1), pl.Element(tr), pl.Element(d)),
                               lambda j: (layer, pl.multiple_of(src_row(j), F32_SUBLANES), 0))],
        out_specs=pl.BlockSpec((tr, d), lambda j: (j, 0)),
        compiler_params=_params("arbitrary"),
        name=name,
    )(w_t)


def _repack_w_in(w_t, layer):
    tr = MXU_COLS
    n_plain, n_main = W_BI // tr, U_GATES // tr

    def u_src(j):
        return jnp.where(j < n_plain, j * tr, jnp.where(j < n_main, j * tr + (W_BZ - W_BI), W_BI))

    wu = _cast_rows(w_t, layer, n_rows=U_WIDTH, tr=tr, src_row=u_src, valid_rows_last=2 * N_HEADS,
                    name="repack_u")
    wg = _cast_rows(w_t, layer, n_rows=N_BRANCH * D_MODEL, tr=512, src_row=lambda j: W_GATE + j * 512,
                    valid_rows_last=None, name="repack_g")
    return wu, wg


def _gates_rowform(u, ln):
    g = u[:, U_GATES:U_GATES + 2 * N_HEADS]
    return g.reshape(u.shape[0] // ln, ln, 2 * N_HEADS).transpose(0, 2, 1)


def kernel(x_prompt, x_sample, mem_prompt, cache_a_k, cache_a_v, cache_c_k, cache_c_v, cache_mem_k, cache_mem_v, state_b_C, state_b_n, state_b_m, state_b_conv, state_d_S, norm_g, w_in, conv_w, conv_b, b_ig, b_fg, rel_bias, lam_q1, lam_k1, lam_q2, lam_k2, gn_b, subln_c, gn_d, mem_norm_g, w_mk, w_mv, w_branch, w_out, final_g):
    batch, seq, d = x_prompt.shape
    nb, t, _ = x_sample.shape
    depth = w_in.shape[0]
    n_mem = mem_prompt.shape[1]
    past = cache_c_k.shape[2]
    mp = batch * seq
    assert d == D_MODEL and seq % C_BLOCK == 0 and seq % A_QBLOCK == 0 and seq % SCAN_CHUNK == 0

    xp = x_prompt.reshape(mp, d)
    xs = x_sample.reshape(nb * t, d)
    mem = mem_prompt.reshape(batch * n_mem, d)
    ca_k, ca_v = _cache_rows(cache_a_k), _cache_rows(cache_a_v)
    cc_k, cc_v = _cache_rows(cache_c_k), _cache_rows(cache_c_v)
    cm_k, cm_v = _cache_rows(cache_mem_k), _cache_rows(cache_mem_v)
    pos_p = jnp.arange(seq)
    pos_s = past + jnp.arange(t)
    a_rows = min(A_WINDOW, seq)
    mem_tq = _divisor_tile(seq, 512, 16)

    w_t = jnp.swapaxes(w_in, 1, 2)

    sp, ss = [], []
    for l in range(depth):
        lam_init = 0.8 - 0.6 * math.exp(-0.3 * l)
        lamp = jnp.stack([lam_q1[l], lam_k1[l], lam_q2[l], lam_k2[l]]).astype(f32)
        wu, wg = _repack_w_in(w_t, l)
        u = _norm_matmul(xp, norm_g[l], wu, tm_target=1024, tn=U_TILE, w_rows_are_outputs=True,
                         name="in_proj")
        us = _norm_matmul(xs, norm_g[l], wu, tm_target=1024, tn=U_TILE, w_rows_are_outputs=True,
                          name="in_proj_sample")
        mkv = _norm_matmul(mem, mem_norm_g[l], jnp.concatenate([w_mk[l], w_mv[l]], axis=1).astype(bf16),
                           tm_target=512, tn=512, w_rows_are_outputs=False, name="mem_proj")

        oa_p, ak_p, av_p = _band_prompt(u, rel_bias[l], batch, seq, a_rows)
        zeros_c = jnp.zeros((batch, N_HEADS, HEAD_DIM, HEAD_DIM), f32)
        ob_p, c_p, n_p, m_p = _mlstm(
            u, _gates_rowform(u, SCAN_CHUNK), conv_w[l], conv_b[l], b_ig[l], b_fg[l], gn_b[l],
            zeros_c, jnp.zeros((batch, N_HEADS, HEAD_DIM), f32), jnp.zeros((batch, N_HEADS), f32),
            jnp.zeros((batch, CONV_W - 1, B_QK), f32), batch, seq // SCAN_CHUNK, SCAN_CHUNK)
        oc_p, ck_p, cv_p = _diff_prompt(u, lamp, subln_c[l], lam_init, batch, seq)
        od_p, s_p = _retention(u, pos_p, gn_d[l], zeros_c, batch, seq // SCAN_CHUNK, SCAN_CHUNK)
        om_p = _mem_attn_prompt(u, mkv, batch, seq, mem_tq, n_mem)

        outs_s, (ak_s, av_s, ck_s, cv_s), (c_s, n_s, m_s, s_s) = _sample_branches(
            us, l, (ca_k, ca_v, cc_k, cc_v, cm_k, cm_v), rel_bias[l], lamp, lam_init,
            (subln_c[l], gn_b[l], gn_d[l]), conv_w[l], conv_b[l], b_ig[l], b_fg[l],
            (state_b_C[l].astype(f32), state_b_n[l].astype(f32), state_b_m[l].astype(f32),
             state_b_conv[l].astype(f32), state_d_S[l].astype(f32)), pos_s, nb, t)

        wb, wo = w_branch[l].astype(bf16), w_out[l].astype(bf16)
        final = l == depth - 1
        xp = _merge(xp, norm_g[l], (oa_p, ob_p, oc_p, od_p, om_p), wg, wb, wo, final_g, final=final,
                    name="gated_merge")
        xs = _merge(xs, norm_g[l], outs_s, wg, wb, wo, final_g, final=final, name="gated_merge_sample")

        def conv_tail(arr, n_seq, rows):
            return jnp.stack([arr[(i + 1) * rows - (CONV_W - 1):(i + 1) * rows, U_BQK:U_BQK + B_QK]
                              for i in range(n_seq)])

        def hd(a):
            return a.reshape(a.shape[:2] + (N_HEADS, HEAD_DIM))

        def rows_hd(a):
            return a.reshape(a.shape[0], a.shape[1] // N_HEADS, N_HEADS, HEAD_DIM)

        sp.append((rows_hd(ak_p), rows_hd(av_p), rows_hd(ck_p), rows_hd(cv_p),
                   hd(mkv[:, :BR_WIDTH].reshape(batch, n_mem, BR_WIDTH)),
                   hd(mkv[:, BR_WIDTH:].reshape(batch, n_mem, BR_WIDTH)),
                   c_p, n_p, m_p, conv_tail(u, batch, seq), s_p))
        ss.append((rows_hd(ak_s), rows_hd(av_s), rows_hd(ck_s), rows_hd(cv_s),
                   c_s, n_s, m_s, us[:, U_BQK:U_BQK + B_QK].reshape(nb, t, B_QK)[:, t - (CONV_W - 1):], s_s))

    y_prompt = xp.reshape(batch, seq, d)
    y_sample = xs.reshape(nb, t, d)
    p_states = tuple(jnp.stack([st[i] for st in sp]) for i in range(11))
    s_states = tuple(jnp.stack([st[i] for st in ss]) for i in range(9))
    return (y_prompt, y_sample) + p_states + s_states
```

```python
import functools
import math

import jax
import jax.numpy as jnp
import numpy as np
from jax import lax
from jax.experimental import pallas as pl
from jax.experimental.pallas import tpu as pltpu

f32 = jnp.float32
bf16 = jnp.bfloat16

D_MODEL = 2048
N_HEADS = 4
HEAD_DIM = 128
BR_WIDTH = N_HEADS * HEAD_DIM
N_BRANCH = 5
CHUNK = 64
A_PREV_CHUNKS = 8
A_WINDOW = A_PREV_CHUNKS * CHUNK
A_BAND = (A_PREV_CHUNKS + 1) * CHUNK
A_REL_CLIP = 128
CONV_W = 4
B_QK = 2 * BR_WIDTH
DIFF_DH = HEAD_DIM // 2
RET_GAMMA_EXP0 = 5.0
ROPE_BASE = 10000.0
EPS = 1e-6
NEG = -1e30
HEAD_SCALE = HEAD_DIM ** -0.5
DIFF_SCALE = DIFF_DH ** -0.5
LOG2_E = math.log2(math.e)

U_AQ, U_AK, U_AV, U_AZ = 0, 512, 1024, 1536
U_BQK, U_BV, U_BO, U_BZ = 2048, 3072, 3584, 4096
U_CQ, U_CK, U_CV, U_CZ = 4608, 5120, 5632, 6144
U_DQ, U_DK, U_DV, U_DZ = 6656, 7168, 7680, 8192
U_MQ, U_MZ = 8704, 9216
U_GATES = 9728
GATE_PAD = 128
MXU_COLS = 256
U_WIDTH = U_GATES + MXU_COLS
U_TILE = 3 * MXU_COLS
W_BI = 4096
W_BZ = 4104
W_GATE = 9736

VMEM_LIMIT_BYTES = 56 * 1024 * 1024
A_QBLOCK = 4 * CHUNK
C_BLOCK = 512
SCAN_CHUNK = 256
NT_DIMS = (((1,), (1,)), ((), ()))
TN_DIMS = (((0,), (0,)), ((), ()))


def _divisor_tile(n, target, multiple):
    best = None
    for t in range(multiple, min(n, target) + 1, multiple):
        if n % t == 0:
            best = t
    assert best is not None, (n, target, multiple)
    return best


def _params(*sem):
    return pltpu.CompilerParams(dimension_semantics=sem, vmem_limit_bytes=VMEM_LIMIT_BYTES)


def _sigmoid(x):
    return 1.0 / (1.0 + jnp.exp(-x))


def _silu(x):
    return x * _sigmoid(x)


def _log_sigmoid(x):
    return jnp.minimum(x, 0.0) - jnp.log1p(jnp.exp(-jnp.abs(x)))


def _head_norm(y, gain):
    return y * lax.rsqrt(jnp.mean(y * y, axis=-1, keepdims=True) + EPS) * gain


def _rmsnorm_rows(x_ref, g_ref, h_ref, rows):
    tm = x_ref.shape[0]
    for r in range(0, tm, rows):
        x = x_ref[r:r + rows, :]
        ms = jnp.mean(x * x, axis=-1, keepdims=True)
        h_ref[r:r + rows, :] = ((x * lax.rsqrt(ms + EPS)) * g_ref[...]).astype(bf16)


def _norm_matmul_body(x_ref, g_ref, w_ref, o_ref, h_ref, *, rows, w_rows_are_outputs):
    @pl.when(pl.program_id(1) == 0)
    def _():
        _rmsnorm_rows(x_ref, g_ref, h_ref, rows)

    if w_rows_are_outputs:
        o = lax.dot_general(h_ref[...], w_ref[...], NT_DIMS, preferred_element_type=f32)
    else:
        o = jnp.dot(h_ref[...], w_ref[...], preferred_element_type=f32)
    o_ref[...] = o.astype(o_ref.dtype)


def _norm_matmul(x, g, w, *, tm_target, tn, w_rows_are_outputs, name):
    m, d = x.shape
    n = w.shape[0] if w_rows_are_outputs else w.shape[1]
    tm = _divisor_tile(m, tm_target, 16)
    rows = _divisor_tile(tm, 256, 8)
    assert n % tn == 0
    w_spec = (pl.BlockSpec((tn, d), lambda i, j: (j, 0)) if w_rows_are_outputs
              else pl.BlockSpec((d, tn), lambda i, j: (0, j)))
    return pl.pallas_call(
        functools.partial(_norm_matmul_body, rows=rows, w_rows_are_outputs=w_rows_are_outputs),
        out_shape=[jax.ShapeDtypeStruct((m, n), f32), jax.ShapeDtypeStruct((m, d), bf16)],
        grid=(m // tm, n // tn),
        in_specs=[pl.BlockSpec((tm, d), lambda i, j: (i, 0)),
                  pl.BlockSpec((1, d), lambda i, j: (0, 0)),
                  w_spec],
        out_specs=[pl.BlockSpec((tm, tn), lambda i, j: (i, j)),
                   pl.BlockSpec((tm, d), lambda i, j: (i, 0))],
        compiler_params=_params("parallel", "arbitrary"),
        name=name,
    )(x, g.reshape(1, d), w)


def _store_head_rows(dst_ref, src_ref):
    rows = src_ref.shape[0]
    for h in range(N_HEADS):
        dst_ref[pl.ds(h, rows, stride=N_HEADS), :] = src_ref[:, h * HEAD_DIM:(h + 1) * HEAD_DIM]


def _band_prompt_body(q_ref, k0_ref, k1_ref, k2_ref, v0_ref, v1_ref, v2_ref, z_ref, bias_ref,
                      o_ref, ks_ref, vs_ref, *, n_tail):
    tq = q_ref.shape[0]
    i = pl.program_id(1)

    @pl.when(i >= pl.num_programs(1) - n_tail)
    def _():
        _store_head_rows(ks_ref, k2_ref)
        _store_head_rows(vs_ref, v2_ref)

    w_idx = lax.broadcasted_iota(jnp.int32, (tq, 3 * tq), 1)
    valid = (w_idx + (i - 2) * tq) >= 0
    for h in range(N_HEADS):
        sl = slice(h * HEAD_DIM, (h + 1) * HEAD_DIM)
        q = q_ref[:, sl].astype(bf16)
        kw = jnp.concatenate([k0_ref[:, sl], k1_ref[:, sl], k2_ref[:, sl]], axis=0).astype(bf16)
        vw = jnp.concatenate([v0_ref[:, sl], v1_ref[:, sl], v2_ref[:, sl]], axis=0).astype(bf16)
        s = lax.dot_general(q, kw, NT_DIMS, preferred_element_type=f32) * HEAD_SCALE + bias_ref[h]
        s = jnp.where(valid, s, NEG)
        p = jnp.exp(s - jnp.max(s, axis=-1, keepdims=True))
        l = jnp.sum(p, axis=-1, keepdims=True)
        o = jnp.dot(p.astype(bf16), vw, preferred_element_type=f32) / l
        o_ref[:, sl] = (o * _silu(z_ref[:, sl])).astype(bf16)


def _rel_bias_toeplitz(rel_bias, rows, cols, rel0):
    p = rows + cols
    j = np.arange(p)
    d = np.where(j < cols, j, j - p)
    idx = np.clip(rel0 - d, -A_REL_CLIP, A_REL_CLIP) + A_REL_CLIP
    v = rel_bias.astype(f32)[:, idx]
    flat = jnp.tile(v, (1, rows))[:, :rows * (p - 1)]
    return flat.reshape(rel_bias.shape[0], rows, p - 1)[:, :, :cols]


def _band_bias_prompt(rel_bias):
    r = np.arange(A_QBLOCK)[:, None]
    w = np.arange(3 * A_QBLOCK)[None, :]
    kj = w - CHUNK * (r // CHUNK)
    inside = (kj >= 0) & (kj < A_BAND)
    table = _rel_bias_toeplitz(rel_bias, A_QBLOCK, 3 * A_QBLOCK, 2 * A_QBLOCK)
    return jnp.where(jnp.asarray(inside)[None], table, NEG)


def _band_prompt(u, rel_bias, batch, seq, a_rows):
    tq = A_QBLOCK
    nqb = seq // tq
    assert a_rows % tq == 0
    n_tail = a_rows // tq
    bias = _band_bias_prompt(rel_bias)

    def blk(col, back):
        return pl.BlockSpec((tq, BR_WIDTH),
                            lambda b, i: (b * nqb + jnp.maximum(i - back, 0), col // BR_WIDTH))

    state_shape = jax.ShapeDtypeStruct((batch, a_rows * N_HEADS, HEAD_DIM), f32)
    state_spec = pl.BlockSpec((None, tq * N_HEADS, HEAD_DIM),
                              lambda b, i: (b, jnp.maximum(i - (nqb - n_tail), 0), 0))
    return pl.pallas_call(
        functools.partial(_band_prompt_body, n_tail=n_tail),
        out_shape=[jax.ShapeDtypeStruct((batch * seq, BR_WIDTH), bf16), state_shape, state_shape],
        grid=(batch, nqb),
        in_specs=[blk(U_AQ, 0),
                  blk(U_AK, 2), blk(U_AK, 1), blk(U_AK, 0),
                  blk(U_AV, 2), blk(U_AV, 1), blk(U_AV, 0),
                  blk(U_AZ, 0),
                  pl.BlockSpec((N_HEADS, tq, 3 * tq), lambda b, i: (0, 0, 0))],
        out_specs=[pl.BlockSpec((tq, BR_WIDTH), lambda b, i: (b * nqb + i, 0)), state_spec, state_spec],
        compiler_params=_params("parallel", "arbitrary"),
        name="band_attn_prompt",
    )(u, u, u, u, u, u, u, u, bias)


def _stack_heads(x):
    return jnp.concatenate([x[:, h * HEAD_DIM:(h + 1) * HEAD_DIM] for h in range(N_HEADS)], axis=0)


def _sample_attend(qs, t, ck_ref, cv_ref, k_new, v_new, scale, bias_c=None, bias_n=None):
    rows = qs.shape[0]
    row_head = (lax.broadcasted_iota(jnp.int32, (rows, 1), 0) // t) % N_HEADS
    ck = ck_ref[...].astype(bf16)
    s = lax.dot_general(qs, ck, NT_DIMS, preferred_element_type=f32) * scale
    if bias_c is not None:
        s = s + bias_c
    col_head = lax.broadcasted_iota(jnp.int32, (1, ck.shape[0]), 1) % N_HEADS
    parts = [(jnp.where(col_head == row_head, s, NEG), cv_ref[...].astype(bf16))]
    if k_new is not None:
        s = lax.dot_general(qs, _stack_heads(k_new).astype(bf16), NT_DIMS, preferred_element_type=f32) * scale
        if bias_n is not None:
            s = s + bias_n
        col_head = lax.broadcasted_iota(jnp.int32, (1, N_HEADS * t), 1) // t
        parts.append((jnp.where(col_head == row_head, s, NEG), _stack_heads(v_new).astype(bf16)))
    m = functools.reduce(jnp.maximum, [jnp.max(s, axis=-1, keepdims=True) for s, _ in parts])
    l, o = 0.0, 0.0
    for s, v in parts:
        p = jnp.exp(s - m)
        l = l + jnp.sum(p, axis=-1, keepdims=True)
        o = o + jnp.dot(p.astype(bf16), v, preferred_element_type=f32)
    return o / l


def _band_sample_body(q_ref, k_ref, v_ref, z_ref, ck_ref, cv_ref, bc_ref, bn_ref, o_ref, ks_ref, vs_ref):
    t = q_ref.shape[0]
    _store_head_rows(ks_ref, k_ref)
    _store_head_rows(vs_ref, v_ref)
    o = _sample_attend(_stack_heads(q_ref[...]).astype(bf16), t, ck_ref, cv_ref, k_ref[...], v_ref[...],
                       HEAD_SCALE, bc_ref[...], bn_ref[...])
    for h in range(N_HEADS):
        sl = slice(h * HEAD_DIM, (h + 1) * HEAD_DIM)
        o_ref[:, sl] = (o[h * t:(h + 1) * t] * _silu(z_ref[:, sl])).astype(bf16)


def _mem_sample_body(q_ref, z_ref, ck_ref, cv_ref, o_ref):
    t = q_ref.shape[0]
    o = _sample_attend(_stack_heads(q_ref[...]).astype(bf16), t, ck_ref, cv_ref, None, None, HEAD_SCALE)
    for h in range(N_HEADS):
        sl = slice(h * HEAD_DIM, (h + 1) * HEAD_DIM)
        o_ref[:, sl] = (o[h * t:(h + 1) * t] * _silu(z_ref[:, sl])).astype(bf16)


def _cache_rows(cache):
    d0, d1, rows = cache.shape[:3]
    return cache.reshape(d0, d1, rows * N_HEADS, HEAD_DIM)


def _mem_body(q_ref, z_ref, k_ref, v_ref, o_ref):
    for h in range(N_HEADS):
        sl = slice(h * HEAD_DIM, (h + 1) * HEAD_DIM)
        q = q_ref[:, sl].astype(bf16)
        k = k_ref[:, sl].astype(bf16)
        v = v_ref[:, sl].astype(bf16)
        s = lax.dot_general(q, k, NT_DIMS, preferred_element_type=f32) * HEAD_SCALE
        p = jnp.exp(s - jnp.max(s, axis=-1, keepdims=True))
        l = jnp.sum(p, axis=-1, keepdims=True)
        o = jnp.dot(p.astype(bf16), v, preferred_element_type=f32) / l
        o_ref[:, sl] = (o * _silu(z_ref[:, sl])).astype(bf16)


def _mem_attn_prompt(u, mkv, nb, t, tq, n_mem):
    nq = t // tq

    def blk(col):
        return pl.BlockSpec((tq, BR_WIDTH), lambda b, i: (b * nq + i, col // BR_WIDTH))

    return pl.pallas_call(
        _mem_body,
        out_shape=jax.ShapeDtypeStruct((nb * t, BR_WIDTH), bf16),
        grid=(nb, nq),
        in_specs=[blk(U_MQ), blk(U_MZ),
                  pl.BlockSpec((n_mem, BR_WIDTH), lambda b, i: (b, 0)),
                  pl.BlockSpec((n_mem, BR_WIDTH), lambda b, i: (b, 1))],
        out_specs=pl.BlockSpec((tq, BR_WIDTH), lambda b, i: (b * nq + i, 0)),
        compiler_params=_params("parallel", "arbitrary"),
        name="mem_attn_prompt",
    )(u, u, mkv, mkv)


def _diff_lambda(lamp_ref, lam_init):
    lp = lamp_ref[...]
    a = jnp.sum(lp[0:1] * lp[1:2], axis=-1, keepdims=True)
    b = jnp.sum(lp[2:3] * lp[3:4], axis=-1, keepdims=True)
    return jnp.exp(a) - jnp.exp(b) + lam_init


def _diff_epilogue(o0, o1, lam, gain, z, lam_init):
    y = _head_norm(o0 - lam * o1, gain) * (1.0 - lam_init)
    return (y * _silu(z)).astype(bf16)


def _half_masks():
    lane = lax.broadcasted_iota(jnp.int32, (1, HEAD_DIM), 1)
    return lane < DIFF_DH, lane >= DIFF_DH


def _diff_prompt_body(qt_ref, kt_ref, q_ref, k_ref, v_ref, z_ref, lamp_ref, gain_ref, o_ref, ks_ref, vs_ref,
                      m_scr, l_scr, acc_scr, *, lam_init):
    tq, tk = q_ref.shape[0], k_ref.shape[0]
    i = qt_ref[pl.program_id(1)]
    j = kt_ref[pl.program_id(1)]

    @pl.when(j == 0)
    def _():
        m_scr[...] = jnp.full(m_scr.shape, NEG, f32)
        l_scr[...] = jnp.zeros(l_scr.shape, f32)
        acc_scr[...] = jnp.zeros(acc_scr.shape, f32)

    lo, hi = _half_masks()

    def step(diagonal):
        if diagonal:
            kc = lax.broadcasted_iota(jnp.int32, (tk, 2 * tq), 0) // CHUNK
            qpos = lax.broadcasted_iota(jnp.int32, (tk, 2 * tq), 1)
            qc = jnp.where(qpos >= tq, qpos - tq, qpos) // CHUNK
            mask = kc <= qc
        for h in range(N_HEADS):
            sl = slice(h * HEAD_DIM, (h + 1) * HEAD_DIM)
            q = q_ref[:, sl] * (DIFF_SCALE * LOG2_E)
            qq = jnp.concatenate([jnp.where(lo, q, 0.0), jnp.where(hi, q, 0.0)], axis=0).astype(bf16)
            k = k_ref[:, sl].astype(bf16)
            vt = v_ref[:, sl].T.astype(bf16)
            s = lax.dot_general(k, qq, NT_DIMS, preferred_element_type=f32)
            if diagonal:
                s = jnp.where(mask, s, NEG)
            m_old = m_scr[h]
            m_new = jnp.maximum(m_old, jnp.max(s, axis=0, keepdims=True))
            p = jnp.exp2(s - m_new)
            alpha = jnp.exp2(m_old - m_new)
            l_scr[h] = alpha * l_scr[h] + jnp.sum(p, axis=0, keepdims=True)
            acc_scr[h] = alpha * acc_scr[h] + jnp.dot(vt, p.astype(bf16), preferred_element_type=f32)
            m_scr[h] = m_new

    @pl.when(j < i)
    def _():
        step(False)

    @pl.when(j == i)
    def _():
        step(True)
        _store_head_rows(ks_ref, k_ref)
        _store_head_rows(vs_ref, v_ref)
        lam = _diff_lambda(lamp_ref, lam_init)
        for h in range(N_HEADS):
            sl = slice(h * HEAD_DIM, (h + 1) * HEAD_DIM)
            on = acc_scr[h] / l_scr[h]
            o0 = on[:, :tq].T
            o1 = on[:, tq:].T
            o_ref[:, sl] = _diff_epilogue(o0, o1, lam, gain_ref[:, sl], z_ref[:, sl], lam_init)


def _diff_prompt(u, lamp, gain, lam_init, batch, seq):
    t = _divisor_tile(seq, C_BLOCK, 2 * CHUNK)
    nt = seq // t

    pairs = [(i, j) for i in range(nt) for j in range(i + 1)]
    q_tile = jnp.asarray([p[0] for p in pairs], jnp.int32)
    k_tile = jnp.asarray([p[1] for p in pairs], jnp.int32)

    def qblk(col):
        return pl.BlockSpec((t, BR_WIDTH), lambda b, p, qt, kt: (b * nt + qt[p], col // BR_WIDTH))

    def kblk(col):
        return pl.BlockSpec((t, BR_WIDTH), lambda b, p, qt, kt: (b * nt + kt[p], col // BR_WIDTH))

    state_shape = jax.ShapeDtypeStruct((batch, seq * N_HEADS, HEAD_DIM), f32)
    state_spec = pl.BlockSpec((None, t * N_HEADS, HEAD_DIM), lambda b, p, qt, kt: (b, qt[p], 0))
    return pl.pallas_call(
        functools.partial(_diff_prompt_body, lam_init=lam_init),
        out_shape=[jax.ShapeDtypeStruct((batch * seq, BR_WIDTH), bf16), state_shape, state_shape],
        grid_spec=pltpu.PrefetchScalarGridSpec(
            num_scalar_prefetch=2,
            grid=(batch, len(pairs)),
            in_specs=[qblk(U_CQ), kblk(U_CK), kblk(U_CV), qblk(U_CZ),
                      pl.BlockSpec((4, DIFF_DH), lambda b, p, qt, kt: (0, 0)),
                      pl.BlockSpec((1, BR_WIDTH), lambda b, p, qt, kt: (0, 0))],
            out_specs=[pl.BlockSpec((t, BR_WIDTH), lambda b, p, qt, kt: (b * nt + qt[p], 0)),
                       state_spec, state_spec],
            scratch_shapes=[pltpu.VMEM((N_HEADS, 1, 2 * t), f32),
                            pltpu.VMEM((N_HEADS, 1, 2 * t), f32),
                            pltpu.VMEM((N_HEADS, HEAD_DIM, 2 * t), f32)]),
        compiler_params=_params("parallel", "arbitrary"),
        name="diff_attn_prompt",
    )(q_tile, k_tile, u, u, u, u, lamp, gain.reshape(1, BR_WIDTH))


def _diff_sample_body(q_ref, k_ref, v_ref, z_ref, ck_ref, cv_ref, lamp_ref, gain_ref, o_ref, ks_ref, vs_ref,
                      *, lam_init):
    t = q_ref.shape[0]
    _store_head_rows(ks_ref, k_ref)
    _store_head_rows(vs_ref, v_ref)
    lo, hi = _half_masks()
    lam = _diff_lambda(lamp_ref, lam_init)
    q = _stack_heads(q_ref[...]) * DIFF_SCALE
    qq = jnp.concatenate([jnp.where(lo, q, 0.0), jnp.where(hi, q, 0.0)], axis=0).astype(bf16)
    o = _sample_attend(qq, t, ck_ref, cv_ref, k_ref[...], v_ref[...], 1.0)
    for h in range(N_HEADS):
        sl = slice(h * HEAD_DIM, (h + 1) * HEAD_DIM)
        o0 = o[h * t:(h + 1) * t]
        o1 = o[(N_HEADS + h) * t:(N_HEADS + h + 1) * t]
        o_ref[:, sl] = _diff_epilogue(o0, o1, lam, gain_ref[:, sl], z_ref[:, sl], lam_init)


CONV_PAD = 8


def _first_last(chunk_axis):
    if chunk_axis is None:
        def run(f):
            f()
        return run, run
    c = pl.program_id(chunk_axis)
    return pl.when(c == 0), pl.when(c == pl.num_programs(chunk_axis) - 1)


def _mlstm_body(qk_ref, v_ref, og_ref, z_ref, gc_ref, gr_ref, cw_ref, cb_ref, gbr_ref, gbc_ref, gn_ref,
                c0_ref, n0_ref, m0_ref, conv0_ref,
                out_ref, c_out_ref, n_out_ref, m_out_ref,
                cbuf, c_scr, n_scr, m_scr, *, chunk_axis=1):
    ln = qk_ref.shape[0]
    on_first, on_last = _first_last(chunk_axis)
    lo = CONV_PAD - (CONV_W - 1)

    @on_first
    def _():
        cbuf[lo:CONV_PAD, :] = conv0_ref[...]
        c_scr[...] = c0_ref[...]
        n_scr[...] = n0_ref[...]
        m_scr[...] = m0_ref[...]

    cbuf[CONV_PAD:CONV_PAD + ln, :] = qk_ref[...]
    conv = cb_ref[...] + cbuf[lo:lo + ln, :] * cw_ref[0:1, :]
    for jj in range(1, CONV_W):
        conv = conv + cbuf[lo + jj:lo + jj + ln, :] * cw_ref[jj:jj + 1, :]
    tail = cbuf[lo + ln:CONV_PAD + ln, :]
    cbuf[lo:CONV_PAD, :] = tail
    act = _silu(conv)

    gcb = gc_ref[...] + gbr_ref[...]
    grb = gr_ref[...] + gbc_ref[...]
    lfc = _log_sigmoid(gcb)
    lfr = _log_sigmoid(grb)
    row = lax.broadcasted_iota(jnp.int32, (ln, ln), 0)
    col = lax.broadcasted_iota(jnp.int32, (ln, ln), 1)
    causal = col <= row

    for h in range(N_HEADS):
        sl = slice(h * HEAD_DIM, (h + 1) * HEAD_DIM)
        q = act[:, h * HEAD_DIM:(h + 1) * HEAD_DIM]
        k = act[:, BR_WIDTH + h * HEAD_DIM:BR_WIDTH + (h + 1) * HEAD_DIM] * HEAD_SCALE
        v = v_ref[:, sl]
        i_col = gcb[:, h:h + 1]
        lf_col = lfc[:, N_HEADS + h:N_HEADS + h + 1]
        i_row = grb[h:h + 1, :]
        lf_row = lfr[N_HEADS + h:N_HEADS + h + 1, :]
        b_col = jnp.sum(jnp.where(causal, lf_row, 0.0), axis=1, keepdims=True)
        b_row = jnp.sum(jnp.where(row <= col, lf_col, 0.0), axis=0, keepdims=True)
        b_last = b_col[ln - 1:ln, :]
        m_old = m_scr[h][:, 0:1]
        c_old = c_scr[h]
        n_old = n_scr[h]

        dlog = jnp.where(causal, b_col - b_row + i_row, NEG)
        inter = b_col + m_old
        mt = jnp.maximum(inter, jnp.max(dlog, axis=1, keepdims=True))
        w = jnp.exp(dlog - mt)
        wi = jnp.exp(inter - mt)
        qb = q.astype(bf16)
        vb = v.astype(bf16)
        qk = lax.dot_general(qb, k.astype(bf16), NT_DIMS, preferred_element_type=f32) * w
        num = (jnp.dot(qk.astype(bf16), vb, preferred_element_type=f32)
               + wi * jnp.dot(qb, c_old.astype(bf16), preferred_element_type=f32))
        den = jnp.sum(qk, axis=1, keepdims=True) + wi * jnp.sum(q * n_old, axis=1, keepdims=True)
        hh = num / jnp.maximum(jnp.abs(den), jnp.exp(-mt))

        m_new = mt[ln - 1:ln, :]
        ws = jnp.exp(b_last - b_col + i_col - m_new)
        dec = jnp.exp(b_last + m_old - m_new)
        kw = k * ws
        c_scr[h] = dec * c_old + lax.dot_general(kw.astype(bf16), vb, TN_DIMS, preferred_element_type=f32)
        n_scr[h] = dec * n_old + jnp.sum(kw, axis=0, keepdims=True)
        m_scr[h] = jnp.broadcast_to(m_new, (1, HEAD_DIM))

        y = _head_norm(_sigmoid(og_ref[:, sl]) * hh, gn_ref[:, sl])
        out_ref[:, sl] = (y * _silu(z_ref[:, sl])).astype(bf16)

    @on_last
    def _():
        c_out_ref[...] = c_scr[...]
        n_out_ref[...] = n_scr[...]
        m_out_ref[...] = m_scr[...]


def _mlstm(u, gates_r, conv_w, conv_b, b_ig, b_fg, gn, c0, n0, m0, conv0, ng, nc, ln):
    gbias = jnp.concatenate([b_ig, b_fg]).astype(f32)
    gbr = jnp.zeros((1, GATE_PAD), f32).at[0, :2 * N_HEADS].set(gbias)
    gbc = gbias.reshape(2 * N_HEADS, 1)

    def blk(col, width):
        return pl.BlockSpec((ln, width), lambda g, c: (g * nc + c, col // width))

    def const(shape):
        return pl.BlockSpec(shape, lambda g, c: (0,) * len(shape))

    def per_seq(shape):
        return pl.BlockSpec((None,) + shape, lambda g, c: (g,) + (0,) * len(shape))

    state_shapes = [jax.ShapeDtypeStruct((ng, N_HEADS, HEAD_DIM, HEAD_DIM), f32),
                    jax.ShapeDtypeStruct((ng, N_HEADS, 1, HEAD_DIM), f32),
                    jax.ShapeDtypeStruct((ng, N_HEADS, 1, HEAD_DIM), f32)]
    state_specs = [per_seq((N_HEADS, HEAD_DIM, HEAD_DIM)),
                   per_seq((N_HEADS, 1, HEAD_DIM)),
                   per_seq((N_HEADS, 1, HEAD_DIM))]
    out, c_new, n_new, m_new = pl.pallas_call(
        _mlstm_body,
        out_shape=[jax.ShapeDtypeStruct((ng * nc * ln, BR_WIDTH), bf16)] + state_shapes,
        grid=(ng, nc),
        in_specs=[blk(U_BQK, B_QK), blk(U_BV, BR_WIDTH), blk(U_BO, BR_WIDTH), blk(U_BZ, BR_WIDTH),
                  blk(U_GATES, GATE_PAD),
                  pl.BlockSpec((None, 2 * N_HEADS, ln), lambda g, c: (g * nc + c, 0, 0)),
                  const((CONV_W, B_QK)), const((1, B_QK)), const((1, GATE_PAD)), const((2 * N_HEADS, 1)),
                  const((1, BR_WIDTH))] + state_specs + [per_seq((CONV_W - 1, B_QK))],
        out_specs=[pl.BlockSpec((ln, BR_WIDTH), lambda g, c: (g * nc + c, 0))] + state_specs,
        scratch_shapes=[pltpu.VMEM((CONV_PAD + ln, B_QK), f32),
                        pltpu.VMEM((N_HEADS, HEAD_DIM, HEAD_DIM), f32),
                        pltpu.VMEM((N_HEADS, 1, HEAD_DIM), f32),
                        pltpu.VMEM((N_HEADS, 1, HEAD_DIM), f32)],
        compiler_params=_params("parallel", "arbitrary"),
        name="mlstm",
    )(u, u, u, u, u, gates_r, conv_w, conv_b.reshape(1, B_QK), gbr, gbc, gn.reshape(1, BR_WIDTH),
      c0, n0.reshape(ng, N_HEADS, 1, HEAD_DIM),
      jnp.broadcast_to(m0[:, :, None, None], (ng, N_HEADS, 1, HEAD_DIM)), conv0)
    return out, c_new, n_new[:, :, 0, :], m_new[:, :, 0, 0]


def _ret_body(q_ref, k_ref, v_ref, z_ref, cc_ref, ss_ref, gn_ref, s0_ref, out_ref, s_out_ref, s_scr,
              *, chunk_axis=1):
    ln = q_ref.shape[0]
    on_first, on_last = _first_last(chunk_axis)

    @on_first
    def _():
        s_scr[...] = s0_ref[...]

    row = lax.broadcasted_iota(jnp.int32, (ln, ln), 0)
    col = lax.broadcasted_iota(jnp.int32, (ln, ln), 1)
    rel = (row - col).astype(f32)
    tpos = lax.broadcasted_iota(jnp.int32, (ln, 1), 0).astype(f32)
    cc = cc_ref[...]
    ss = ss_ref[...]
    for h in range(N_HEADS):
        sl = slice(h * HEAD_DIM, (h + 1) * HEAD_DIM)
        lg = math.log(1.0 - 2.0 ** (-RET_GAMMA_EXP0 - h))
        q = q_ref[:, sl]
        k = k_ref[:, sl]
        qr = q * cc + pltpu.roll(q, DIFF_DH, 1) * ss
        kr = (k * cc + pltpu.roll(k, DIFF_DH, 1) * ss) * HEAD_SCALE
        vb = v_ref[:, sl].astype(bf16)
        decay = jnp.where(rel >= 0.0, jnp.exp(jnp.maximum(rel, 0.0) * lg), 0.0)
        att = lax.dot_general(qr.astype(bf16), kr.astype(bf16), NT_DIMS, preferred_element_type=f32) * decay
        q_dec = qr * jnp.exp((tpos + 1.0) * lg)
        s_old = s_scr[h]
        o = (jnp.dot(att.astype(bf16), vb, preferred_element_type=f32)
             + jnp.dot(q_dec.astype(bf16), s_old.astype(bf16), preferred_element_type=f32))
        k_dec = kr * jnp.exp((ln - 1.0 - tpos) * lg)
        s_scr[h] = math.exp(ln * lg) * s_old + lax.dot_general(k_dec.astype(bf16), vb, TN_DIMS,
                                                               preferred_element_type=f32)
        out_ref[:, sl] = (_head_norm(o, gn_ref[:, sl]) * _silu(z_ref[:, sl])).astype(bf16)

    @on_last
    def _():
        s_out_ref[...] = s_scr[...]


def _rope_tables(pos):
    inv = ROPE_BASE ** (-jnp.arange(0, HEAD_DIM, 2, dtype=f32) / HEAD_DIM)
    ang = pos.astype(f32)[:, None] * inv[None, :]
    cos, sin = jnp.cos(ang), jnp.sin(ang)
    return jnp.concatenate([cos, cos], axis=-1), jnp.concatenate([-sin, sin], axis=-1)


def _retention(u, pos, gn, s0, ng, nc, ln):
    cc, ss = _rope_tables(pos)

    def blk(col):
        return pl.BlockSpec((ln, BR_WIDTH), lambda g, c: (g * nc + c, col // BR_WIDTH))

    rope_spec = pl.BlockSpec((ln, HEAD_DIM), lambda g, c: (c, 0))
    state_spec = pl.BlockSpec((None, N_HEADS, HEAD_DIM, HEAD_DIM), lambda g, c: (g, 0, 0, 0))
    return pl.pallas_call(
        _ret_body,
        out_shape=[jax.ShapeDtypeStruct((ng * nc * ln, BR_WIDTH), bf16),
                   jax.ShapeDtypeStruct((ng, N_HEADS, HEAD_DIM, HEAD_DIM), f32)],
        grid=(ng, nc),
        in_specs=[blk(U_DQ), blk(U_DK), blk(U_DV), blk(U_DZ), rope_spec, rope_spec,
                  pl.BlockSpec((1, BR_WIDTH), lambda g, c: (0, 0)), state_spec],
        out_specs=[pl.BlockSpec((ln, BR_WIDTH), lambda g, c: (g * nc + c, 0)), state_spec],
        scratch_shapes=[pltpu.VMEM((N_HEADS, HEAD_DIM, HEAD_DIM), f32)],
        compiler_params=_params("parallel", "arbitrary"),
        name="retention",
    )(u, u, u, u, cc, ss, gn.reshape(1, BR_WIDTH), s0)


def _sample_branches_body(*refs, lam_init):
    it = iter(refs)

    def take(n):
        return [next(it) for _ in range(n)]

    a_in, c_in, m_in, b_in, d_in = take(8), take(8), take(4), take(15), take(8)
    a_out, c_out, m_out, b_out, d_out = take(3), take(3), take(1), take(4), take(2)
    b_scr, d_scr = take(4), take(1)
    _band_sample_body(*a_in, *a_out)
    _diff_sample_body(*c_in, *c_out, lam_init=lam_init)
    _mem_sample_body(*m_in, *m_out)
    _mlstm_body(*b_in, *b_out, *b_scr, chunk_axis=None)
    _ret_body(*d_in, *d_out, *d_scr, chunk_axis=None)


def _sample_branches(u, layer, caches, rel_bias, lamp, lam_init, gains, conv_w, conv_b, b_ig, b_fg,
                     states, pos, nb, t):
    ca_k, ca_v, cc_k, cc_v, cm_k, cm_v = caches
    subln_c, gn_b, gn_d = gains
    c0, n0, m0, conv0, s0 = states
    nrow = ca_k.shape[2] // N_HEADS
    bias_c = _rel_bias_toeplitz(rel_bias, t, nrow, nrow)
    bias_c = jnp.repeat(bias_c, N_HEADS, axis=2).reshape(N_HEADS * t, nrow * N_HEADS)
    bias_n = jnp.tile(_rel_bias_toeplitz(rel_bias, t, t, 0), (1, 1, N_HEADS)).reshape(N_HEADS * t, N_HEADS * t)
    gbias = jnp.concatenate([b_ig, b_fg]).astype(f32)
    gbr = jnp.zeros((1, GATE_PAD), f32).at[0, :2 * N_HEADS].set(gbias)
    cc, ss = _rope_tables(pos)

    def blk(col, width=BR_WIDTH):
        return pl.BlockSpec((t, width), lambda b: (b, col // width))

    def const(shape):
        return pl.BlockSpec(shape, lambda b: (0,) * len(shape))

    def per_seq(shape):
        return pl.BlockSpec((None,) + shape, lambda b: (b,) + (0,) * len(shape))

    def cache(arr):
        return pl.BlockSpec((None, None) + arr.shape[2:], lambda b: (layer, b, 0, 0))

    mat, vec = (N_HEADS, HEAD_DIM, HEAD_DIM), (N_HEADS, 1, HEAD_DIM)
    kv_state = (t * N_HEADS, HEAD_DIM)
    in_specs = (
        [blk(U_AQ), blk(U_AK), blk(U_AV), blk(U_AZ), cache(ca_k), cache(ca_v),
         const((N_HEADS * t, nrow * N_HEADS)), const((N_HEADS * t, N_HEADS * t))]
        + [blk(U_CQ), blk(U_CK), blk(U_CV), blk(U_CZ), cache(cc_k), cache(cc_v),
           const((4, DIFF_DH)), const((1, BR_WIDTH))]
        + [blk(U_MQ), blk(U_MZ), cache(cm_k), cache(cm_v)]
        + [blk(U_BQK, B_QK), blk(U_BV), blk(U_BO), blk(U_BZ), blk(U_GATES, GATE_PAD),
           per_seq((2 * N_HEADS, t)), const((CONV_W, B_QK)), const((1, B_QK)), const((1, GATE_PAD)),
           const((2 * N_HEADS, 1)), const((1, BR_WIDTH)), per_seq(mat), per_seq(vec), per_seq(vec),
           per_seq((CONV_W - 1, B_QK))]
        + [blk(U_DQ), blk(U_DK), blk(U_DV), blk(U_DZ), const((t, HEAD_DIM)), const((t, HEAD_DIM)),
           const((1, BR_WIDTH)), per_seq(mat)])
    args = (
        [u, u, u, u, ca_k, ca_v, bias_c, bias_n]
        + [u, u, u, u, cc_k, cc_v, lamp, subln_c.reshape(1, BR_WIDTH)]
        + [u, u, cm_k, cm_v]
        + [u, u, u, u, u, _gates_rowform(u, t), conv_w, conv_b.reshape(1, B_QK), gbr,
           gbias.reshape(2 * N_HEADS, 1), gn_b.reshape(1, BR_WIDTH), c0,
           n0.reshape(nb, N_HEADS, 1, HEAD_DIM),
           jnp.broadcast_to(m0[:, :, None, None], (nb, N_HEADS, 1, HEAD_DIM)), conv0]
        + [u, u, u, u, cc, ss, gn_d.reshape(1, BR_WIDTH), s0])
    branch_out = jax.ShapeDtypeStruct((nb * t, BR_WIDTH), bf16)
    kv_out = jax.ShapeDtypeStruct((nb,) + kv_state, f32)
    mat_out = jax.ShapeDtypeStruct((nb,) + mat, f32)
    vec_out = jax.ShapeDtypeStruct((nb,) + vec, f32)
    out_shape = ([branch_out, kv_out, kv_out] * 2 + [branch_out]
                 + [branch_out, mat_out, vec_out, vec_out] + [branch_out, mat_out])
    o_spec = pl.BlockSpec((t, BR_WIDTH), lambda b: (b, 0))
    out_specs = ([o_spec, per_seq(kv_state), per_seq(kv_state)] * 2 + [o_spec]
                 + [o_spec, per_seq(mat), per_seq(vec), per_seq(vec)] + [o_spec, per_seq(mat)])
    (oa, ak, av, oc, ck, cv, om, ob, c_new, n_new, m_new, od, s_new) = pl.pallas_call(
        functools.partial(_sample_branches_body, lam_init=lam_init),
        out_shape=out_shape,
        grid=(nb,),
        in_specs=in_specs,
        out_specs=out_specs,
        scratch_shapes=[pltpu.VMEM((CONV_PAD + t, B_QK), f32), pltpu.VMEM(mat, f32),
                        pltpu.VMEM(vec, f32), pltpu.VMEM(vec, f32), pltpu.VMEM(mat, f32)],
        compiler_params=_params("arbitrary"),
        name="sample_branches",
    )(*args)
    return ((oa, ob, oc, od, om), (ak, av, ck, cv),
            (c_new, n_new[:, :, 0, :], m_new[:, :, 0, 0], s_new))


def _merge_body(x_ref, h_ref, oa_ref, ob_ref, oc_ref, od_ref, om_ref,
                wg0_ref, wg1_ref, wg2_ref, wg3_ref, wg4_ref, wb_ref, wo_ref, fg_ref,
                wo_last_ref, y_ref, mrg_scr, *, rows, final):
    n = pl.program_id(1)

    @pl.when(n == 0)
    def _():
        y_ref[...] = jnp.zeros(y_ref.shape, f32)
        mrg_scr[...] = jnp.zeros(mrg_scr.shape, bf16)

    y_ref[...] += jnp.dot(mrg_scr[...], wo_ref[...], preferred_element_type=f32)
    h = h_ref[...]
    merged = None
    for i, (o_ref, wg_ref) in enumerate(zip((oa_ref, ob_ref, oc_ref, od_ref, om_ref),
                                            (wg0_ref, wg1_ref, wg2_ref, wg3_ref, wg4_ref))):
        gate = _sigmoid(lax.dot_general(h, wg_ref[...], NT_DIMS, preferred_element_type=f32))
        term = gate * jnp.dot(o_ref[...], wb_ref[i], preferred_element_type=f32)
        merged = term if merged is None else merged + term
    mrg_scr[...] = merged.astype(bf16)

    @pl.when(n == pl.num_programs(1) - 1)
    def _():
        y_ref[...] += jnp.dot(mrg_scr[...], wo_last_ref[...], preferred_element_type=f32)
        tm = x_ref.shape[0]
        for r in range(0, tm, rows):
            y = x_ref[r:r + rows, :] + y_ref[r:r + rows, :]
            if final:
                y = (y * lax.rsqrt(jnp.mean(y * y, axis=-1, keepdims=True) + EPS)) * fg_ref[...]
            y_ref[r:r + rows, :] = y


def _merge(x, h, outs, wg, wb, wo, fg, *, final, name, tm_target=704, tn=256):
    m, d = x.shape
    tm = _divisor_tile(m, tm_target, 16)
    rows = _divisor_tile(tm, 256, 8)
    nn = d // tn

    def wg_spec(i):
        return pl.BlockSpec((tn, d), lambda r, n: (i * nn + n, 0))

    o_spec = pl.BlockSpec((tm, BR_WIDTH), lambda r, n: (r, 0))
    return pl.pallas_call(
        functools.partial(_merge_body, rows=rows, final=final),
        out_shape=jax.ShapeDtypeStruct((m, d), f32),
        grid=(m // tm, nn),
        in_specs=[pl.BlockSpec((tm, d), lambda r, n: (r, 0)),
                  pl.BlockSpec((tm, d), lambda r, n: (r, 0))]
                 + [o_spec] * N_BRANCH
                 + [wg_spec(i) for i in range(N_BRANCH)]
                 + [pl.BlockSpec((N_BRANCH, BR_WIDTH, tn), lambda r, n: (0, 0, n)),
                    pl.BlockSpec((tn, d), lambda r, n: (jnp.maximum(n - 1, 0), 0)),
                    pl.BlockSpec((1, d), lambda r, n: (0, 0)),
                    pl.BlockSpec((tn, d), lambda r, n: (nn - 1, 0))],
        out_specs=pl.BlockSpec((tm, d), lambda r, n: (r, 0)),
        scratch_shapes=[pltpu.VMEM((tm, tn), bf16)],
        compiler_params=_params("parallel", "arbitrary"),
        name=name,
    )(x, h, *outs, wg, wg, wg, wg, wg, wb, wo, fg.reshape(1, d), wo)


F32_SUBLANES = 8


def _cast_rows_body(src_ref, o_ref, *, valid_rows_last):
    tr = o_ref.shape[0]
    x = src_ref[0]
    if valid_rows_last is not None:
        row = lax.broadcasted_iota(jnp.int32, (tr, 1), 0)
        keep = jnp.logical_or(pl.program_id(0) < pl.num_programs(0) - 1, row < valid_rows_last)
        x = jnp.where(keep, x, 0.0)
    o_ref[...] = x.astype(bf16)


def _cast_rows(w_t, layer, *, n_rows, tr, src_row, valid_rows_last, name):
    d = w_t.shape[2]
    assert n_rows % tr == 0
    return pl.pallas_call(
        functools.partial(_cast_rows_body, valid_rows_last=valid_rows_last),
        out_shape=jax.ShapeDtypeStruct((n_rows, d), bf16),
        grid=(n_rows // tr,),
        in_specs=[pl.BlockSpec((pl.Element(1), pl.Element(tr), pl.Element(d)),
                               lambda j: (layer, pl.multiple_of(src_row(j), F32_SUBLANES), 0))],
        out_specs=pl.BlockSpec((tr, d), lambda j: (j, 0)),
        compiler_params=_params("arbitrary"),
        name=name,
    )(w_t)


def _repack_w_in(w_t, layer):
    tr = MXU_COLS
    n_plain, n_main = W_BI // tr, U_GATES // tr

    def u_src(j):
        return jnp.where(j < n_plain, j * tr, jnp.where(j < n_main, j * tr + (W_BZ - W_BI), W_BI))

    wu = _cast_rows(w_t, layer, n_rows=U_WIDTH, tr=tr, src_row=u_src, valid_rows_last=2 * N_HEADS,
                    name="repack_u")
    wg = _cast_rows(w_t, layer, n_rows=N_BRANCH * D_MODEL, tr=512, src_row=lambda j: W_GATE + j * 512,
                    valid_rows_last=None, name="repack_g")
    return wu, wg


def _gates_rowform(u, ln):
    g = u[:, U_GATES:U_GATES + 2 * N_HEADS]
    return g.reshape(u.shape[0] // ln, ln, 2 * N_HEADS).transpose(0, 2, 1)


def kernel(x_prompt, x_sample, mem_prompt, cache_a_k, cache_a_v, cache_c_k, cache_c_v, cache_mem_k, cache_mem_v, state_b_C, state_b_n, state_b_m, state_b_conv, state_d_S, norm_g, w_in, conv_w, conv_b, b_ig, b_fg, rel_bias, lam_q1, lam_k1, lam_q2, lam_k2, gn_b, subln_c, gn_d, mem_norm_g, w_mk, w_mv, w_branch, w_out, final_g):
    batch, seq, d = x_prompt.shape
    nb, t, _ = x_sample.shape
    depth = w_in.shape[0]
    n_mem = mem_prompt.shape[1]
    past = cache_c_k.shape[2]
    mp = batch * seq
    assert d == D_MODEL and seq % C_BLOCK == 0 and seq % A_QBLOCK == 0 and seq % SCAN_CHUNK == 0

    xp = x_prompt.reshape(mp, d)
    xs = x_sample.reshape(nb * t, d)
    mem = mem_prompt.reshape(batch * n_mem, d)
    ca_k, ca_v = _cache_rows(cache_a_k), _cache_rows(cache_a_v)
    cc_k, cc_v = _cache_rows(cache_c_k), _cache_rows(cache_c_v)
    cm_k, cm_v = _cache_rows(cache_mem_k), _cache_rows(cache_mem_v)
    pos_p = jnp.arange(seq)
    pos_s = past + jnp.arange(t)
    a_rows = min(A_WINDOW, seq)
    mem_tq = _divisor_tile(seq, 512, 16)

    w_t = jnp.swapaxes(w_in, 1, 2)

    sp, ss = [], []
    for l in range(depth):
        lam_init = 0.8 - 0.6 * math.exp(-0.3 * l)
        lamp = jnp.stack([lam_q1[l], lam_k1[l], lam_q2[l], lam_k2[l]]).astype(f32)
        wu, wg = _repack_w_in(w_t, l)
        u, hp = _norm_matmul(xp, norm_g[l], wu, tm_target=1024, tn=U_TILE, w_rows_are_outputs=True,
                             name="in_proj")
        us, hs = _norm_matmul(xs, norm_g[l], wu, tm_target=1024, tn=U_TILE, w_rows_are_outputs=True,
                              name="in_proj_sample")
        mkv, _ = _norm_matmul(mem, mem_norm_g[l], jnp.concatenate([w_mk[l], w_mv[l]], axis=1).astype(bf16),
                              tm_target=512, tn=512, w_rows_are_outputs=False, name="mem_proj")

        oa_p, ak_p, av_p = _band_prompt(u, rel_bias[l], batch, seq, a_rows)
        zeros_c = jnp.zeros((batch, N_HEADS, HEAD_DIM, HEAD_DIM), f32)
        ob_p, c_p, n_p, m_p = _mlstm(
            u, _gates_rowform(u, SCAN_CHUNK), conv_w[l], conv_b[l], b_ig[l], b_fg[l], gn_b[l],
            zeros_c, jnp.zeros((batch, N_HEADS, HEAD_DIM), f32), jnp.zeros((batch, N_HEADS), f32),
            jnp.zeros((batch, CONV_W - 1, B_QK), f32), batch, seq // SCAN_CHUNK, SCAN_CHUNK)
        oc_p, ck_p, cv_p = _diff_prompt(u, lamp, subln_c[l], lam_init, batch, seq)
        od_p, s_p = _retention(u, pos_p, gn_d[l], zeros_c, batch, seq // SCAN_CHUNK, SCAN_CHUNK)
        om_p = _mem_attn_prompt(u, mkv, batch, seq, mem_tq, n_mem)

        outs_s, (ak_s, av_s, ck_s, cv_s), (c_s, n_s, m_s, s_s) = _sample_branches(
            us, l, (ca_k, ca_v, cc_k, cc_v, cm_k, cm_v), rel_bias[l], lamp, lam_init,
            (subln_c[l], gn_b[l], gn_d[l]), conv_w[l], conv_b[l], b_ig[l], b_fg[l],
            (state_b_C[l].astype(f32), state_b_n[l].astype(f32), state_b_m[l].astype(f32),
             state_b_conv[l].astype(f32), state_d_S[l].astype(f32)), pos_s, nb, t)

        wb, wo = w_branch[l].astype(bf16), w_out[l].astype(bf16)
        final = l == depth - 1
        xp = _merge(xp, hp, (oa_p, ob_p, oc_p, od_p, om_p), wg, wb, wo, final_g, final=final,
                    name="gated_merge")
        xs = _merge(xs, hs, outs_s, wg, wb, wo, final_g, final=final, name="gated_merge_sample")

        def conv_tail(arr, n_seq, rows):
            return jnp.stack([arr[(i + 1) * rows - (CONV_W - 1):(i + 1) * rows, U_BQK:U_BQK + B_QK]
                              for i in range(n_seq)])

        def hd(a):
            return a.reshape(a.shape[:2] + (N_HEADS, HEAD_DIM))

        def rows_hd(a):
            return a.reshape(a.shape[0], a.shape[1] // N_HEADS, N_HEADS, HEAD_DIM)

        sp.append((rows_hd(ak_p), rows_hd(av_p), rows_hd(ck_p), rows_hd(cv_p),
                   hd(mkv[:, :BR_WIDTH].reshape(batch, n_mem, BR_WIDTH)),
                   hd(mkv[:, BR_WIDTH:].reshape(batch, n_mem, BR_WIDTH)),
                   c_p, n_p, m_p, conv_tail(u, batch, seq), s_p))
        ss.append((rows_hd(ak_s), rows_hd(av_s), rows_hd(ck_s), rows_hd(cv_s),
                   c_s, n_s, m_s, us[:, U_BQK:U_BQK + B_QK].reshape(nb, t, B_QK)[:, t - (CONV_W - 1):], s_s))

    y_prompt = xp.reshape(batch, seq, d)
    y_sample = xs.reshape(nb, t, d)
    p_states = tuple(jnp.stack([st[i] for st in sp]) for i in range(11))
    s_states = tuple(jnp.stack([st[i] for st in ss]) for i in range(9))
    return (y_prompt, y_sample) + p_states + s_states
```
